```python
import math
import jax, jax.numpy as jnp
from jax import lax
import numpy as np

D_MODEL = 1024
BATCH = 8
SEQ = 4096
DEPTH = 4

N_CONV_LAYERS = DEPTH // 2
N_ATTN_LAYERS = DEPTH - N_CONV_LAYERS

CONV_WIDTH = 31

WINDOW_DILATIONS = ((128, 1), (512, 4), (2048, 16))
N_GROUPS = len(WINDOW_DILATIONS)
HEADS_PER_GROUP = 8
HEAD_DIM = 128
GROUP_WIDTH = HEADS_PER_GROUP * HEAD_DIM
Q_WIDTH = N_GROUPS * GROUP_WIDTH
ATTN_OUT_WIDTH = HEADS_PER_GROUP * HEAD_DIM
ATTN_BLOCK = 128
ROPE_THETA = 500000.0
ROT_DIM = HEAD_DIM // 4
NEG_INF = -1e30

N_EXPERTS = 64
TOP_K = 8
EXPERT_DIM = 256
SHARED_DIM = 256
ROUTED_SCALE = 2.5
MOE_ROWS = 128

DEEPNORM_ALPHA = (2.0 * DEPTH) ** 0.25
DEEPNORM_BETA = (8.0 * DEPTH) ** -0.25
LN_EPS = 1e-5

kernel_name = "yoco_conformer_dilated_attn_moe"


def layernorm(x, g, b):
    x32 = x.astype(jnp.float32)
    mu = jnp.mean(x32, axis=-1, keepdims=True)
    var = jnp.mean(jnp.square(x32 - mu), axis=-1, keepdims=True)
    y = (x32 - mu) * lax.rsqrt(var + LN_EPS) * g.astype(jnp.float32) + b.astype(jnp.float32)
    return y.astype(x.dtype)


def rotary(t, positions):
    half = ROT_DIM // 2
    inv_freq = ROPE_THETA ** (-jnp.arange(half, dtype=jnp.float32) * 2.0 / ROT_DIM)
    ang = positions.astype(jnp.float32)[..., None] * inv_freq
    cos = jnp.cos(ang)[:, :, None, None, :]
    sin = jnp.sin(ang)[:, :, None, None, :]
    tr = t[..., :ROT_DIM].astype(jnp.float32)
    x1, x2 = tr[..., :half], tr[..., half:]
    rot = jnp.concatenate([x1 * cos - x2 * sin, x2 * cos + x1 * sin], axis=-1)
    return jnp.concatenate([rot.astype(t.dtype), t[..., ROT_DIM:]], axis=-1)


def conv_module(x, w_in, b_in, w_dw, b_dw, ln_g, ln_b, w_out, b_out):
    h = x @ w_in + b_in
    a, g = jnp.split(h, 2, axis=-1)
    h = a * jax.nn.sigmoid(g)
    h = lax.conv_general_dilated(
        h, w_dw[:, None, :].astype(h.dtype), window_strides=(1,),
        padding=[(CONV_WIDTH - 1, 0)], dimension_numbers=("NWC", "WIO", "NWC"),
        feature_group_count=D_MODEL) + b_dw
    h = jax.nn.silu(layernorm(h, ln_g, ln_b))
    return h @ w_out + b_out


def dilated_window_attention(q, k, v, dil, steps):
    B, S, H, Dh = q.shape
    L = S // dil
    nb = -(-L // ATTN_BLOCK)
    Lp = nb * ATTN_BLOCK

    def blocks(t):
        t = t.reshape(B, L, dil, H, Dh).transpose(0, 2, 1, 3, 4)
        t = jnp.pad(t, ((0, 0), (0, 0), (0, Lp - L), (0, 0), (0, 0)))
        return t.reshape(B, dil, nb, ATTN_BLOCK, H, Dh)

    def with_prev(t):
        prev = jnp.pad(t[:, :, :-1], ((0, 0), (0, 0), (1, 0), (0, 0), (0, 0), (0, 0)))
        return jnp.concatenate([prev, t], axis=3)

    qb = blocks(q)
    kc = with_prev(blocks(k))
    vc = with_prev(blocks(v))
    s = jnp.einsum("bgnqhd,bgnkhd->bgnhqk", qb, kc,
                   preferred_element_type=jnp.float32) * (1.0 / math.sqrt(Dh))
    qi = jnp.arange(ATTN_BLOCK)[:, None]
    kc_idx = jnp.arange(2 * ATTN_BLOCK)[None, :]
    dist = qi + ATTN_BLOCK - kc_idx
    band = (dist >= 0) & (dist <= steps)
    blk = jnp.arange(nb)[:, None, None]
    valid = band[None] & ((blk > 0) | (kc_idx[None] >= ATTN_BLOCK))
    s = jnp.where(valid[None, None, :, None], s, NEG_INF)
    m = jnp.max(s, axis=-1, keepdims=True)
    p = jnp.exp(s - m)
    l = jnp.sum(p, axis=-1)
    o = jnp.einsum("bgnhqk,bgnkhd->bgnhqd", p, vc.astype(jnp.float32)) / l[..., None]
    lse = m[..., 0] + jnp.log(l)
    o = o.transpose(0, 1, 2, 4, 3, 5).reshape(B, dil, Lp, H, Dh)[:, :, :L]
    o = o.transpose(0, 2, 1, 3, 4).reshape(B, S, H, Dh)
    lse = lse.transpose(0, 1, 2, 4, 3).reshape(B, dil, Lp, H)[:, :, :L]
    lse = lse.transpose(0, 2, 1, 3).reshape(B, S, H)
    return o, lse


def dilated_mixture_attention(x, positions, w_q, w_o, k, v):
    B, S, _ = x.shape
    q = rotary((x @ w_q).reshape(B, S, N_GROUPS, HEADS_PER_GROUP, HEAD_DIM), positions)
    outs, lses = [], []
    for g, (win, dil) in enumerate(WINDOW_DILATIONS):
        steps = win // dil
        assert steps <= ATTN_BLOCK
        o, lse = dilated_window_attention(q[:, :, g], k[:, :, g], v[:, :, g], dil, steps)
        outs.append(o)
        lses.append(lse)
    wts = jax.nn.softmax(jnp.stack(lses, axis=2), axis=2)
    o = jnp.einsum("bsgh,bsghd->bshd", wts, jnp.stack(outs, axis=2))
    return o.astype(x.dtype).reshape(B, S, ATTN_OUT_WIDTH) @ w_o


def moe_ffn(x, router_w, router_bias, w_gate, w_up, w_down, sw_gate, sw_up, sw_down):
    B, S, D = x.shape
    T = B * S
    xt = x.reshape(T, D)
    scores = jax.nn.sigmoid(jnp.dot(xt.astype(jnp.float32), router_w.astype(jnp.float32)))
    _, idx = lax.top_k(scores + router_bias.astype(jnp.float32), TOP_K)
    gate = jnp.take_along_axis(scores, idx, axis=-1)
    gate = gate / jnp.sum(gate, axis=-1, keepdims=True) * ROUTED_SCALE
    N = T * TOP_K
    e_flat = idx.reshape(N)
    tok_flat = jnp.arange(N, dtype=jnp.int32) // TOP_K
    g_flat = gate.reshape(N)
    order = jnp.argsort(e_flat)
    e_sorted = e_flat[order]
    counts = jnp.bincount(e_flat, length=N_EXPERTS)
    padded = (counts + MOE_ROWS - 1) // MOE_ROWS * MOE_ROWS
    starts = jnp.cumsum(counts) - counts
    pstarts = jnp.cumsum(padded) - padded
    dest = pstarts[e_sorted] + jnp.arange(N, dtype=jnp.int32) - starts[e_sorted]
    n_blocks = -(-N // MOE_ROWS) + N_EXPERTS
    P = n_blocks * MOE_ROWS
    row_tok = jnp.zeros((P,), jnp.int32).at[dest].set(tok_flat[order])
    row_gate = jnp.zeros((P,), jnp.float32).at[dest].set(g_flat[order])
    pends = jnp.cumsum(padded)
    blk_start = jnp.arange(n_blocks) * MOE_ROWS
    blk_exp = jnp.minimum(jnp.sum(pends[None, :] <= blk_start[:, None], axis=1), N_EXPERTS - 1)

    def expert_block(args):
        tok, e = args
        xb = xt[tok]
        h = jax.nn.silu(xb @ w_gate[e]) * (xb @ w_up[e])
        return h @ w_down[e]

    y = lax.map(expert_block, (row_tok.reshape(n_blocks, MOE_ROWS), blk_exp))
    y = y.reshape(P, D) * row_gate[:, None].astype(x.dtype)
    routed = jax.ops.segment_sum(y, row_tok, num_segments=T)
    shared = (jax.nn.silu(xt @ sw_gate) * (xt @ sw_up)) @ sw_down
    return (routed + shared).reshape(B, S, D)


def setup_inputs(seed: int = 0) -> dict:
    key = jax.random.key(seed)
    ks = iter(jax.random.split(key, 32))
    f32 = jnp.float32

    def nrm(shape, scale):
        return jax.random.normal(next(ks), shape, f32) * scale

    D = D_MODEL
    nA, nB = N_CONV_LAYERS, N_ATTN_LAYERS
    return {
        "x": nrm((BATCH, SEQ, D), 1.0),
        "positions": jnp.broadcast_to(jnp.arange(SEQ, dtype=jnp.int32), (BATCH, SEQ)),
        "ln1_g": 1.0 + nrm((DEPTH, D), 0.02),
        "ln1_b": nrm((DEPTH, D), 0.02),
        "ln2_g": 1.0 + nrm((DEPTH, D), 0.02),
        "ln2_b": nrm((DEPTH, D), 0.02),
        "conv_w_in": nrm((nA, D, 2 * D), D ** -0.5),
        "conv_b_in": nrm((nA, 2 * D), 0.02),
        "conv_w_dw": nrm((nA, CONV_WIDTH, D), CONV_WIDTH ** -0.5),
        "conv_b_dw": nrm((nA, D), 0.02),
        "conv_ln_g": 1.0 + nrm((nA, D), 0.02),
        "conv_ln_b": nrm((nA, D), 0.02),
        "conv_w_out": nrm((nA, D, D), D ** -0.5 * DEEPNORM_BETA),
        "conv_b_out": nrm((nA, D), 0.02),
        "w_kv": nrm((D, 2 * Q_WIDTH), D ** -0.5),
        "attn_w_q": nrm((nB, D, Q_WIDTH), D ** -0.5),
        "attn_w_o": nrm((nB, ATTN_OUT_WIDTH, D), ATTN_OUT_WIDTH ** -0.5 * DEEPNORM_BETA),
        "router_w": nrm((DEPTH, D, N_EXPERTS), D ** -0.5),
        "router_bias": nrm((DEPTH, N_EXPERTS), 0.01),
        "exp_w_gate": nrm((DEPTH, N_EXPERTS, D, EXPERT_DIM), D ** -0.5),
        "exp_w_up": nrm((DEPTH, N_EXPERTS, D, EXPERT_DIM), D ** -0.5),
        "exp_w_down": nrm((DEPTH, N_EXPERTS, EXPERT_DIM, D), EXPERT_DIM ** -0.5 * DEEPNORM_BETA),
        "sh_w_gate": nrm((DEPTH, D, SHARED_DIM), D ** -0.5),
        "sh_w_up": nrm((DEPTH, D, SHARED_DIM), D ** -0.5),
        "sh_w_down": nrm((DEPTH, SHARED_DIM, D), SHARED_DIM ** -0.5 * DEEPNORM_BETA),
    }


def reference(x, positions, ln1_g, ln1_b, ln2_g, ln2_b, conv_w_in, conv_b_in, conv_w_dw,
              conv_b_dw, conv_ln_g, conv_ln_b, conv_w_out, conv_b_out, w_kv, attn_w_q,
              attn_w_o, router_w, router_bias, exp_w_gate, exp_w_up, exp_w_down,
              sh_w_gate, sh_w_up, sh_w_down):
    B, S, _ = x.shape
    k_shared = None
    v_shared = None
    for layer in range(DEPTH):
        if layer < N_CONV_LAYERS:
            mix = conv_module(x, conv_w_in[layer], conv_b_in[layer], conv_w_dw[layer],
                              conv_b_dw[layer], conv_ln_g[layer], conv_ln_b[layer],
                              conv_w_out[layer], conv_b_out[layer])
        else:
            if layer == N_CONV_LAYERS:
                kv = (x @ w_kv).reshape(B, S, 2, N_GROUPS, HEADS_PER_GROUP, HEAD_DIM)
                k_shared = rotary(kv[:, :, 0], positions)
                v_shared = kv[:, :, 1]
            j = layer - N_CONV_LAYERS
            mix = dilated_mixture_attention(x, positions, attn_w_q[j], attn_w_o[j],
                                            k_shared, v_shared)
        x = layernorm(DEEPNORM_ALPHA * x + mix, ln1_g[layer], ln1_b[layer])
        ffn = moe_ffn(x, router_w[layer], router_bias[layer], exp_w_gate[layer],
                      exp_w_up[layer], exp_w_down[layer], sh_w_gate[layer],
                      sh_w_up[layer], sh_w_down[layer])
        x = layernorm(DEEPNORM_ALPHA * x + ffn, ln2_g[layer], ln2_b[layer])
    return x
```

```python
import functools
import math

import jax
import jax.numpy as jnp
from jax import lax
from jax.experimental import pallas as pl
from jax.experimental.pallas import tpu as pltpu

F32 = jnp.float32
BF16 = jnp.bfloat16
I32 = jnp.int32

LANES = 128
HEAD_DIM = 128
ROT_DIM = HEAD_DIM // 4
ROPE_THETA = 500000.0
ATTN_BLOCK = 128
WINDOW_DILATIONS = ((128, 1), (512, 4), (2048, 16))
NEG_INF = -1e30
TOP_K = 8
ROUTED_SCALE = 2.5
LN_EPS = 1e-5
CONV_HALO = 32
VMEM_LIMIT = 56 * 1024 * 1024


def _cparams(sem):
    return pltpu.CompilerParams(dimension_semantics=sem, vmem_limit_bytes=VMEM_LIMIT)


def _ln(y, g, b):
    mu = jnp.mean(y, axis=-1, keepdims=True)
    d = y - mu
    var = jnp.mean(d * d, axis=-1, keepdims=True)
    return d * lax.rsqrt(var + LN_EPS) * g + b


def _full(shape):
    n = len(shape)
    return pl.BlockSpec(shape, lambda *_: (0,) * n)


def _glu_kernel(x_ref, w_ref, b_ref, o_ref):
    d = o_ref.shape[-1]
    h = jnp.dot(x_ref[...].astype(BF16), w_ref[...], preferred_element_type=F32) + b_ref[...]
    o_ref[...] = h[:, :d] * jax.nn.sigmoid(h[:, d:])


def _glu(xt, w_in, b_in, tm):
    t, d = xt.shape
    return pl.pallas_call(
        _glu_kernel,
        grid=(t // tm,),
        in_specs=[pl.BlockSpec((tm, d), lambda i: (i, 0)), _full((d, 2 * d)), _full((1, 2 * d))],
        out_specs=pl.BlockSpec((tm, d), lambda i: (i, 0)),
        out_shape=jax.ShapeDtypeStruct((t, d), F32),
        compiler_params=_cparams(("arbitrary",)),
        name="conv_glu",
    )(xt, w_in, b_in)


def _dwconv_kernel(h_ref, w_ref, b_ref, o_ref, buf, *, rows):
    s = pl.program_id(2)
    tm = h_ref.shape[0]
    width = w_ref.shape[0]
    off = CONV_HALO - (width - 1)

    @pl.when(s == 0)
    def _():
        buf[0:CONV_HALO, :] = jnp.zeros((CONV_HALO, buf.shape[1]), F32)

    @pl.when(s > 0)
    def _():
        buf[0:CONV_HALO, :] = buf[tm:tm + CONV_HALO, :]

    buf[CONV_HALO:CONV_HALO + tm, :] = h_ref[...]
    for r0 in range(0, tm, rows):
        acc = None
        for k in range(width):
            term = buf[r0 + off + k:r0 + off + k + rows, :] * w_ref[k:k + 1, :]
            acc = term if acc is None else acc + term
        o_ref[r0:r0 + rows, :] = acc + b_ref[...]


def _dwconv(h, w_dw, b_dw, b, s, tm, cw):
    t, d = h.shape
    h3 = h.reshape(b, s, d)
    width = w_dw.shape[0]
    out = pl.pallas_call(
        functools.partial(_dwconv_kernel, rows=64),
        grid=(b, d // cw, s // tm),
        in_specs=[pl.BlockSpec((None, tm, cw), lambda bi, c, si: (bi, si, c)),
                  pl.BlockSpec((width, cw), lambda bi, c, si: (0, c)),
                  pl.BlockSpec((1, cw), lambda bi, c, si: (0, c))],
        out_specs=pl.BlockSpec((None, tm, cw), lambda bi, c, si: (bi, si, c)),
        out_shape=jax.ShapeDtypeStruct((b, s, d), F32),
        scratch_shapes=[pltpu.VMEM((CONV_HALO + tm, cw), F32)],
        compiler_params=_cparams(("arbitrary", "arbitrary", "arbitrary")),
        name="conv_depthwise",
    )(h3, w_dw, b_dw)
    return out.reshape(t, d)


def _conv_out_kernel(c_ref, x_ref, cg_ref, cb_ref, w_ref, bo_ref, g_ref, b_ref, o_ref, *, alpha):
    u = _ln(c_ref[...], cg_ref[...], cb_ref[...])
    u = u * jax.nn.sigmoid(u)
    mix = jnp.dot(u.astype(BF16), w_ref[...], preferred_element_type=F32) + bo_ref[...]
    o_ref[...] = _ln(alpha * x_ref[...] + mix, g_ref[...], b_ref[...])


def _conv_out(c, xt, cg, cb, w_out, b_out, g1, b1, alpha, tm):
    t, d = xt.shape
    row = pl.BlockSpec((tm, d), lambda i: (i, 0))
    vec = _full((1, d))
    return pl.pallas_call(
        functools.partial(_conv_out_kernel, alpha=alpha),
        grid=(t // tm,),
        in_specs=[row, row, vec, vec, _full((d, d)), vec, vec, vec],
        out_specs=row,
        out_shape=jax.ShapeDtypeStruct((t, d), F32),
        compiler_params=_cparams(("arbitrary",)),
        name="conv_out_ln",
    )(c, xt, cg, cb, w_out, b_out, g1, b1)


def _rope_kernel(pos_ref, invf_ref, c_ref, s_ref):
    ang = pos_ref[...].astype(F32) * invf_ref[...]
    lane = lax.broadcasted_iota(I32, ang.shape, 1)
    half = ROT_DIM // 2
    c_ref[...] = jnp.where(lane < ROT_DIM, jnp.cos(ang), 1.0)
    sn = jnp.sin(ang)
    s_ref[...] = jnp.where(lane < half, -sn, jnp.where(lane < ROT_DIM, sn, 0.0))


def _rope_tables(positions, tm):
    t = positions.size
    half = ROT_DIM // 2
    inv_freq = ROPE_THETA ** (-jnp.arange(half, dtype=F32) * 2.0 / ROT_DIM)
    invf = jnp.zeros((1, LANES), F32).at[0, :half].set(inv_freq).at[0, half:ROT_DIM].set(inv_freq)
    out = jax.ShapeDtypeStruct((t, LANES), F32)
    return pl.pallas_call(
        _rope_kernel,
        grid=(t // tm,),
        in_specs=[pl.BlockSpec((tm, 1), lambda i: (i, 0)), _full((1, LANES))],
        out_specs=[pl.BlockSpec((tm, LANES), lambda i: (i, 0))] * 2,
        out_shape=[out, out],
        compiler_params=_cparams(("arbitrary",)),
        name="rope_tables",
    )(positions.reshape(t, 1), invf)


def _proj_kernel(x_ref, c_ref, s_ref, w_ref, *o_refs, rot, heads):
    y = jnp.dot(x_ref[...].astype(BF16), w_ref[...], preferred_element_type=F32)
    cos = c_ref[...]
    sin = s_ref[...]
    lane = lax.broadcasted_iota(I32, cos.shape, 1)
    first = lane < ROT_DIM // 2
    for j, o_ref in enumerate(o_refs):
        for h in range(heads):
            t = y[:, (j * heads + h) * HEAD_DIM:(j * heads + h + 1) * HEAD_DIM]
            if rot[j]:
                partner = jnp.where(first, pltpu.roll(t, HEAD_DIM - ROT_DIM // 2, 1), pltpu.roll(t, ROT_DIM // 2, 1))
                t = t * cos + partner * sin
            o_ref[:, h * HEAD_DIM:(h + 1) * HEAD_DIM] = t.astype(BF16)


def _project(x3, cos3, sin3, w, rot, dil, tm):
    b, s, d = x3.shape
    l = s // dil
    n_out = len(rot)
    gw = w.shape[1] // n_out
    xv = x3.reshape(b, l, dil * d)
    cv = cos3.reshape(b, l, dil * LANES)
    sv = sin3.reshape(b, l, dil * LANES)
    out = jax.ShapeDtypeStruct((b, dil, l, gw), BF16)
    return pl.pallas_call(
        functools.partial(_proj_kernel, rot=rot, heads=gw // HEAD_DIM),
        grid=(b, dil, l // tm),
        in_specs=[pl.BlockSpec((None, tm, d), lambda bi, r, i: (bi, i, r)),
                  pl.BlockSpec((None, tm, LANES), lambda bi, r, i: (bi, i, r)),
                  pl.BlockSpec((None, tm, LANES), lambda bi, r, i: (bi, i, r)),
                  _full((d, n_out * gw))],
        out_specs=[pl.BlockSpec((None, None, tm, gw), lambda bi, r, i: (bi, r, i, 0))] * n_out,
        out_shape=[out] * n_out,
        compiler_params=_cparams(("arbitrary", "arbitrary", "arbitrary")),
        name=f"proj_dil{dil}_n{n_out}",
    )(xv, cv, sv, w)


def _attn_kernel(q_ref, kc_ref, kh_ref, vc_ref, vh_ref, o_ref, lse_ref, *, heads):
    i = pl.program_id(2)
    tq = q_ref.shape[0]
    blk = ATTN_BLOCK
    scale = 1.0 / math.sqrt(HEAD_DIM)
    row = lax.broadcasted_iota(I32, (blk, blk), 0)
    col = lax.broadcasted_iota(I32, (blk, blk), 1)
    cur_mask = col <= row
    prev_band = col >= row
    first_mask = jnp.logical_and(prev_band, i > 0)
    dims = (((1,), (1,)), ((), ()))
    for n in range(tq // blk):
        rs = slice(n * blk, (n + 1) * blk)
        lse_tile = jnp.zeros((blk, LANES), F32)
        for h in range(heads):
            cs = slice(h * HEAD_DIM, (h + 1) * HEAD_DIM)
            q = q_ref[rs, cs]
            kc = kc_ref[rs, cs]
            vc = vc_ref[rs, cs]
            if n == 0:
                kp, vp, pmask = kh_ref[:, cs], vh_ref[:, cs], first_mask
            else:
                ps = slice((n - 1) * blk, n * blk)
                kp, vp, pmask = kc_ref[ps, cs], vc_ref[ps, cs], prev_band
            sc = lax.dot_general(q, kc, dims, preferred_element_type=F32) * scale
            sp = lax.dot_general(q, kp, dims, preferred_element_type=F32) * scale
            sc = jnp.where(cur_mask, sc, NEG_INF)
            sp = jnp.where(pmask, sp, NEG_INF)
            m = jnp.maximum(jnp.max(sc, axis=1, keepdims=True), jnp.max(sp, axis=1, keepdims=True))
            pc = jnp.exp(sc - m)
            pp = jnp.exp(sp - m)
            den = jnp.sum(pc, axis=1, keepdims=True) + jnp.sum(pp, axis=1, keepdims=True)
            o = (jnp.dot(pc.astype(BF16), vc, preferred_element_type=F32)
                 + jnp.dot(pp.astype(BF16), vp, preferred_element_type=F32))
            o_ref[rs, cs] = o / den
            lse_tile = jnp.where(col == h, m + jnp.log(den), lse_tile)
        lse_ref[rs, :] = lse_tile


def _attention(q, k, v, dil, tq):
    b, _, l, gw = q.shape
    nb = tq // ATTN_BLOCK
    cur = pl.BlockSpec((None, None, tq, gw), lambda bi, r, i: (bi, r, i, 0))
    halo = pl.BlockSpec((None, None, ATTN_BLOCK, gw), lambda bi, r, i: (bi, r, jnp.maximum(i * nb - 1, 0), 0))
    o, lse = pl.pallas_call(
        functools.partial(_attn_kernel, heads=gw // HEAD_DIM),
        grid=(b, dil, l // tq),
        in_specs=[cur, cur, halo, cur, halo],
        out_specs=[pl.BlockSpec((None, tq, gw), lambda bi, r, i: (bi, i, r)),
                   pl.BlockSpec((None, tq, LANES), lambda bi, r, i: (bi, i, r))],
        out_shape=[jax.ShapeDtypeStruct((b, l, dil * gw), F32),
                   jax.ShapeDtypeStruct((b, l, dil * LANES), F32)],
        compiler_params=_cparams(("arbitrary", "arbitrary", "arbitrary")),
        name=f"attn_dil{dil}",
    )(q, k, k, v, v)
    return o.reshape(b * l * dil, gw), lse.reshape(b * l * dil, LANES)


def _attn_out_kernel(o0_ref, o1_ref, o2_ref, l0_ref, l1_ref, l2_ref, x_ref, w_ref, g_ref, b_ref,
                     out_ref, mixed, *, alpha, heads):
    l0, l1, l2 = l0_ref[...], l1_ref[...], l2_ref[...]
    m = jnp.maximum(jnp.maximum(l0, l1), l2)
    e0, e1, e2 = jnp.exp(l0 - m), jnp.exp(l1 - m), jnp.exp(l2 - m)
    den = e0 + e1 + e2
    w0, w1, w2 = e0 / den, e1 / den, e2 / den
    for h in range(heads):
        cs = slice(h * HEAD_DIM, (h + 1) * HEAD_DIM)
        o = (w0[:, h:h + 1] * o0_ref[:, cs] + w1[:, h:h + 1] * o1_ref[:, cs] + w2[:, h:h + 1] * o2_ref[:, cs])
        mixed[:, cs] = o.astype(BF16)
    mix = jnp.dot(mixed[...], w_ref[...], preferred_element_type=F32)
    out_ref[...] = _ln(alpha * x_ref[...] + mix, g_ref[...], b_ref[...])


def _attn_out(outs, lses, xt, w_o, g1, b1, alpha, tm):
    t, d = xt.shape
    gw = w_o.shape[0]
    orow = pl.BlockSpec((tm, gw), lambda i: (i, 0))
    lrow = pl.BlockSpec((tm, LANES), lambda i: (i, 0))
    xrow = pl.BlockSpec((tm, d), lambda i: (i, 0))
    return pl.pallas_call(
        functools.partial(_attn_out_kernel, alpha=alpha, heads=gw // HEAD_DIM),
        grid=(t // tm,),
        in_specs=[orow] * 3 + [lrow] * 3 + [xrow, _full((gw, d)), _full((1, d)), _full((1, d))],
        out_specs=xrow,
        out_shape=jax.ShapeDtypeStruct((t, d), F32),
        scratch_shapes=[pltpu.VMEM((tm, gw), BF16)],
        compiler_params=_cparams(("arbitrary",)),
        name="attn_out_ln",
    )(*outs, *lses, xt, w_o, g1, b1)


def _router_kernel(x_ref, rw_ref, rb_ref, idx_ref, gate_ref, rank_ref, cnt_ref, carry):
    @pl.when(pl.program_id(0) == 0)
    def _():
        carry[...] = jnp.zeros(carry.shape, F32)

    tm = x_ref.shape[0]
    ne = rw_ref.shape[1]
    logits = lax.dot_general(x_ref[...], rw_ref[...], (((1,), (0,)), ((), ())),
                             precision=lax.Precision.HIGHEST, preferred_element_type=F32)
    scores = jax.nn.sigmoid(logits)
    sel = scores + rb_ref[...]
    lane = lax.broadcasted_iota(I32, (tm, ne), 1)
    chosen = jnp.zeros((tm, ne), F32)
    gsum = jnp.zeros((tm, 1), F32)
    picks = []
    for _ in range(TOP_K):
        best = jnp.max(sel, axis=1, keepdims=True)
        j = jnp.min(jnp.where(sel == best, lane, ne), axis=1, keepdims=True)
        onehot = lane == j
        sc = jnp.sum(jnp.where(onehot, scores, 0.0), axis=1, keepdims=True)
        sel = jnp.where(onehot, -jnp.inf, sel)
        chosen = chosen + onehot.astype(F32)
        gsum = gsum + sc
        picks.append((j, onehot, sc))
    r = lax.broadcasted_iota(I32, (tm, tm), 0)
    c = lax.broadcasted_iota(I32, (tm, tm), 1)
    earlier = (c < r).astype(BF16)
    rank_all = jnp.dot(earlier, chosen.astype(BF16), preferred_element_type=F32) + carry[...]
    out_lane = lax.broadcasted_iota(I32, (tm, LANES), 1)
    idx_t = jnp.zeros((tm, LANES), I32)
    gate_t = jnp.zeros((tm, LANES), F32)
    rank_t = jnp.zeros((tm, LANES), I32)
    for k, (j, onehot, sc) in enumerate(picks):
        rk = jnp.sum(jnp.where(onehot, rank_all, 0.0), axis=1, keepdims=True)
        idx_t = jnp.where(out_lane == k, j, idx_t)
        gate_t = jnp.where(out_lane == k, sc / gsum * ROUTED_SCALE, gate_t)
        rank_t = jnp.where(out_lane == k, rk.astype(I32), rank_t)
    idx_ref[...] = idx_t
    gate_ref[...] = gate_t
    rank_ref[...] = rank_t
    carry[...] = carry[...] + jnp.sum(chosen, axis=0, keepdims=True)
    cnt_ref[...] = carry[...]


def _router(xt, router_w, router_bias, tm):
    t, d = xt.shape
    ne = router_w.shape[1]
    wide = pl.BlockSpec((tm, LANES), lambda i: (i, 0))
    return pl.pallas_call(
        _router_kernel,
        grid=(t // tm,),
        in_specs=[pl.BlockSpec((tm, d), lambda i: (i, 0)), _full((d, ne)), _full((1, ne))],
        out_specs=[wide, wide, wide, _full((1, ne))],
        out_shape=[jax.ShapeDtypeStruct((t, LANES), I32), jax.ShapeDtypeStruct((t, LANES), F32),
                   jax.ShapeDtypeStruct((t, LANES), I32), jax.ShapeDtypeStruct((1, ne), F32)],
        scratch_shapes=[pltpu.VMEM((1, ne), F32)],
        compiler_params=_cparams(("arbitrary",)),
        name="moe_router",
    )(xt, router_w, router_bias.reshape(1, ne))


def _row_copy(src, src_row, dst, dst_row, sem):
    return pltpu.make_async_copy(src.at[pl.ds(src_row, 1), :], dst.at[pl.ds(dst_row, 1), :], sem)


def _dispatch_kernel(cnt_ref, pstart_ref, dest_hbm, x_hbm, xs_hbm, dest_smem, zrow, sem_idx, sem, *, tm, br):
    i = pl.program_id(0)
    n = tm * TOP_K
    fetch = pltpu.make_async_copy(dest_hbm.at[pl.ds(i * n, n)], dest_smem, sem_idx)
    fetch.start()

    @pl.when(i == 0)
    def _():
        zrow[...] = jnp.zeros(zrow.shape, zrow.dtype)
        for e in range(cnt_ref.shape[0]):
            cnt = cnt_ref[e]
            pad_end = (cnt + br - 1) // br * br
            base = pstart_ref[e]

            def fill(r, carry):
                _row_copy(zrow, 0, xs_hbm, base + r, sem).start()
                return carry

            def drain(r, carry):
                _row_copy(zrow, 0, xs_hbm, base + r, sem).wait()
                return carry

            lax.fori_loop(cnt, pad_end, fill, 0)
            lax.fori_loop(cnt, pad_end, drain, 0)

    fetch.wait()

    def issue(r, carry):
        for k in range(TOP_K):
            _row_copy(x_hbm, i * tm + r, xs_hbm, dest_smem[r * TOP_K + k], sem).start()
        return carry

    def drain_rows(r, carry):
        for k in range(TOP_K):
            _row_copy(x_hbm, i * tm + r, xs_hbm, dest_smem[r * TOP_K + k], sem).wait()
        return carry

    lax.fori_loop(0, tm, issue, 0)
    lax.fori_loop(0, tm, drain_rows, 0)


def _dispatch(xt, dest_flat, counts, pstarts, n_rows, tm, br):
    t, d = xt.shape
    grid_spec = pltpu.PrefetchScalarGridSpec(
        num_scalar_prefetch=2,
        grid=(t // tm,),
        in_specs=[pl.BlockSpec(memory_space=pl.ANY), pl.BlockSpec(memory_space=pl.ANY)],
        out_specs=pl.BlockSpec(memory_space=pl.ANY),
        scratch_shapes=[pltpu.SMEM((tm * TOP_K,), I32), pltpu.VMEM((8, d), xt.dtype),
                        pltpu.SemaphoreType.DMA, pltpu.SemaphoreType.DMA],
    )
    return pl.pallas_call(
        functools.partial(_dispatch_kernel, tm=tm, br=br),
        grid_spec=grid_spec,
        out_shape=jax.ShapeDtypeStruct((n_rows, d), xt.dtype),
        compiler_params=_cparams(("arbitrary",)),
        name="moe_dispatch",
    )(counts, pstarts, dest_flat, xt)


def _expert_kernel(be_ref, nu_ref, xs_ref, wg_ref, wu_ref, wd_ref, y_ref):
    @pl.when(pl.program_id(0) < nu_ref[0])
    def _():
        xb = xs_ref[...].astype(BF16)
        g = jnp.dot(xb, wg_ref[...].astype(BF16), preferred_element_type=F32)
        u = jnp.dot(xb, wu_ref[...].astype(BF16), preferred_element_type=F32)
        h = g * jax.nn.sigmoid(g) * u
        y_ref[...] = jnp.dot(h.astype(BF16), wd_ref[...].astype(BF16), preferred_element_type=F32)


def _experts(xs, blk_exp, n_used, w_gate, w_up, w_down, br):
    p, d = xs.shape
    f = w_gate.shape[2]
    n_blocks = p // br

    def row_map(i, be, nu):
        return (jnp.minimum(i, nu[0] - 1), 0)

    def w_map(i, be, nu):
        return (be[jnp.minimum(i, nu[0] - 1)], 0, 0)

    grid_spec = pltpu.PrefetchScalarGridSpec(
        num_scalar_prefetch=2,
        grid=(n_blocks,),
        in_specs=[pl.BlockSpec((br, d), row_map),
                  pl.BlockSpec((None, d, f), w_map),
                  pl.BlockSpec((None, d, f), w_map),
                  pl.BlockSpec((None, f, d), w_map)],
        out_specs=pl.BlockSpec((br, d), row_map),
    )
    return pl.pallas_call(
        _expert_kernel,
        grid_spec=grid_spec,
        out_shape=jax.ShapeDtypeStruct((p, d), F32),
        compiler_params=_cparams(("arbitrary",)),
        name="moe_experts",
    )(blk_exp, n_used, xs, w_gate, w_up, w_down)


def _combine_kernel(dest_hbm, y_hbm, gate_ref, x_ref, sg_ref, su_ref, sd_ref, g_ref, b_ref, o_ref,
                    dest_smem, ybuf, sem_idx, sem, *, tm, alpha):
    i = pl.program_id(0)
    n = tm * TOP_K
    fetch = pltpu.make_async_copy(dest_hbm.at[pl.ds(i * n, n)], dest_smem, sem_idx)
    fetch.start()
    fetch.wait()

    def issue(r, carry):
        for k in range(TOP_K):
            _row_copy(y_hbm, dest_smem[r * TOP_K + k], ybuf.at[k], r, sem).start()
        return carry

    def drain(r, carry):
        for k in range(TOP_K):
            _row_copy(y_hbm, dest_smem[r * TOP_K + k], ybuf.at[k], r, sem).wait()
        return carry

    lax.fori_loop(0, tm, issue, 0)
    x = x_ref[...]
    xb = x.astype(BF16)
    g = jnp.dot(xb, sg_ref[...], preferred_element_type=F32)
    u = jnp.dot(xb, su_ref[...], preferred_element_type=F32)
    h = g * jax.nn.sigmoid(g) * u
    acc = jnp.dot(h.astype(BF16), sd_ref[...], preferred_element_type=F32)
    lax.fori_loop(0, tm, drain, 0)
    gate = gate_ref[...]
    routed = gate[:, 0:1] * ybuf[0]
    for k in range(1, TOP_K):
        routed = routed + gate[:, k:k + 1] * ybuf[k]
    o_ref[...] = _ln(alpha * x + (routed + acc), g_ref[...], b_ref[...])


def _combine(dest_flat, y, gate, xt, sw_gate, sw_up, sw_down, g2, b2, alpha, tm):
    t, d = xt.shape
    f = sw_gate.shape[1]
    row = pl.BlockSpec((tm, d), lambda i: (i, 0))
    return pl.pallas_call(
        functools.partial(_combine_kernel, tm=tm, alpha=alpha),
        grid=(t // tm,),
        in_specs=[pl.BlockSpec(memory_space=pl.ANY), pl.BlockSpec(memory_space=pl.ANY),
                  pl.BlockSpec((tm, LANES), lambda i: (i, 0)), row,
                  _full((d, f)), _full((d, f)), _full((f, d)), _full((1, d)), _full((1, d))],
        out_specs=row,
        out_shape=jax.ShapeDtypeStruct((t, d), F32),
        scratch_shapes=[pltpu.SMEM((tm * TOP_K,), I32), pltpu.VMEM((TOP_K, tm, d), F32),
                        pltpu.SemaphoreType.DMA, pltpu.SemaphoreType.DMA],
        compiler_params=_cparams(("arbitrary",)),
        name="moe_combine_ln",
    )(dest_flat, y, gate, xt, sw_gate, sw_up, sw_down, g2, b2)


def _moe_layer(xt, router_w, router_bias, w_gate, w_up, w_down, sw_gate, sw_up, sw_down, g2, b2, alpha, cfg):
    t, d = xt.shape
    ne = router_w.shape[1]
    br = cfg["moe_rows"]
    idx, gate, rank, counts = _router(xt, router_w, router_bias, cfg["router_tm"])
    counts = counts.reshape(ne).astype(I32)
    padded = (counts + br - 1) // br * br
    pends = jnp.cumsum(padded)
    pstarts = pends - padded
    dest = (pstarts[idx[:, :TOP_K]] + rank[:, :TOP_K]).reshape(t * TOP_K)
    n_blocks = t * TOP_K // br + ne
    blk_start = jnp.arange(n_blocks, dtype=I32) * br
    blk_exp = jnp.minimum(jnp.sum(pends[None, :] <= blk_start[:, None], axis=1), ne - 1).astype(I32)
    n_used = (pends[-1:] // br).astype(I32)
    xs = _dispatch(xt, dest, counts, pstarts, n_blocks * br, cfg["dispatch_tm"], br)
    y = _experts(xs, blk_exp, n_used, w_gate, w_up, w_down, br)
    return _combine(dest, y, gate, xt, sw_gate.astype(BF16), sw_up.astype(BF16), sw_down.astype(BF16),
                    g2, b2, alpha, cfg["combine_tm"])


def _config(b, s, d):
    return dict(row_tm=min(512, s), conv_tm=min(512, s), conv_cw=min(256, d), proj_tm=256,
                router_tm=min(512, b * s), moe_rows=256, dispatch_tm=min(256, b * s), combine_tm=128)


def kernel(x, positions, ln1_g, ln1_b, ln2_g, ln2_b, conv_w_in, conv_b_in, conv_w_dw, conv_b_dw, conv_ln_g, conv_ln_b, conv_w_out, conv_b_out, w_kv, attn_w_q, attn_w_o, router_w, router_bias, exp_w_gate, exp_w_up, exp_w_down, sh_w_gate, sh_w_up, sh_w_down):
    b, s, d = x.shape
    t = b * s
    depth = ln1_g.shape[0]
    n_conv = conv_w_in.shape[0]
    alpha = (2.0 * depth) ** 0.25
    cfg = _config(b, s, d)
    n_groups = len(WINDOW_DILATIONS)
    gw = attn_w_o.shape[1]
    q_width = n_groups * gw
    vec = lambda a: a.reshape(1, -1)

    xt = x.reshape(t, d)
    kv = None
    tables = None
    for layer in range(depth):
        g1, b1 = vec(ln1_g[layer]), vec(ln1_b[layer])
        if layer < n_conv:
            h = _glu(xt, conv_w_in[layer].astype(BF16), vec(conv_b_in[layer]), cfg["row_tm"])
            c = _dwconv(h, conv_w_dw[layer], vec(conv_b_dw[layer]), b, s, cfg["conv_tm"], cfg["conv_cw"])
            xt = _conv_out(c, xt, vec(conv_ln_g[layer]), vec(conv_ln_b[layer]), conv_w_out[layer].astype(BF16),
                           vec(conv_b_out[layer]), g1, b1, alpha, cfg["row_tm"])
        else:
            j = layer - n_conv
            if tables is None:
                cos, sin = _rope_tables(positions, cfg["row_tm"])
                tables = (cos.reshape(b, s, LANES), sin.reshape(b, s, LANES))
            x3 = xt.reshape(b, s, d)
            wq = attn_w_q[j]
            outs, lses = [], []
            new_kv = []
            for g, (win, dil) in enumerate(WINDOW_DILATIONS):
                assert win // dil == ATTN_BLOCK
                cols = slice(g * gw, (g + 1) * gw)
                tm = min(cfg["proj_tm"], s // dil)
                if kv is None:
                    w = jnp.concatenate([wq[:, cols], w_kv[:, cols], w_kv[:, q_width:][:, cols]], axis=1).astype(BF16)
                    q, kg, vg = _project(x3, *tables, w, (True, True, False), dil, tm)
                    new_kv.append((kg, vg))
                else:
                    (q,) = _project(x3, *tables, wq[:, cols].astype(BF16), (True,), dil, tm)
                    kg, vg = kv[g]
                o, lse = _attention(q, kg, vg, dil, min(512, s // dil))
                outs.append(o)
                lses.append(lse)
            if kv is None:
                kv = new_kv
            xt = _attn_out(outs, lses, xt, attn_w_o[j].astype(BF16), g1, b1, alpha, cfg["proj_tm"])
        xt = _moe_layer(xt, router_w[layer], router_bias[layer], exp_w_gate[layer], exp_w_up[layer],
                        exp_w_down[layer], sh_w_gate[layer], sh_w_up[layer], sh_w_down[layer],
                        vec(ln2_g[layer]), vec(ln2_b[layer]), alpha, cfg)
    return xt.reshape(b, s, d)
```

```python
import functools
import math

import jax
import jax.numpy as jnp
from jax import lax
from jax.experimental import pallas as pl
from jax.experimental.pallas import tpu as pltpu

F32 = jnp.float32
BF16 = jnp.bfloat16
I32 = jnp.int32
U32 = jnp.uint32

LANES = 128
HEAD_DIM = 128
ROT_DIM = HEAD_DIM // 4
ROPE_THETA = 500000.0
ATTN_BLOCK = 128
WINDOW_DILATIONS = ((128, 1), (512, 4), (2048, 16))
NEG_INF = -1e30
TOP_K = 8
ROUTED_SCALE = 2.5
LN_EPS = 1e-5
CONV_HALO = 32
VMEM_LIMIT = 56 * 1024 * 1024


def _cparams(sem):
    return pltpu.CompilerParams(dimension_semantics=sem, vmem_limit_bytes=VMEM_LIMIT)


def _ln(y, g, b):
    mu = jnp.mean(y, axis=-1, keepdims=True)
    d = y - mu
    var = jnp.mean(d * d, axis=-1, keepdims=True)
    return d * lax.rsqrt(var + LN_EPS) * g + b


def _full(shape):
    n = len(shape)
    return pl.BlockSpec(shape, lambda *_: (0,) * n)


def _glu_kernel(x_ref, w_ref, b_ref, o_ref):
    d = o_ref.shape[-1]
    h = jnp.dot(x_ref[...].astype(BF16), w_ref[...], preferred_element_type=F32) + b_ref[...]
    o_ref[...] = h[:, :d] * jax.nn.sigmoid(h[:, d:])


def _glu(xt, w_in, b_in, tm):
    t, d = xt.shape
    return pl.pallas_call(
        _glu_kernel,
        grid=(t // tm,),
        in_specs=[pl.BlockSpec((tm, d), lambda i: (i, 0)), _full((d, 2 * d)), _full((1, 2 * d))],
        out_specs=pl.BlockSpec((tm, d), lambda i: (i, 0)),
        out_shape=jax.ShapeDtypeStruct((t, d), F32),
        compiler_params=_cparams(("arbitrary",)),
        name="conv_glu",
    )(xt, w_in, b_in)


def _dwconv_kernel(h_ref, w_ref, b_ref, o_ref, buf, *, rows):
    s = pl.program_id(2)
    tm = h_ref.shape[0]
    width = w_ref.shape[0]
    off = CONV_HALO - (width - 1)

    @pl.when(s == 0)
    def _():
        buf[0:CONV_HALO, :] = jnp.zeros((CONV_HALO, buf.shape[1]), F32)

    @pl.when(s > 0)
    def _():
        buf[0:CONV_HALO, :] = buf[tm:tm + CONV_HALO, :]

    buf[CONV_HALO:CONV_HALO + tm, :] = h_ref[...]
    for r0 in range(0, tm, rows):
        acc = None
        for k in range(width):
            term = buf[r0 + off + k:r0 + off + k + rows, :] * w_ref[k:k + 1, :]
            acc = term if acc is None else acc + term
        o_ref[r0:r0 + rows, :] = acc + b_ref[...]


def _dwconv(h, w_dw, b_dw, b, s, tm, cw):
    t, d = h.shape
    h3 = h.reshape(b, s, d)
    width = w_dw.shape[0]
    out = pl.pallas_call(
        functools.partial(_dwconv_kernel, rows=64),
        grid=(b, d // cw, s // tm),
        in_specs=[pl.BlockSpec((None, tm, cw), lambda bi, c, si: (bi, si, c)),
                  pl.BlockSpec((width, cw), lambda bi, c, si: (0, c)),
                  pl.BlockSpec((1, cw), lambda bi, c, si: (0, c))],
        out_specs=pl.BlockSpec((None, tm, cw), lambda bi, c, si: (bi, si, c)),
        out_shape=jax.ShapeDtypeStruct((b, s, d), F32),
        scratch_shapes=[pltpu.VMEM((CONV_HALO + tm, cw), F32)],
        compiler_params=_cparams(("arbitrary", "arbitrary", "arbitrary")),
        name="conv_depthwise",
    )(h3, w_dw, b_dw)
    return out.reshape(t, d)


def _conv_out_kernel(c_ref, x_ref, cg_ref, cb_ref, w_ref, bo_ref, g_ref, b_ref, o_ref, *, alpha):
    u = _ln(c_ref[...], cg_ref[...], cb_ref[...])
    u = u * jax.nn.sigmoid(u)
    mix = jnp.dot(u.astype(BF16), w_ref[...], preferred_element_type=F32) + bo_ref[...]
    o_ref[...] = _ln(alpha * x_ref[...] + mix, g_ref[...], b_ref[...])


def _conv_out(c, xt, cg, cb, w_out, b_out, g1, b1, alpha, tm):
    t, d = xt.shape
    row = pl.BlockSpec((tm, d), lambda i: (i, 0))
    vec = _full((1, d))
    return pl.pallas_call(
        functools.partial(_conv_out_kernel, alpha=alpha),
        grid=(t // tm,),
        in_specs=[row, row, vec, vec, _full((d, d)), vec, vec, vec],
        out_specs=row,
        out_shape=jax.ShapeDtypeStruct((t, d), F32),
        compiler_params=_cparams(("arbitrary",)),
        name="conv_out_ln",
    )(c, xt, cg, cb, w_out, b_out, g1, b1)


def _rope_kernel(pos_ref, invf_ref, c_ref, s_ref):
    ang = pos_ref[...].astype(F32) * invf_ref[...]
    lane = lax.broadcasted_iota(I32, ang.shape, 1)
    half = ROT_DIM // 2
    c_ref[...] = jnp.where(lane < ROT_DIM, jnp.cos(ang), 1.0)
    sn = jnp.sin(ang)
    s_ref[...] = jnp.where(lane < half, -sn, jnp.where(lane < ROT_DIM, sn, 0.0))


def _rope_tables(positions, tm):
    t = positions.size
    half = ROT_DIM // 2
    inv_freq = ROPE_THETA ** (-jnp.arange(half, dtype=F32) * 2.0 / ROT_DIM)
    invf = jnp.zeros((1, LANES), F32).at[0, :half].set(inv_freq).at[0, half:ROT_DIM].set(inv_freq)
    out = jax.ShapeDtypeStruct((t, LANES), F32)
    return pl.pallas_call(
        _rope_kernel,
        grid=(t // tm,),
        in_specs=[pl.BlockSpec((tm, 1), lambda i: (i, 0)), _full((1, LANES))],
        out_specs=[pl.BlockSpec((tm, LANES), lambda i: (i, 0))] * 2,
        out_shape=[out, out],
        compiler_params=_cparams(("arbitrary",)),
        name="rope_tables",
    )(positions.reshape(t, 1), invf)


def _proj_kernel(x_ref, c_ref, s_ref, w_ref, *o_refs, rot, heads):
    y = jnp.dot(x_ref[...].astype(BF16), w_ref[...], preferred_element_type=F32)
    cos = c_ref[...]
    sin = s_ref[...]
    lane = lax.broadcasted_iota(I32, cos.shape, 1)
    first = lane < ROT_DIM // 2
    for j, o_ref in enumerate(o_refs):
        for h in range(heads):
            t = y[:, (j * heads + h) * HEAD_DIM:(j * heads + h + 1) * HEAD_DIM]
            if rot[j]:
                partner = jnp.where(first, pltpu.roll(t, HEAD_DIM - ROT_DIM // 2, 1), pltpu.roll(t, ROT_DIM // 2, 1))
                t = t * cos + partner * sin
            o_ref[:, h * HEAD_DIM:(h + 1) * HEAD_DIM] = t.astype(BF16)


def _project(x3, cos3, sin3, w, rot, dil, tm):
    b, s, d = x3.shape
    l = s // dil
    n_out = len(rot)
    gw = w.shape[1] // n_out
    xv = x3.reshape(b, l, dil * d)
    cv = cos3.reshape(b, l, dil * LANES)
    sv = sin3.reshape(b, l, dil * LANES)
    out = jax.ShapeDtypeStruct((b, dil, l, gw), BF16)
    return pl.pallas_call(
        functools.partial(_proj_kernel, rot=rot, heads=gw // HEAD_DIM),
        grid=(b, dil, l // tm),
        in_specs=[pl.BlockSpec((None, tm, d), lambda bi, r, i: (bi, i, r)),
                  pl.BlockSpec((None, tm, LANES), lambda bi, r, i: (bi, i, r)),
                  pl.BlockSpec((None, tm, LANES), lambda bi, r, i: (bi, i, r)),
                  _full((d, n_out * gw))],
        out_specs=[pl.BlockSpec((None, None, tm, gw), lambda bi, r, i: (bi, r, i, 0))] * n_out,
        out_shape=[out] * n_out,
        compiler_params=_cparams(("arbitrary", "arbitrary", "arbitrary")),
        name=f"proj_dil{dil}_n{n_out}",
    )(xv, cv, sv, w)


def _attn_kernel(q_ref, kc_ref, kh_ref, vc_ref, vh_ref, o_ref, lse_ref, *, heads):
    i = pl.program_id(2)
    tq = q_ref.shape[0]
    blk = ATTN_BLOCK
    scale = 1.0 / math.sqrt(HEAD_DIM)
    row = lax.broadcasted_iota(I32, (blk, blk), 0)
    col = lax.broadcasted_iota(I32, (blk, blk), 1)
    cur_mask = col <= row
    prev_band = col >= row
    first_mask = jnp.logical_and(prev_band, i > 0)
    dims = (((1,), (1,)), ((), ()))
    for n in range(tq // blk):
        rs = slice(n * blk, (n + 1) * blk)
        lse_tile = jnp.zeros((blk, LANES), F32)
        for h in range(heads):
            cs = slice(h * HEAD_DIM, (h + 1) * HEAD_DIM)
            q = q_ref[rs, cs]
            kc = kc_ref[rs, cs]
            vc = vc_ref[rs, cs]
            if n == 0:
                kp, vp, pmask = kh_ref[:, cs], vh_ref[:, cs], first_mask
            else:
                ps = slice((n - 1) * blk, n * blk)
                kp, vp, pmask = kc_ref[ps, cs], vc_ref[ps, cs], prev_band
            sc = lax.dot_general(q, kc, dims, preferred_element_type=F32) * scale
            sp = lax.dot_general(q, kp, dims, preferred_element_type=F32) * scale
            sc = jnp.where(cur_mask, sc, NEG_INF)
            sp = jnp.where(pmask, sp, NEG_INF)
            m = jnp.maximum(jnp.max(sc, axis=1, keepdims=True), jnp.max(sp, axis=1, keepdims=True))
            pc = jnp.exp(sc - m)
            pp = jnp.exp(sp - m)
            den = jnp.sum(pc, axis=1, keepdims=True) + jnp.sum(pp, axis=1, keepdims=True)
            o = (jnp.dot(pc.astype(BF16), vc, preferred_element_type=F32)
                 + jnp.dot(pp.astype(BF16), vp, preferred_element_type=F32))
            o_ref[rs, cs] = o / den
            lse_tile = jnp.where(col == h, m + jnp.log(den), lse_tile)
        lse_ref[rs, :] = lse_tile


def _attention(q, k, v, dil, tq):
    b, _, l, gw = q.shape
    nb = tq // ATTN_BLOCK
    cur = pl.BlockSpec((None, None, tq, gw), lambda bi, r, i: (bi, r, i, 0))
    halo = pl.BlockSpec((None, None, ATTN_BLOCK, gw), lambda bi, r, i: (bi, r, jnp.maximum(i * nb - 1, 0), 0))
    o, lse = pl.pallas_call(
        functools.partial(_attn_kernel, heads=gw // HEAD_DIM),
        grid=(b, dil, l // tq),
        in_specs=[cur, cur, halo, cur, halo],
        out_specs=[pl.BlockSpec((None, tq, gw), lambda bi, r, i: (bi, i, r)),
                   pl.BlockSpec((None, tq, LANES), lambda bi, r, i: (bi, i, r))],
        out_shape=[jax.ShapeDtypeStruct((b, l, dil * gw), F32),
                   jax.ShapeDtypeStruct((b, l, dil * LANES), F32)],
        compiler_params=_cparams(("arbitrary", "arbitrary", "arbitrary")),
        name=f"attn_dil{dil}",
    )(q, k, k, v, v)
    return o.reshape(b * l * dil, gw), lse.reshape(b * l * dil, LANES)


def _attn_out_kernel(o0_ref, o1_ref, o2_ref, l0_ref, l1_ref, l2_ref, x_ref, w_ref, g_ref, b_ref,
                     out_ref, mixed, *, alpha, heads):
    l0, l1, l2 = l0_ref[...], l1_ref[...], l2_ref[...]
    m = jnp.maximum(jnp.maximum(l0, l1), l2)
    e0, e1, e2 = jnp.exp(l0 - m), jnp.exp(l1 - m), jnp.exp(l2 - m)
    den = e0 + e1 + e2
    w0, w1, w2 = e0 / den, e1 / den, e2 / den
    for h in range(heads):
        cs = slice(h * HEAD_DIM, (h + 1) * HEAD_DIM)
        o = (w0[:, h:h + 1] * o0_ref[:, cs] + w1[:, h:h + 1] * o1_ref[:, cs] + w2[:, h:h + 1] * o2_ref[:, cs])
        mixed[:, cs] = o.astype(BF16)
    mix = jnp.dot(mixed[...], w_ref[...], preferred_element_type=F32)
    out_ref[...] = _ln(alpha * x_ref[...] + mix, g_ref[...], b_ref[...])


def _attn_out(outs, lses, xt, w_o, g1, b1, alpha, tm):
    t, d = xt.shape
    gw = w_o.shape[0]
    orow = pl.BlockSpec((tm, gw), lambda i: (i, 0))
    lrow = pl.BlockSpec((tm, LANES), lambda i: (i, 0))
    xrow = pl.BlockSpec((tm, d), lambda i: (i, 0))
    return pl.pallas_call(
        functools.partial(_attn_out_kernel, alpha=alpha, heads=gw // HEAD_DIM),
        grid=(t // tm,),
        in_specs=[orow] * 3 + [lrow] * 3 + [xrow, _full((gw, d)), _full((1, d)), _full((1, d))],
        out_specs=xrow,
        out_shape=jax.ShapeDtypeStruct((t, d), F32),
        scratch_shapes=[pltpu.VMEM((tm, gw), BF16)],
        compiler_params=_cparams(("arbitrary",)),
        name="attn_out_ln",
    )(*outs, *lses, xt, w_o, g1, b1)


def _router_kernel(x_ref, rw_ref, rb_ref, idx_ref, gate_ref, rank_ref, cnt_ref, carry):
    @pl.when(pl.program_id(0) == 0)
    def _():
        carry[...] = jnp.zeros(carry.shape, F32)

    tm = x_ref.shape[0]
    ne = rw_ref.shape[1]
    logits = lax.dot_general(x_ref[...], rw_ref[...], (((1,), (0,)), ((), ())),
                             precision=lax.Precision.HIGHEST, preferred_element_type=F32)
    scores = jax.nn.sigmoid(logits)
    sel = scores + rb_ref[...]
    lane = lax.broadcasted_iota(I32, (tm, ne), 1)
    chosen = jnp.zeros((tm, ne), F32)
    gsum = jnp.zeros((tm, 1), F32)
    picks = []
    for _ in range(TOP_K):
        best = jnp.max(sel, axis=1, keepdims=True)
        j = jnp.min(jnp.where(sel == best, lane, ne), axis=1, keepdims=True)
        onehot = lane == j
        sc = jnp.sum(jnp.where(onehot, scores, 0.0), axis=1, keepdims=True)
        sel = jnp.where(onehot, -jnp.inf, sel)
        chosen = chosen + onehot.astype(F32)
        gsum = gsum + sc
        picks.append((j, onehot, sc))
    r = lax.broadcasted_iota(I32, (tm, tm), 0)
    c = lax.broadcasted_iota(I32, (tm, tm), 1)
    earlier = (c < r).astype(BF16)
    rank_all = jnp.dot(earlier, chosen.astype(BF16), preferred_element_type=F32) + carry[...]
    out_lane = lax.broadcasted_iota(I32, (tm, LANES), 1)
    idx_t = jnp.zeros((tm, LANES), I32)
    gate_t = jnp.zeros((tm, LANES), F32)
    rank_t = jnp.zeros((tm, LANES), I32)
    for k, (j, onehot, sc) in enumerate(picks):
        rk = jnp.sum(jnp.where(onehot, rank_all, 0.0), axis=1, keepdims=True)
        idx_t = jnp.where(out_lane == k, j, idx_t)
        gate_t = jnp.where(out_lane == k, sc / gsum * ROUTED_SCALE, gate_t)
        rank_t = jnp.where(out_lane == k, rk.astype(I32), rank_t)
    idx_ref[...] = idx_t
    gate_ref[...] = gate_t
    rank_ref[...] = rank_t
    carry[...] = carry[...] + jnp.sum(chosen, axis=0, keepdims=True)
    cnt_ref[...] = carry[...]


def _router(xt, router_w, router_bias, tm):
    t, d = xt.shape
    ne = router_w.shape[1]
    wide = pl.BlockSpec((tm, LANES), lambda i: (i, 0))
    return pl.pallas_call(
        _router_kernel,
        grid=(t // tm,),
        in_specs=[pl.BlockSpec((tm, d), lambda i: (i, 0)), _full((d, ne)), _full((1, ne))],
        out_specs=[wide, wide, wide, _full((1, ne))],
        out_shape=[jax.ShapeDtypeStruct((t, LANES), I32), jax.ShapeDtypeStruct((t, LANES), F32),
                   jax.ShapeDtypeStruct((t, LANES), I32), jax.ShapeDtypeStruct((1, ne), F32)],
        scratch_shapes=[pltpu.VMEM((1, ne), F32)],
        compiler_params=_cparams(("arbitrary",)),
        name="moe_router",
    )(xt, router_w, router_bias.reshape(1, ne))


def _pack_bf16_pairs(v):
    half = v.shape[1] // 2
    hi = lax.bitcast_convert_type(v[:, :half].astype(BF16).astype(F32), U32)
    lo = lax.bitcast_convert_type(v[:, half:].astype(BF16).astype(F32), U32)
    return hi | (lo >> 16)


def _unpack_bf16_pairs(p):
    first = lax.bitcast_convert_type(p & jnp.uint32(0xFFFF0000), F32)
    second = lax.bitcast_convert_type(p << 16, F32)
    return first, second


def _row_copy(src, src_row, dst, dst_row, sem):
    return pltpu.make_async_copy(src.at[pl.ds(src_row, 1), :], dst.at[pl.ds(dst_row, 1), :], sem)


def _dispatch_kernel(cnt_ref, pstart_ref, dest_hbm, x_ref, xs_hbm, dest_smem, packed, zrow, sem_idx, sem, *, tm, br):
    i = pl.program_id(0)
    n = tm * TOP_K
    fetch = pltpu.make_async_copy(dest_hbm.at[pl.ds(i * n, n)], dest_smem, sem_idx)
    fetch.start()
    packed[...] = _pack_bf16_pairs(x_ref[...])

    @pl.when(i == 0)
    def _():
        zrow[...] = jnp.zeros(zrow.shape, zrow.dtype)
        for e in range(cnt_ref.shape[0]):
            cnt = cnt_ref[e]
            pad_end = (cnt + br - 1) // br * br
            base = pstart_ref[e]

            def fill(r, carry):
                _row_copy(zrow, 0, xs_hbm, base + r, sem).start()
                return carry

            def drain(r, carry):
                _row_copy(zrow, 0, xs_hbm, base + r, sem).wait()
                return carry

            lax.fori_loop(cnt, pad_end, fill, 0)
            lax.fori_loop(cnt, pad_end, drain, 0)

    fetch.wait()

    def issue(r, carry):
        for k in range(TOP_K):
            _row_copy(packed, r, xs_hbm, dest_smem[r * TOP_K + k], sem).start(priority=k % 2)
        return carry

    def drain_rows(r, carry):
        for k in range(TOP_K):
            _row_copy(packed, r, xs_hbm, dest_smem[r * TOP_K + k], sem).wait()
        return carry

    lax.fori_loop(0, tm, issue, 0)
    lax.fori_loop(0, tm, drain_rows, 0)


def _dispatch(xt, dest_flat, counts, pstarts, n_rows, tm, br):
    t, d = xt.shape
    grid_spec = pltpu.PrefetchScalarGridSpec(
        num_scalar_prefetch=2,
        grid=(t // tm,),
        in_specs=[pl.BlockSpec(memory_space=pl.ANY), pl.BlockSpec((tm, d), lambda i, *_: (i, 0))],
        out_specs=pl.BlockSpec(memory_space=pl.ANY),
        scratch_shapes=[pltpu.SMEM((tm * TOP_K,), I32), pltpu.VMEM((tm, d // 2), U32), pltpu.VMEM((8, d // 2), U32),
                        pltpu.SemaphoreType.DMA, pltpu.SemaphoreType.DMA],
    )
    return pl.pallas_call(
        functools.partial(_dispatch_kernel, tm=tm, br=br),
        grid_spec=grid_spec,
        out_shape=jax.ShapeDtypeStruct((n_rows, d // 2), U32),
        compiler_params=_cparams(("arbitrary",)),
        name="moe_dispatch",
    )(counts, pstarts, dest_flat, xt)


def _expert_kernel(be_ref, nu_ref, xs_ref, wg_ref, wu_ref, wd_ref, y_ref, wg_b, wu_b, wd_b):
    i = pl.program_id(0)
    live = i < nu_ref[0]

    @pl.when(jnp.logical_and(live, jnp.logical_or(i == 0, be_ref[i] != be_ref[jnp.maximum(i - 1, 0)])))
    def _():
        wg_b[...] = wg_ref[...].astype(BF16)
        wu_b[...] = wu_ref[...].astype(BF16)
        wd_b[...] = wd_ref[...].astype(BF16)

    @pl.when(live)
    def _():
        first, second = _unpack_bf16_pairs(xs_ref[...])
        xb = jnp.concatenate([first.astype(BF16), second.astype(BF16)], axis=1)
        g = jnp.dot(xb, wg_b[...], preferred_element_type=F32)
        u = jnp.dot(xb, wu_b[...], preferred_element_type=F32)
        h = g * jax.nn.sigmoid(g) * u
        y = jnp.dot(h.astype(BF16), wd_b[...], preferred_element_type=F32)
        y_ref[...] = _pack_bf16_pairs(y)


def _experts(xs, blk_exp, n_used, w_gate, w_up, w_down, br):
    p, dh = xs.shape
    d, f = w_gate.shape[1:]
    n_blocks = p // br

    def row_map(i, be, nu):
        return (jnp.minimum(i, nu[0] - 1), 0)

    def w_map(i, be, nu):
        return (be[jnp.minimum(i, nu[0] - 1)], 0, 0)

    grid_spec = pltpu.PrefetchScalarGridSpec(
        num_scalar_prefetch=2,
        grid=(n_blocks,),
        in_specs=[pl.BlockSpec((br, dh), row_map),
                  pl.BlockSpec((None, d, f), w_map),
                  pl.BlockSpec((None, d, f), w_map),
                  pl.BlockSpec((None, f, d), w_map)],
        out_specs=pl.BlockSpec((br, dh), row_map),
        scratch_shapes=[pltpu.VMEM((d, f), BF16), pltpu.VMEM((d, f), BF16), pltpu.VMEM((f, d), BF16)],
    )
    return pl.pallas_call(
        _expert_kernel,
        grid_spec=grid_spec,
        out_shape=jax.ShapeDtypeStruct((p, dh), U32),
        compiler_params=_cparams(("arbitrary",)),
        name="moe_experts",
    )(blk_exp, n_used, xs, w_gate, w_up, w_down)


def _combine_kernel(dest_hbm, y_hbm, gate_ref, x_ref, sg_ref, su_ref, sd_ref, g_ref, b_ref, o_ref,
                    dest_smem, ybuf, sem_idx, sem, *, tm, alpha):
    i = pl.program_id(0)
    n = tm * TOP_K
    fetch = pltpu.make_async_copy(dest_hbm.at[pl.ds(i * n, n)], dest_smem, sem_idx)
    fetch.start()
    fetch.wait()

    def issue(r, carry):
        for k in range(TOP_K):
            _row_copy(y_hbm, dest_smem[r * TOP_K + k], ybuf.at[k], r, sem).start(priority=k % 2)
        return carry

    def drain(r, carry):
        for k in range(TOP_K):
            _row_copy(y_hbm, dest_smem[r * TOP_K + k], ybuf.at[k], r, sem).wait()
        return carry

    lax.fori_loop(0, tm, issue, 0)
    x = x_ref[...]
    xb = x.astype(BF16)
    g = jnp.dot(xb, sg_ref[...], preferred_element_type=F32)
    u = jnp.dot(xb, su_ref[...], preferred_element_type=F32)
    h = g * jax.nn.sigmoid(g) * u
    acc = jnp.dot(h.astype(BF16), sd_ref[...], preferred_element_type=F32)
    lax.fori_loop(0, tm, drain, 0)
    gate = gate_ref[...]
    first = second = None
    for k in range(TOP_K):
        a, b = _unpack_bf16_pairs(ybuf[k])
        w = gate[:, k:k + 1]
        first = w * a if first is None else first + w * a
        second = w * b if second is None else second + w * b
    routed = jnp.concatenate([first, second], axis=1)
    o_ref[...] = _ln(alpha * x + (routed + acc), g_ref[...], b_ref[...])


def _combine(dest_flat, y, gate, xt, sw_gate, sw_up, sw_down, g2, b2, alpha, tm):
    t, d = xt.shape
    f = sw_gate.shape[1]
    row = pl.BlockSpec((tm, d), lambda i: (i, 0))
    return pl.pallas_call(
        functools.partial(_combine_kernel, tm=tm, alpha=alpha),
        grid=(t // tm,),
        in_specs=[pl.BlockSpec(memory_space=pl.ANY), pl.BlockSpec(memory_space=pl.ANY),
                  pl.BlockSpec((tm, LANES), lambda i: (i, 0)), row,
                  _full((d, f)), _full((d, f)), _full((f, d)), _full((1, d)), _full((1, d))],
        out_specs=row,
        out_shape=jax.ShapeDtypeStruct((t, d), F32),
        scratch_shapes=[pltpu.SMEM((tm * TOP_K,), I32), pltpu.VMEM((TOP_K, tm, d // 2), U32),
                        pltpu.SemaphoreType.DMA, pltpu.SemaphoreType.DMA],
        compiler_params=_cparams(("arbitrary",)),
        name="moe_combine_ln",
    )(dest_flat, y, gate, xt, sw_gate, sw_up, sw_down, g2, b2)


def _moe_layer(xt, router_w, router_bias, w_gate, w_up, w_down, sw_gate, sw_up, sw_down, g2, b2, alpha, cfg):
    t, d = xt.shape
    ne = router_w.shape[1]
    br = cfg["moe_rows"]
    idx, gate, rank, counts = _router(xt, router_w, router_bias, cfg["router_tm"])
    counts = counts.reshape(ne).astype(I32)
    padded = (counts + br - 1) // br * br
    pends = jnp.cumsum(padded)
    pstarts = pends - padded
    dest = (pstarts[idx[:, :TOP_K]] + rank[:, :TOP_K]).reshape(t * TOP_K)
    n_blocks = t * TOP_K // br + ne
    blk_start = jnp.arange(n_blocks, dtype=I32) * br
    blk_exp = jnp.minimum(jnp.sum(pends[None, :] <= blk_start[:, None], axis=1), ne - 1).astype(I32)
    n_used = (pends[-1:] // br).astype(I32)
    xs = _dispatch(xt, dest, counts, pstarts, n_blocks * br, cfg["dispatch_tm"], br)
    y = _experts(xs, blk_exp, n_used, w_gate, w_up, w_down, br)
    return _combine(dest, y, gate, xt, sw_gate.astype(BF16), sw_up.astype(BF16), sw_down.astype(BF16),
                    g2, b2, alpha, cfg["combine_tm"])


def _config(b, s, d):
    return dict(row_tm=min(512, s), conv_tm=min(512, s), conv_cw=min(256, d), proj_tm=256,
                router_tm=min(512, b * s), moe_rows=256, dispatch_tm=min(256, b * s), combine_tm=128)


def kernel(x, positions, ln1_g, ln1_b, ln2_g, ln2_b, conv_w_in, conv_b_in, conv_w_dw, conv_b_dw, conv_ln_g, conv_ln_b, conv_w_out, conv_b_out, w_kv, attn_w_q, attn_w_o, router_w, router_bias, exp_w_gate, exp_w_up, exp_w_down, sh_w_gate, sh_w_up, sh_w_down):
    b, s, d = x.shape
    t = b * s
    depth = ln1_g.shape[0]
    n_conv = conv_w_in.shape[0]
    alpha = (2.0 * depth) ** 0.25
    cfg = _config(b, s, d)
    n_groups = len(WINDOW_DILATIONS)
    gw = attn_w_o.shape[1]
    q_width = n_groups * gw
    vec = lambda a: a.reshape(1, -1)

    xt = x.reshape(t, d)
    kv = None
    tables = None
    for layer in range(depth):
        g1, b1 = vec(ln1_g[layer]), vec(ln1_b[layer])
        if layer < n_conv:
            h = _glu(xt, conv_w_in[layer].astype(BF16), vec(conv_b_in[layer]), cfg["row_tm"])
            c = _dwconv(h, conv_w_dw[layer], vec(conv_b_dw[layer]), b, s, cfg["conv_tm"], cfg["conv_cw"])
            xt = _conv_out(c, xt, vec(conv_ln_g[layer]), vec(conv_ln_b[layer]), conv_w_out[layer].astype(BF16),
                           vec(conv_b_out[layer]), g1, b1, alpha, cfg["row_tm"])
        else:
            j = layer - n_conv
            if tables is None:
                cos, sin = _rope_tables(positions, cfg["row_tm"])
                tables = (cos.reshape(b, s, LANES), sin.reshape(b, s, LANES))
            x3 = xt.reshape(b, s, d)
            wq = attn_w_q[j]
            outs, lses = [], []
            new_kv = []
            for g, (win, dil) in enumerate(WINDOW_DILATIONS):
                assert win // dil == ATTN_BLOCK
                cols = slice(g * gw, (g + 1) * gw)
                tm = min(cfg["proj_tm"], s // dil)
                if kv is None:
                    w = jnp.concatenate([wq[:, cols], w_kv[:, cols], w_kv[:, q_width:][:, cols]], axis=1).astype(BF16)
                    q, kg, vg = _project(x3, *tables, w, (True, True, False), dil, tm)
                    new_kv.append((kg, vg))
                else:
                    (q,) = _project(x3, *tables, wq[:, cols].astype(BF16), (True,), dil, tm)
                    kg, vg = kv[g]
                o, lse = _attention(q, kg, vg, dil, min(512, s // dil))
                outs.append(o)
                lses.append(lse)
            if kv is None:
                kv = new_kv
            xt = _attn_out(outs, lses, xt, attn_w_o[j].astype(BF16), g1, b1, alpha, cfg["proj_tm"])
        xt = _moe_layer(xt, router_w[layer], router_bias[layer], exp_w_gate[layer], exp_w_up[layer],
                        exp_w_down[layer], sh_w_gate[layer], sh_w_up[layer], sh_w_down[layer],
                        vec(ln2_g[layer]), vec(ln2_b[layer]), alpha, cfg)
    return xt.reshape(b, s, d)
```

```python
import functools
import math

import jax
import jax.numpy as jnp
from jax import lax
from jax.experimental import pallas as pl
from jax.experimental.pallas import tpu as pltpu

F32 = jnp.float32
BF16 = jnp.bfloat16
I32 = jnp.int32
U32 = jnp.uint32

LANES = 128
HEAD_DIM = 128
ROT_DIM = HEAD_DIM // 4
ROPE_THETA = 500000.0
ATTN_BLOCK = 128
WINDOW_DILATIONS = ((128, 1), (512, 4), (2048, 16))
NEG_INF = -1e30
TOP_K = 8
ROUTED_SCALE = 2.5
LN_EPS = 1e-5
CONV_HALO = 32
VMEM_LIMIT = 56 * 1024 * 1024
SEG_ALIGN = 8
SORT_CHUNK = 256


def _cparams(sem):
    return pltpu.CompilerParams(dimension_semantics=sem, vmem_limit_bytes=VMEM_LIMIT)


def _ln(y, g, b):
    mu = jnp.mean(y, axis=-1, keepdims=True)
    d = y - mu
    var = jnp.mean(d * d, axis=-1, keepdims=True)
    return d * lax.rsqrt(var + LN_EPS) * g + b


def _full(shape):
    n = len(shape)
    return pl.BlockSpec(shape, lambda *_: (0,) * n)


def _glu_kernel(x_ref, w_ref, b_ref, o_ref):
    d = o_ref.shape[-1]
    h = jnp.dot(x_ref[...].astype(BF16), w_ref[...], preferred_element_type=F32) + b_ref[...]
    o_ref[...] = h[:, :d] * jax.nn.sigmoid(h[:, d:])


def _glu(xt, w_in, b_in, tm):
    t, d = xt.shape
    return pl.pallas_call(
        _glu_kernel,
        grid=(t // tm,),
        in_specs=[pl.BlockSpec((tm, d), lambda i: (i, 0)), _full((d, 2 * d)), _full((1, 2 * d))],
        out_specs=pl.BlockSpec((tm, d), lambda i: (i, 0)),
        out_shape=jax.ShapeDtypeStruct((t, d), F32),
        compiler_params=_cparams(("arbitrary",)),
        name="conv_glu",
    )(xt, w_in, b_in)


def _dwconv_kernel(h_ref, w_ref, b_ref, o_ref, buf, *, rows):
    s = pl.program_id(2)
    tm = h_ref.shape[0]
    width = w_ref.shape[0]
    off = CONV_HALO - (width - 1)

    @pl.when(s == 0)
    def _():
        buf[0:CONV_HALO, :] = jnp.zeros((CONV_HALO, buf.shape[1]), F32)

    @pl.when(s > 0)
    def _():
        buf[0:CONV_HALO, :] = buf[tm:tm + CONV_HALO, :]

    buf[CONV_HALO:CONV_HALO + tm, :] = h_ref[...]
    for r0 in range(0, tm, rows):
        acc = None
        for k in range(width):
            term = buf[r0 + off + k:r0 + off + k + rows, :] * w_ref[k:k + 1, :]
            acc = term if acc is None else acc + term
        o_ref[r0:r0 + rows, :] = acc + b_ref[...]


def _dwconv(h, w_dw, b_dw, b, s, tm, cw):
    t, d = h.shape
    h3 = h.reshape(b, s, d)
    width = w_dw.shape[0]
    out = pl.pallas_call(
        functools.partial(_dwconv_kernel, rows=64),
        grid=(b, d // cw, s // tm),
        in_specs=[pl.BlockSpec((None, tm, cw), lambda bi, c, si: (bi, si, c)),
                  pl.BlockSpec((width, cw), lambda bi, c, si: (0, c)),
                  pl.BlockSpec((1, cw), lambda bi, c, si: (0, c))],
        out_specs=pl.BlockSpec((None, tm, cw), lambda bi, c, si: (bi, si, c)),
        out_shape=jax.ShapeDtypeStruct((b, s, d), F32),
        scratch_shapes=[pltpu.VMEM((CONV_HALO + tm, cw), F32)],
        compiler_params=_cparams(("arbitrary", "arbitrary", "arbitrary")),
        name="conv_depthwise",
    )(h3, w_dw, b_dw)
    return out.reshape(t, d)


def _conv_out_kernel(c_ref, x_ref, cg_ref, cb_ref, w_ref, bo_ref, g_ref, b_ref, o_ref, *, alpha):
    u = _ln(c_ref[...], cg_ref[...], cb_ref[...])
    u = u * jax.nn.sigmoid(u)
    mix = jnp.dot(u.astype(BF16), w_ref[...], preferred_element_type=F32) + bo_ref[...]
    o_ref[...] = _ln(alpha * x_ref[...] + mix, g_ref[...], b_ref[...])


def _conv_out(c, xt, cg, cb, w_out, b_out, g1, b1, alpha, tm):
    t, d = xt.shape
    row = pl.BlockSpec((tm, d), lambda i: (i, 0))
    vec = _full((1, d))
    return pl.pallas_call(
        functools.partial(_conv_out_kernel, alpha=alpha),
        grid=(t // tm,),
        in_specs=[row, row, vec, vec, _full((d, d)), vec, vec, vec],
        out_specs=row,
        out_shape=jax.ShapeDtypeStruct((t, d), F32),
        compiler_params=_cparams(("arbitrary",)),
        name="conv_out_ln",
    )(c, xt, cg, cb, w_out, b_out, g1, b1)


def _rope_kernel(pos_ref, invf_ref, c_ref, s_ref):
    ang = pos_ref[...].astype(F32) * invf_ref[...]
    lane = lax.broadcasted_iota(I32, ang.shape, 1)
    half = ROT_DIM // 2
    c_ref[...] = jnp.where(lane < ROT_DIM, jnp.cos(ang), 1.0)
    sn = jnp.sin(ang)
    s_ref[...] = jnp.where(lane < half, -sn, jnp.where(lane < ROT_DIM, sn, 0.0))


def _rope_tables(positions, tm):
    t = positions.size
    half = ROT_DIM // 2
    inv_freq = ROPE_THETA ** (-jnp.arange(half, dtype=F32) * 2.0 / ROT_DIM)
    invf = jnp.zeros((1, LANES), F32).at[0, :half].set(inv_freq).at[0, half:ROT_DIM].set(inv_freq)
    out = jax.ShapeDtypeStruct((t, LANES), F32)
    return pl.pallas_call(
        _rope_kernel,
        grid=(t // tm,),
        in_specs=[pl.BlockSpec((tm, 1), lambda i: (i, 0)), _full((1, LANES))],
        out_specs=[pl.BlockSpec((tm, LANES), lambda i: (i, 0))] * 2,
        out_shape=[out, out],
        compiler_params=_cparams(("arbitrary",)),
        name="rope_tables",
    )(positions.reshape(t, 1), invf)


def _proj_kernel(x_ref, c_ref, s_ref, w_ref, *rest, rot, heads, dil):
    o_refs, lhs, tabs, xc = rest[:-3], rest[-3], rest[-2], rest[-1]
    tm = o_refs[0].shape[1]
    n_lane_tiles = xc.shape[0]
    if dil > 1:
        for c in range(n_lane_tiles):
            xc[c] = x_ref[:, c * LANES:(c + 1) * LANES]
    for r in range(dil):
        rows = slice(r * tm, (r + 1) * tm)
        if dil > 1:
            src = pl.ds(r, tm, stride=dil)
            for c in range(n_lane_tiles):
                lhs[rows, c * LANES:(c + 1) * LANES] = xc[c, src, :].astype(BF16)
        else:
            src = slice(None)
            lhs[rows, :] = x_ref[...].astype(BF16)
        tabs[0, rows, :] = c_ref[src, :]
        tabs[1, rows, :] = s_ref[src, :]
    y = jnp.dot(lhs[...], w_ref[...], preferred_element_type=F32)
    cos = tabs[0]
    sin = tabs[1]
    lane = lax.broadcasted_iota(I32, cos.shape, 1)
    first = lane < ROT_DIM // 2
    for j, o_ref in enumerate(o_refs):
        for h in range(heads):
            t = y[:, (j * heads + h) * HEAD_DIM:(j * heads + h + 1) * HEAD_DIM]
            if rot[j]:
                partner = jnp.where(first, pltpu.roll(t, HEAD_DIM - ROT_DIM // 2, 1), pltpu.roll(t, ROT_DIM // 2, 1))
                t = t * cos + partner * sin
            t = t.astype(BF16)
            for r in range(dil):
                o_ref[r, :, h * HEAD_DIM:(h + 1) * HEAD_DIM] = t[r * tm:(r + 1) * tm]


def _project(x3, cos3, sin3, w, rot, dil, rows):
    b, s, d = x3.shape
    l = s // dil
    tm = rows // dil
    n_out = len(rot)
    gw = w.shape[1] // n_out
    out = jax.ShapeDtypeStruct((b, dil, l, gw), BF16)
    tab = pl.BlockSpec((None, rows, LANES), lambda bi, i: (bi, i, 0))
    return pl.pallas_call(
        functools.partial(_proj_kernel, rot=rot, heads=gw // HEAD_DIM, dil=dil),
        grid=(b, s // rows),
        in_specs=[pl.BlockSpec((None, rows, d), lambda bi, i: (bi, i, 0)), tab, tab, _full((d, n_out * gw))],
        out_specs=[pl.BlockSpec((None, dil, tm, gw), lambda bi, i: (bi, 0, i, 0))] * n_out,
        out_shape=[out] * n_out,
        scratch_shapes=[pltpu.VMEM((rows, d), BF16), pltpu.VMEM((2, rows, LANES), F32),
                        pltpu.VMEM((d // LANES, rows, LANES), F32)],
        compiler_params=_cparams(("arbitrary", "arbitrary")),
        name=f"proj_dil{dil}_n{n_out}",
    )(x3, cos3, sin3, w)


def _attn_kernel(q_ref, kc_ref, kh_ref, vc_ref, vh_ref, o_ref, lse_ref, *, heads):
    i = pl.program_id(2)
    tq = q_ref.shape[0]
    blk = ATTN_BLOCK
    scale = 1.0 / math.sqrt(HEAD_DIM)
    row = lax.broadcasted_iota(I32, (blk, blk), 0)
    col = lax.broadcasted_iota(I32, (blk, blk), 1)
    cur_mask = col <= row
    prev_band = col >= row
    first_mask = jnp.logical_and(prev_band, i > 0)
    dims = (((1,), (1,)), ((), ()))
    for n in range(tq // blk):
        rs = slice(n * blk, (n + 1) * blk)
        lse_tile = jnp.zeros((blk, LANES), F32)
        for h in range(heads):
            cs = slice(h * HEAD_DIM, (h + 1) * HEAD_DIM)
            q = q_ref[rs, cs]
            kc = kc_ref[rs, cs]
            vc = vc_ref[rs, cs]
            if n == 0:
                kp, vp, pmask = kh_ref[:, cs], vh_ref[:, cs], first_mask
            else:
                ps = slice((n - 1) * blk, n * blk)
                kp, vp, pmask = kc_ref[ps, cs], vc_ref[ps, cs], prev_band
            sc = lax.dot_general(q, kc, dims, preferred_element_type=F32) * scale
            sp = lax.dot_general(q, kp, dims, preferred_element_type=F32) * scale
            sc = jnp.where(cur_mask, sc, NEG_INF)
            sp = jnp.where(pmask, sp, NEG_INF)
            m = jnp.maximum(jnp.max(sc, axis=1, keepdims=True), jnp.max(sp, axis=1, keepdims=True))
            pc = jnp.exp(sc - m)
            pp = jnp.exp(sp - m)
            den = jnp.sum(pc, axis=1, keepdims=True) + jnp.sum(pp, axis=1, keepdims=True)
            o = (jnp.dot(pc.astype(BF16), vc, preferred_element_type=F32)
                 + jnp.dot(pp.astype(BF16), vp, preferred_element_type=F32))
            o_ref[rs, cs] = o / den
            lse_tile = jnp.where(col == h, m + jnp.log(den), lse_tile)
        lse_ref[rs, :] = lse_tile


def _attention(q, k, v, dil, tq):
    b, _, l, gw = q.shape
    nb = tq // ATTN_BLOCK
    cur = pl.BlockSpec((None, None, tq, gw), lambda bi, r, i: (bi, r, i, 0))
    halo = pl.BlockSpec((None, None, ATTN_BLOCK, gw), lambda bi, r, i: (bi, r, jnp.maximum(i * nb - 1, 0), 0))
    o, lse = pl.pallas_call(
        functools.partial(_attn_kernel, heads=gw // HEAD_DIM),
        grid=(b, dil, l // tq),
        in_specs=[cur, cur, halo, cur, halo],
        out_specs=[pl.BlockSpec((None, None, tq, gw), lambda bi, r, i: (bi, r, i, 0)),
                   pl.BlockSpec((None, None, tq, LANES), lambda bi, r, i: (bi, r, i, 0))],
        out_shape=[jax.ShapeDtypeStruct((b, dil, l, gw), F32),
                   jax.ShapeDtypeStruct((b, dil, l, LANES), F32)],
        compiler_params=_cparams(("arbitrary", "arbitrary", "arbitrary")),
        name=f"attn_dil{dil}",
    )(q, k, k, v, v)
    return o, lse


def _attn_out_kernel(o0_ref, o1_ref, o2_ref, l0_ref, l1_ref, l2_ref, x_ref, w_ref, g_ref, b_ref,
                     out_ref, onat, lnat, mixed, *, alpha, heads, dils):
    tm = x_ref.shape[0]
    for gi, (o_ref, l_ref) in enumerate(((o0_ref, l0_ref), (o1_ref, l1_ref), (o2_ref, l2_ref))):
        dil = dils[gi]
        for r in range(dil):
            dst = pl.ds(r, tm // dil, stride=dil) if dil > 1 else slice(None)
            for h in range(heads):
                onat[gi * heads + h, dst, :] = o_ref[r, :, h * HEAD_DIM:(h + 1) * HEAD_DIM]
            lnat[gi, dst, :] = l_ref[r]
    l0, l1, l2 = lnat[0], lnat[1], lnat[2]
    m = jnp.maximum(jnp.maximum(l0, l1), l2)
    e0, e1, e2 = jnp.exp(l0 - m), jnp.exp(l1 - m), jnp.exp(l2 - m)
    den = e0 + e1 + e2
    w0, w1, w2 = e0 / den, e1 / den, e2 / den
    for h in range(heads):
        cs = slice(h * HEAD_DIM, (h + 1) * HEAD_DIM)
        o = w0[:, h:h + 1] * onat[h] + w1[:, h:h + 1] * onat[heads + h] + w2[:, h:h + 1] * onat[2 * heads + h]
        mixed[:, cs] = o.astype(BF16)
    mix = jnp.dot(mixed[...], w_ref[...], preferred_element_type=F32)
    out_ref[...] = _ln(alpha * x_ref[...] + mix, g_ref[...], b_ref[...])


def _attn_out(outs, lses, x3, w_o, g1, b1, alpha, tm):
    b, s, d = x3.shape
    gw = w_o.shape[0]
    dils = tuple(o.shape[1] for o in outs)
    ospecs = [pl.BlockSpec((None, dil, tm // dil, gw), lambda bi, i: (bi, 0, i, 0)) for dil in dils]
    lspecs = [pl.BlockSpec((None, dil, tm // dil, LANES), lambda bi, i: (bi, 0, i, 0)) for dil in dils]
    xrow = pl.BlockSpec((None, tm, d), lambda bi, i: (bi, i, 0))
    n = len(dils)
    return pl.pallas_call(
        functools.partial(_attn_out_kernel, alpha=alpha, heads=gw // HEAD_DIM, dils=dils),
        grid=(b, s // tm),
        in_specs=ospecs + lspecs + [xrow, _full((gw, d)), _full((1, d)), _full((1, d))],
        out_specs=xrow,
        out_shape=jax.ShapeDtypeStruct((b, s, d), F32),
        scratch_shapes=[pltpu.VMEM((n * gw // HEAD_DIM, tm, HEAD_DIM), F32), pltpu.VMEM((n, tm, LANES), F32),
                        pltpu.VMEM((tm, gw), BF16)],
        compiler_params=_cparams(("arbitrary", "arbitrary")),
        name="attn_out_ln",
    )(*outs, *lses, x3, w_o, g1, b1)


def _pack_bf16_pairs(v):
    half = v.shape[1] // 2
    hi = lax.bitcast_convert_type(v[:, :half].astype(BF16).astype(F32), U32)
    lo = lax.bitcast_convert_type(v[:, half:].astype(BF16).astype(F32), U32)
    return hi | (lo >> 16)


def _unpack_bf16_pairs(p):
    first = lax.bitcast_convert_type(p & jnp.uint32(0xFFFF0000), F32)
    second = lax.bitcast_convert_type(p << 16, F32)
    return first, second


def _router_kernel(x_ref, rw_ref, rb_ref, lp_ref, lpt_ref, gate_ref, cnt_ref, base_ref, tot_ref, carry):
    @pl.when(pl.program_id(0) == 0)
    def _():
        carry[...] = jnp.zeros(carry.shape, F32)

    tm = x_ref.shape[0]
    ne = rw_ref.shape[1]
    logits = lax.dot_general(x_ref[...], rw_ref[...], (((1,), (0,)), ((), ())),
                             precision=lax.Precision.HIGHEST, preferred_element_type=F32)
    scores = jax.nn.sigmoid(logits)
    sel = scores + rb_ref[...]
    lane = lax.broadcasted_iota(I32, (tm, ne), 1)
    chosen = jnp.zeros((tm, ne), F32)
    gsum = jnp.zeros((tm, 1), F32)
    picks = []
    for _ in range(TOP_K):
        best = jnp.max(sel, axis=1, keepdims=True)
        j = jnp.min(jnp.where(sel == best, lane, ne), axis=1, keepdims=True)
        onehot = lane == j
        sc = jnp.sum(jnp.where(onehot, scores, 0.0), axis=1, keepdims=True)
        sel = jnp.where(onehot, -jnp.inf, sel)
        chosen = chosen + onehot.astype(F32)
        gsum = gsum + sc
        picks.append((onehot, sc))
    r = lax.broadcasted_iota(I32, (tm, tm), 0)
    c = lax.broadcasted_iota(I32, (tm, tm), 1)
    rank = jnp.dot((c < r).astype(BF16), chosen.astype(BF16), preferred_element_type=F32)
    cnt = jnp.sum(chosen, axis=0, keepdims=True)
    cnt_al = jnp.floor((cnt + (SEG_ALIGN - 1)) / SEG_ALIGN) * SEG_ALIGN
    er = lax.broadcasted_iota(I32, (ne, ne), 0)
    ec = lax.broadcasted_iota(I32, (ne, ne), 1)
    seg_off = jnp.dot(jnp.broadcast_to(cnt_al, (8, ne)).astype(BF16), (er < ec).astype(BF16),
                      preferred_element_type=F32)[0:1]
    slot = rank + seg_off
    out_lane = lax.broadcasted_iota(I32, (tm, LANES), 1)
    lp_t = jnp.zeros((tm, LANES), I32)
    gate_t = jnp.zeros((tm, LANES), F32)
    for k, (onehot, sc) in enumerate(picks):
        row = jnp.sum(jnp.where(onehot, slot, 0.0), axis=1, keepdims=True)
        lp_t = jnp.where(out_lane == k, row.astype(I32), lp_t)
        gate_t = jnp.where(out_lane == k, sc / gsum * ROUTED_SCALE, gate_t)
    lp_ref[...] = lp_t
    lpt_ref[...] = lp_t.T[0:TOP_K]
    gate_ref[...] = gate_t
    cnt_ref[...] = cnt_al
    base_ref[...] = carry[...]
    carry[...] = carry[...] + cnt_al
    tot_ref[...] = carry[...]


def _router(xt, router_w, router_bias, tm):
    t, d = xt.shape
    ne = router_w.shape[1]
    n_tiles = t // tm
    wide = pl.BlockSpec((tm, LANES), lambda i: (i, 0))
    per_tile = pl.BlockSpec((None, 1, ne), lambda i: (i, 0, 0))
    return pl.pallas_call(
        _router_kernel,
        grid=(n_tiles,),
        in_specs=[pl.BlockSpec((tm, d), lambda i: (i, 0)), _full((d, ne)), _full((1, ne))],
        out_specs=[wide, pl.BlockSpec((None, TOP_K, tm), lambda i: (i, 0, 0)), wide, per_tile, per_tile, _full((1, ne))],
        out_shape=[jax.ShapeDtypeStruct((t, LANES), I32), jax.ShapeDtypeStruct((n_tiles, TOP_K, tm), I32),
                   jax.ShapeDtypeStruct((t, LANES), F32), jax.ShapeDtypeStruct((n_tiles, 1, ne), F32),
                   jax.ShapeDtypeStruct((n_tiles, 1, ne), F32), jax.ShapeDtypeStruct((1, ne), F32)],
        scratch_shapes=[pltpu.VMEM((1, ne), F32)],
        compiler_params=_cparams(("arbitrary",)),
        name="moe_router",
    )(xt, router_w, router_bias.reshape(1, ne))


def _segment_loops(tbl, ne, make_copy):
    def visit(action):
        def body(e, off):
            n = pl.multiple_of(tbl[e], SEG_ALIGN)
            g = pl.multiple_of(tbl[ne + e], SEG_ALIGN)
            off = pl.multiple_of(off, SEG_ALIGN)

            @pl.when(n > 0)
            def _():
                action(make_copy(off, g, n))

            return off + n
        return body

    start = lambda: lax.fori_loop(0, ne, visit(lambda cp: cp.start()), 0)
    wait = lambda: lax.fori_loop(0, ne, visit(lambda cp: cp.wait()), 0)
    return start, wait


def _dispatch_kernel(fs_ref, fn_ref, tbl_hbm, lpt_ref, x_ref, xs_hbm, tbl, sorted_buf, zbuf, sem_tbl, sem, *, ne):
    i = pl.program_id(0)
    tt = x_ref.shape[0]
    width = tbl.shape[0]
    fetch = pltpu.make_async_copy(tbl_hbm.at[pl.ds(i * width, width)], tbl, sem_tbl)
    fetch.start()

    @pl.when(i == 0)
    def _():
        zbuf[...] = jnp.zeros(zbuf.shape, zbuf.dtype)

        def visit(action):
            def body(e, carry):
                n = pl.multiple_of(fn_ref[e], SEG_ALIGN)
                s = pl.multiple_of(fs_ref[e], SEG_ALIGN)

                @pl.when(n > 0)
                def _():
                    action(pltpu.make_async_copy(zbuf.at[pl.ds(0, n), :], xs_hbm.at[pl.ds(s, n), :], sem))

                return carry
            return body

        lax.fori_loop(0, ne, visit(lambda cp: cp.start()), 0)
        lax.fori_loop(0, ne, visit(lambda cp: cp.wait()), 0)

    fetch.wait()
    xb = x_ref[...].astype(BF16)
    lpt = lpt_ref[...]
    n_chunks = lax.shift_right_logical(tbl[2 * ne] + (SORT_CHUNK - 1), SORT_CHUNK.bit_length() - 1)

    def chunk(ci, carry):
        j0 = pl.multiple_of(ci * SORT_CHUNK, SORT_CHUNK)
        rows = j0 + lax.broadcasted_iota(I32, (SORT_CHUNK, tt), 0)
        hit = lpt[0:1, :] == rows
        for k in range(1, TOP_K):
            hit = jnp.logical_or(hit, lpt[k:k + 1, :] == rows)
        srt = jnp.dot(jnp.where(hit, 1.0, 0.0).astype(BF16), xb, preferred_element_type=F32)
        sorted_buf[pl.ds(j0, SORT_CHUNK), :] = _pack_bf16_pairs(srt)
        return carry

    lax.fori_loop(0, n_chunks, chunk, 0)
    start, wait = _segment_loops(
        tbl, ne, lambda off, g, n: pltpu.make_async_copy(sorted_buf.at[pl.ds(off, n), :], xs_hbm.at[pl.ds(g, n), :], sem))
    start()
    wait()


def _sorted_rows(tt, ne):
    return -(-(tt * TOP_K + ne * SEG_ALIGN) // SORT_CHUNK) * SORT_CHUNK


def _dispatch(xt, lpt, table, fill_start, fill_n, n_rows, tt, br):
    t, d = xt.shape
    ne = fill_start.shape[0]
    grid_spec = pltpu.PrefetchScalarGridSpec(
        num_scalar_prefetch=2,
        grid=(t // tt,),
        in_specs=[pl.BlockSpec(memory_space=pl.ANY),
                  pl.BlockSpec((None, TOP_K, tt), lambda i, *_: (i, 0, 0)),
                  pl.BlockSpec((tt, d), lambda i, *_: (i, 0))],
        out_specs=pl.BlockSpec(memory_space=pl.ANY),
        scratch_shapes=[pltpu.SMEM((2 * ne + LANES,), I32), pltpu.VMEM((_sorted_rows(tt, ne), d // 2), U32),
                        pltpu.VMEM((br, d // 2), U32), pltpu.SemaphoreType.DMA, pltpu.SemaphoreType.DMA],
    )
    return pl.pallas_call(
        functools.partial(_dispatch_kernel, ne=ne),
        grid_spec=grid_spec,
        out_shape=jax.ShapeDtypeStruct((n_rows, d // 2), U32),
        compiler_params=_cparams(("arbitrary",)),
        name="moe_dispatch",
    )(fill_start, fill_n, table, lpt, xt)


def _expert_kernel(be_ref, nu_ref, xs_ref, wg_ref, wu_ref, wd_ref, y_ref, wg_b, wu_b, wd_b):
    i = pl.program_id(0)
    live = i < nu_ref[0]

    @pl.when(jnp.logical_and(live, jnp.logical_or(i == 0, be_ref[i] != be_ref[jnp.maximum(i - 1, 0)])))
    def _():
        wg_b[...] = wg_ref[...].astype(BF16)
        wu_b[...] = wu_ref[...].astype(BF16)
        wd_b[...] = wd_ref[...].astype(BF16)

    @pl.when(live)
    def _():
        first, second = _unpack_bf16_pairs(xs_ref[...])
        xb = jnp.concatenate([first.astype(BF16), second.astype(BF16)], axis=1)
        g = jnp.dot(xb, wg_b[...], preferred_element_type=F32)
        u = jnp.dot(xb, wu_b[...], preferred_element_type=F32)
        h = g * jax.nn.sigmoid(g) * u
        y = jnp.dot(h.astype(BF16), wd_b[...], preferred_element_type=F32)
        y_ref[...] = _pack_bf16_pairs(y)


def _experts(xs, blk_exp, n_used, w_gate, w_up, w_down, br):
    p, dh = xs.shape
    d, f = w_gate.shape[1:]
    n_blocks = p // br

    def row_map(i, be, nu):
        return (jnp.minimum(i, nu[0] - 1), 0)

    def w_map(i, be, nu):
        return (be[jnp.minimum(i, nu[0] - 1)], 0, 0)

    grid_spec = pltpu.PrefetchScalarGridSpec(
        num_scalar_prefetch=2,
        grid=(n_blocks,),
        in_specs=[pl.BlockSpec((br, dh), row_map),
                  pl.BlockSpec((None, d, f), w_map),
                  pl.BlockSpec((None, d, f), w_map),
                  pl.BlockSpec((None, f, d), w_map)],
        out_specs=pl.BlockSpec((br, dh), row_map),
        scratch_shapes=[pltpu.VMEM((d, f), BF16), pltpu.VMEM((d, f), BF16), pltpu.VMEM((f, d), BF16)],
    )
    return pl.pallas_call(
        _expert_kernel,
        grid_spec=grid_spec,
        out_shape=jax.ShapeDtypeStruct((p, dh), U32),
        compiler_params=_cparams(("arbitrary",)),
        name="moe_experts",
    )(blk_exp, n_used, xs, w_gate, w_up, w_down)


def _combine_kernel(tbl_hbm, y_hbm, lp_ref, gate_ref, x_ref, sg_ref, su_ref, sd_ref, g_ref, b_ref, o_ref,
                    tbl, ysorted, acc, sem_tbl, sem, *, ne, alpha):
    i = pl.program_id(0)
    tt = x_ref.shape[0]
    half = x_ref.shape[1] // 2
    width = tbl.shape[0]
    fetch = pltpu.make_async_copy(tbl_hbm.at[pl.ds(i * width, width)], tbl, sem_tbl)
    fetch.start()
    fetch.wait()
    start, wait = _segment_loops(
        tbl, ne, lambda off, g, n: pltpu.make_async_copy(y_hbm.at[pl.ds(g, n), :], ysorted.at[pl.ds(off, n), :], sem))
    start()
    x = x_ref[...]
    xb = x.astype(BF16)
    g = jnp.dot(xb, sg_ref[...], preferred_element_type=F32)
    u = jnp.dot(xb, su_ref[...], preferred_element_type=F32)
    h = g * jax.nn.sigmoid(g) * u
    acc[...] = jnp.dot(h.astype(BF16), sd_ref[...], preferred_element_type=F32)
    wait()
    lp = lp_ref[...]
    gate = gate_ref[...]
    total = tbl[2 * ne]
    n_chunks = lax.shift_right_logical(total + (SORT_CHUNK - 1), SORT_CHUNK.bit_length() - 1)

    def chunk(ci, carry):
        j0 = pl.multiple_of(ci * SORT_CHUNK, SORT_CHUNK)
        cols = j0 + lax.broadcasted_iota(I32, (tt, SORT_CHUNK), 1)
        gmat = jnp.zeros((tt, SORT_CHUNK), F32)
        for k in range(TOP_K):
            gmat = jnp.where(lp[:, k:k + 1] == cols, gate[:, k:k + 1], gmat)
        g_hi = gmat.astype(BF16)
        g_lo = (gmat - g_hi.astype(F32)).astype(BF16)
        rows = j0 + lax.broadcasted_iota(I32, (SORT_CHUNK, half), 0)
        ys = jnp.where(rows < total, ysorted[pl.ds(j0, SORT_CHUNK), :], jnp.uint32(0))
        first, second = _unpack_bf16_pairs(ys)
        first = first.astype(BF16)
        second = second.astype(BF16)
        acc[:, :half] += (jnp.dot(g_hi, first, preferred_element_type=F32)
                          + jnp.dot(g_lo, first, preferred_element_type=F32))
        acc[:, half:] += (jnp.dot(g_hi, second, preferred_element_type=F32)
                          + jnp.dot(g_lo, second, preferred_element_type=F32))
        return carry

    lax.fori_loop(0, n_chunks, chunk, 0)
    o_ref[...] = _ln(alpha * x + acc[...], g_ref[...], b_ref[...])


def _combine(table, y, lp, gate, xt, sw_gate, sw_up, sw_down, g2, b2, alpha, tt, ne):
    t, d = xt.shape
    f = sw_gate.shape[1]
    row = pl.BlockSpec((tt, d), lambda i: (i, 0))
    wide = pl.BlockSpec((tt, LANES), lambda i: (i, 0))
    return pl.pallas_call(
        functools.partial(_combine_kernel, ne=ne, alpha=alpha),
        grid=(t // tt,),
        in_specs=[pl.BlockSpec(memory_space=pl.ANY), pl.BlockSpec(memory_space=pl.ANY), wide, wide, row,
                  _full((d, f)), _full((d, f)), _full((f, d)), _full((1, d)), _full((1, d))],
        out_specs=row,
        out_shape=jax.ShapeDtypeStruct((t, d), F32),
        scratch_shapes=[pltpu.SMEM((2 * ne + LANES,), I32), pltpu.VMEM((_sorted_rows(tt, ne), d // 2), U32),
                        pltpu.VMEM((tt, d), F32), pltpu.SemaphoreType.DMA, pltpu.SemaphoreType.DMA],
        compiler_params=_cparams(("arbitrary",)),
        name="moe_combine_ln",
    )(table, y, lp, gate, xt, sw_gate, sw_up, sw_down, g2, b2)


def _moe_layer(xt, router_w, router_bias, w_gate, w_up, w_down, sw_gate, sw_up, sw_down, g2, b2, alpha, cfg):
    t, d = xt.shape
    ne = router_w.shape[1]
    br = cfg["moe_rows"]
    tt = cfg["moe_tile"]
    n_tiles = t // tt
    lp, lpt, gate, cnt, base, tot = _router(xt, router_w, router_bias, tt)
    cnt = cnt.reshape(n_tiles, ne).astype(I32)
    tot = tot.reshape(ne).astype(I32)
    padded = (tot + br - 1) // br * br
    pends = jnp.cumsum(padded)
    pstarts = pends - padded
    seg_start = pstarts[None, :] + base.reshape(n_tiles, ne).astype(I32)
    tile_rows = jnp.sum(cnt, axis=1, keepdims=True)
    table = jnp.concatenate([cnt, seg_start, tile_rows, jnp.zeros((n_tiles, LANES - 1), I32)], axis=1).reshape(-1)
    n_blocks = -(-(t * TOP_K + ne * n_tiles * SEG_ALIGN) // br) + ne
    blk_start = jnp.arange(n_blocks, dtype=I32) * br
    blk_exp = jnp.minimum(jnp.sum(pends[None, :] <= blk_start[:, None], axis=1), ne - 1).astype(I32)
    n_used = (pends[-1:] // br).astype(I32)
    xs = _dispatch(xt, lpt, table, pstarts + tot, padded - tot, n_blocks * br, tt, br)
    y = _experts(xs, blk_exp, n_used, w_gate, w_up, w_down, br)
    return _combine(table, y, lp, gate, xt, sw_gate.astype(BF16), sw_up.astype(BF16), sw_down.astype(BF16),
                    g2, b2, alpha, tt, ne)


def _config(b, s, d):
    return dict(row_tm=min(512, s), conv_tm=min(512, s), conv_cw=min(256, d), proj_rows=min(512, s), attn_out_tm=256,
                moe_rows=256, moe_tile=min(256, b * s))


def kernel(x, positions, ln1_g, ln1_b, ln2_g, ln2_b, conv_w_in, conv_b_in, conv_w_dw, conv_b_dw, conv_ln_g, conv_ln_b, conv_w_out, conv_b_out, w_kv, attn_w_q, attn_w_o, router_w, router_bias, exp_w_gate, exp_w_up, exp_w_down, sh_w_gate, sh_w_up, sh_w_down):
    b, s, d = x.shape
    t = b * s
    depth = ln1_g.shape[0]
    n_conv = conv_w_in.shape[0]
    alpha = (2.0 * depth) ** 0.25
    cfg = _config(b, s, d)
    n_groups = len(WINDOW_DILATIONS)
    gw = attn_w_o.shape[1]
    q_width = n_groups * gw
    vec = lambda a: a.reshape(1, -1)

    xt = x.reshape(t, d)
    kv = None
    tables = None
    for layer in range(depth):
        g1, b1 = vec(ln1_g[layer]), vec(ln1_b[layer])
        if layer < n_conv:
            h = _glu(xt, conv_w_in[layer].astype(BF16), vec(conv_b_in[layer]), cfg["row_tm"])
            c = _dwconv(h, conv_w_dw[layer], vec(conv_b_dw[layer]), b, s, cfg["conv_tm"], cfg["conv_cw"])
            xt = _conv_out(c, xt, vec(conv_ln_g[layer]), vec(conv_ln_b[layer]), conv_w_out[layer].astype(BF16),
                           vec(conv_b_out[layer]), g1, b1, alpha, cfg["row_tm"])
        else:
            j = layer - n_conv
            if tables is None:
                cos, sin = _rope_tables(positions, cfg["row_tm"])
                tables = (cos.reshape(b, s, LANES), sin.reshape(b, s, LANES))
            x3 = xt.reshape(b, s, d)
            wq = attn_w_q[j]
            outs, lses = [], []
            new_kv = []
            for g, (win, dil) in enumerate(WINDOW_DILATIONS):
                assert win // dil == ATTN_BLOCK
                cols = slice(g * gw, (g + 1) * gw)
                if kv is None:
                    w = jnp.concatenate([wq[:, cols], w_kv[:, cols], w_kv[:, q_width:][:, cols]], axis=1).astype(BF16)
                    q, kg, vg = _project(x3, *tables, w, (True, True, False), dil, cfg["proj_rows"])
                    new_kv.append((kg, vg))
                else:
                    (q,) = _project(x3, *tables, wq[:, cols].astype(BF16), (True,), dil, cfg["proj_rows"])
                    kg, vg = kv[g]
                o, lse = _attention(q, kg, vg, dil, min(512, s // dil))
                outs.append(o)
                lses.append(lse)
            if kv is None:
                kv = new_kv
            xt = _attn_out(outs, lses, x3, attn_w_o[j].astype(BF16), g1, b1, alpha, cfg["attn_out_tm"]).reshape(t, d)
        xt = _moe_layer(xt, router_w[layer], router_bias[layer], exp_w_gate[layer], exp_w_up[layer],
                        exp_w_down[layer], sh_w_gate[layer], sh_w_up[layer], sh_w_down[layer],
                        vec(ln2_g[layer]), vec(ln2_b[layer]), alpha, cfg)
    return xt.reshape(b, s, d)
```

```python
import functools
import math

import jax
import jax.numpy as jnp
from jax import lax
from jax.experimental import pallas as pl
from jax.experimental.pallas import tpu as pltpu

F32 = jnp.float32
BF16 = jnp.bfloat16
I32 = jnp.int32
U32 = jnp.uint32

LANES = 128
HEAD_DIM = 128
ROT_DIM = HEAD_DIM // 4
ROPE_THETA = 500000.0
ATTN_BLOCK = 128
WINDOW_DILATIONS = ((128, 1), (512, 4), (2048, 16))
NEG_INF = -1e30
TOP_K = 8
ROUTED_SCALE = 2.5
LN_EPS = 1e-5
CONV_HALO = 32
VMEM_LIMIT = 56 * 1024 * 1024
SEG_ALIGN = 8
SORT_CHUNK = 256


def _cparams(sem):
    return pltpu.CompilerParams(dimension_semantics=sem, vmem_limit_bytes=VMEM_LIMIT)


def _ln(y, g, b):
    mu = jnp.mean(y, axis=-1, keepdims=True)
    d = y - mu
    var = jnp.mean(d * d, axis=-1, keepdims=True)
    return d * lax.rsqrt(var + LN_EPS) * g + b


def _full(shape):
    n = len(shape)
    return pl.BlockSpec(shape, lambda *_: (0,) * n)


def _glu_kernel(x_ref, w_ref, b_ref, o_ref):
    d = o_ref.shape[-1]
    h = jnp.dot(x_ref[...].astype(BF16), w_ref[...], preferred_element_type=F32) + b_ref[...]
    o_ref[...] = h[:, :d] * jax.nn.sigmoid(h[:, d:])


def _glu(xt, w_in, b_in, tm):
    t, d = xt.shape
    return pl.pallas_call(
        _glu_kernel,
        grid=(t // tm,),
        in_specs=[pl.BlockSpec((tm, d), lambda i: (i, 0)), _full((d, 2 * d)), _full((1, 2 * d))],
        out_specs=pl.BlockSpec((tm, d), lambda i: (i, 0)),
        out_shape=jax.ShapeDtypeStruct((t, d), F32),
        compiler_params=_cparams(("arbitrary",)),
        name="conv_glu",
    )(xt, w_in, b_in)


def _dwconv_kernel(h_ref, w_ref, b_ref, o_ref, buf, *, rows):
    s = pl.program_id(2)
    tm = h_ref.shape[0]
    width = w_ref.shape[0]
    off = CONV_HALO - (width - 1)

    @pl.when(s == 0)
    def _():
        buf[0:CONV_HALO, :] = jnp.zeros((CONV_HALO, buf.shape[1]), F32)

    @pl.when(s > 0)
    def _():
        buf[0:CONV_HALO, :] = buf[tm:tm + CONV_HALO, :]

    buf[CONV_HALO:CONV_HALO + tm, :] = h_ref[...]
    sub = 8
    for c0 in range(0, buf.shape[1], LANES):
        cs = slice(c0, c0 + LANES)
        for r0 in range(0, tm, rows):
            acc = None
            for phase in range(min(sub, width)):
                n_taps = (width - 1 - phase) // sub + 1
                start = r0 + off + phase
                window = buf[start:start + rows + sub * (n_taps - 1), cs]
                for a in range(n_taps):
                    k = sub * a + phase
                    term = window[sub * a:sub * a + rows] * w_ref[k:k + 1, cs]
                    acc = term if acc is None else acc + term
            o_ref[r0:r0 + rows, cs] = acc + b_ref[:, cs]


def _dwconv(h, w_dw, b_dw, b, s, tm, cw):
    t, d = h.shape
    h3 = h.reshape(b, s, d)
    width = w_dw.shape[0]
    out = pl.pallas_call(
        functools.partial(_dwconv_kernel, rows=64),
        grid=(b, d // cw, s // tm),
        in_specs=[pl.BlockSpec((None, tm, cw), lambda bi, c, si: (bi, si, c)),
                  pl.BlockSpec((width, cw), lambda bi, c, si: (0, c)),
                  pl.BlockSpec((1, cw), lambda bi, c, si: (0, c))],
        out_specs=pl.BlockSpec((None, tm, cw), lambda bi, c, si: (bi, si, c)),
        out_shape=jax.ShapeDtypeStruct((b, s, d), F32),
        scratch_shapes=[pltpu.VMEM((CONV_HALO + tm, cw), F32)],
        compiler_params=_cparams(("arbitrary", "arbitrary", "arbitrary")),
        name="conv_depthwise",
    )(h3, w_dw, b_dw)
    return out.reshape(t, d)


def _conv_out_kernel(c_ref, x_ref, cg_ref, cb_ref, w_ref, bo_ref, g_ref, b_ref, o_ref, *, alpha):
    u = _ln(c_ref[...], cg_ref[...], cb_ref[...])
    u = u * jax.nn.sigmoid(u)
    mix = jnp.dot(u.astype(BF16), w_ref[...], preferred_element_type=F32) + bo_ref[...]
    o_ref[...] = _ln(alpha * x_ref[...] + mix, g_ref[...], b_ref[...])


def _conv_out(c, xt, cg, cb, w_out, b_out, g1, b1, alpha, tm):
    t, d = xt.shape
    row = pl.BlockSpec((tm, d), lambda i: (i, 0))
    vec = _full((1, d))
    return pl.pallas_call(
        functools.partial(_conv_out_kernel, alpha=alpha),
        grid=(t // tm,),
        in_specs=[row, row, vec, vec, _full((d, d)), vec, vec, vec],
        out_specs=row,
        out_shape=jax.ShapeDtypeStruct((t, d), F32),
        compiler_params=_cparams(("arbitrary",)),
        name="conv_out_ln",
    )(c, xt, cg, cb, w_out, b_out, g1, b1)


def _rope_kernel(pos_ref, invf_ref, c_ref, s_ref):
    ang = pos_ref[...].astype(F32) * invf_ref[...]
    lane = lax.broadcasted_iota(I32, ang.shape, 1)
    half = ROT_DIM // 2
    c_ref[...] = jnp.where(lane < ROT_DIM, jnp.cos(ang), 1.0)
    sn = jnp.sin(ang)
    s_ref[...] = jnp.where(lane < half, -sn, jnp.where(lane < ROT_DIM, sn, 0.0))


def _rope_tables(positions, tm):
    t = positions.size
    half = ROT_DIM // 2
    inv_freq = ROPE_THETA ** (-jnp.arange(half, dtype=F32) * 2.0 / ROT_DIM)
    invf = jnp.zeros((1, LANES), F32).at[0, :half].set(inv_freq).at[0, half:ROT_DIM].set(inv_freq)
    out = jax.ShapeDtypeStruct((t, LANES), F32)
    return pl.pallas_call(
        _rope_kernel,
        grid=(t // tm,),
        in_specs=[pl.BlockSpec((tm, 1), lambda i: (i, 0)), _full((1, LANES))],
        out_specs=[pl.BlockSpec((tm, LANES), lambda i: (i, 0))] * 2,
        out_shape=[out, out],
        compiler_params=_cparams(("arbitrary",)),
        name="rope_tables",
    )(positions.reshape(t, 1), invf)


def _proj_kernel(x_ref, c_ref, s_ref, w_ref, *rest, rot, heads, dil):
    o_refs, lhs, tabs, xc = rest[:-3], rest[-3], rest[-2], rest[-1]
    tm = o_refs[0].shape[1]
    n_lane_tiles = xc.shape[0]
    if dil > 1:
        for c in range(n_lane_tiles):
            xc[c] = x_ref[:, c * LANES:(c + 1) * LANES]
    for r in range(dil):
        rows = slice(r * tm, (r + 1) * tm)
        if dil > 1:
            src = pl.ds(r, tm, stride=dil)
            for c in range(n_lane_tiles):
                lhs[rows, c * LANES:(c + 1) * LANES] = xc[c, src, :].astype(BF16)
        else:
            src = slice(None)
            lhs[rows, :] = x_ref[...].astype(BF16)
        tabs[0, rows, :] = c_ref[src, :]
        tabs[1, rows, :] = s_ref[src, :]
    y = jnp.dot(lhs[...], w_ref[...], preferred_element_type=F32)
    cos = tabs[0]
    sin = tabs[1]
    lane = lax.broadcasted_iota(I32, cos.shape, 1)
    first = lane < ROT_DIM // 2
    for j, o_ref in enumerate(o_refs):
        for h in range(heads):
            t = y[:, (j * heads + h) * HEAD_DIM:(j * heads + h + 1) * HEAD_DIM]
            if rot[j]:
                partner = jnp.where(first, pltpu.roll(t, HEAD_DIM - ROT_DIM // 2, 1), pltpu.roll(t, ROT_DIM // 2, 1))
                t = t * cos + partner * sin
            t = t.astype(BF16)
            for r in range(dil):
                o_ref[r, :, h * HEAD_DIM:(h + 1) * HEAD_DIM] = t[r * tm:(r + 1) * tm]


def _project(x3, cos3, sin3, w, rot, dil, rows):
    b, s, d = x3.shape
    l = s // dil
    tm = rows // dil
    n_out = len(rot)
    gw = w.shape[1] // n_out
    out = jax.ShapeDtypeStruct((b, dil, l, gw), BF16)
    tab = pl.BlockSpec((None, rows, LANES), lambda bi, i: (bi, i, 0))
    return pl.pallas_call(
        functools.partial(_proj_kernel, rot=rot, heads=gw // HEAD_DIM, dil=dil),
        grid=(b, s // rows),
        in_specs=[pl.BlockSpec((None, rows, d), lambda bi, i: (bi, i, 0)), tab, tab, _full((d, n_out * gw))],
        out_specs=[pl.BlockSpec((None, dil, tm, gw), lambda bi, i: (bi, 0, i, 0))] * n_out,
        out_shape=[out] * n_out,
        scratch_shapes=[pltpu.VMEM((rows, d), BF16), pltpu.VMEM((2, rows, LANES), F32),
                        pltpu.VMEM((d // LANES, rows, LANES), F32)],
        compiler_params=_cparams(("arbitrary", "arbitrary")),
        name=f"proj_dil{dil}_n{n_out}",
    )(x3, cos3, sin3, w)


def _attn_kernel(q_ref, kc_ref, kh_ref, vc_ref, vh_ref, o_ref, lse_ref, kbuf, vbuf, *, heads):
    i = pl.program_id(2)
    tq = q_ref.shape[0]
    blk = ATTN_BLOCK
    kbuf[0:blk, :] = kh_ref[...]
    kbuf[blk:blk + tq, :] = kc_ref[...]
    vbuf[0:blk, :] = vh_ref[...]
    vbuf[blk:blk + tq, :] = vc_ref[...]
    scale = 1.0 / math.sqrt(HEAD_DIM)
    row = lax.broadcasted_iota(I32, (blk, 2 * blk), 0)
    col = lax.broadcasted_iota(I32, (blk, 2 * blk), 1)
    band = jnp.logical_and(col >= row, col <= row + blk)
    first_band = jnp.logical_and(band, jnp.logical_or(col >= blk, i > 0))
    lane = lax.broadcasted_iota(I32, (blk, LANES), 1)
    dims = (((1,), (1,)), ((), ()))
    for n in range(tq // blk):
        rs = slice(n * blk, (n + 1) * blk)
        ks = slice(n * blk, (n + 2) * blk)
        mask = first_band if n == 0 else band
        lse_tile = jnp.zeros((blk, LANES), F32)
        for h in range(heads):
            cs = slice(h * HEAD_DIM, (h + 1) * HEAD_DIM)
            s = lax.dot_general(q_ref[rs, cs], kbuf[ks, cs], dims, preferred_element_type=F32) * scale
            s = jnp.where(mask, s, NEG_INF)
            m = jnp.max(s, axis=1, keepdims=True)
            p = jnp.exp(s - m)
            den = jnp.sum(p, axis=1, keepdims=True)
            o = jnp.dot(p.astype(BF16), vbuf[ks, cs], preferred_element_type=F32)
            o_ref[rs, cs] = o / den
            lse_tile = jnp.where(lane == h, m + jnp.log(den), lse_tile)
        lse_ref[rs, :] = lse_tile


def _attention(q, k, v, dil, tq):
    b, _, l, gw = q.shape
    nb = tq // ATTN_BLOCK
    cur = pl.BlockSpec((None, None, tq, gw), lambda bi, r, i: (bi, r, i, 0))
    halo = pl.BlockSpec((None, None, ATTN_BLOCK, gw), lambda bi, r, i: (bi, r, jnp.maximum(i * nb - 1, 0), 0))
    o, lse = pl.pallas_call(
        functools.partial(_attn_kernel, heads=gw // HEAD_DIM),
        grid=(b, dil, l // tq),
        in_specs=[cur, cur, halo, cur, halo],
        out_specs=[pl.BlockSpec((None, None, tq, gw), lambda bi, r, i: (bi, r, i, 0)),
                   pl.BlockSpec((None, None, tq, LANES), lambda bi, r, i: (bi, r, i, 0))],
        out_shape=[jax.ShapeDtypeStruct((b, dil, l, gw), F32),
                   jax.ShapeDtypeStruct((b, dil, l, LANES), F32)],
        scratch_shapes=[pltpu.VMEM((ATTN_BLOCK + tq, gw), BF16), pltpu.VMEM((ATTN_BLOCK + tq, gw), BF16)],
        compiler_params=_cparams(("arbitrary", "arbitrary", "arbitrary")),
        name=f"attn_dil{dil}",
    )(q, k, k, v, v)
    return o, lse


def _attn_out_kernel(o0_ref, o1_ref, o2_ref, l0_ref, l1_ref, l2_ref, x_ref, w_ref, g_ref, b_ref,
                     out_ref, onat, lnat, mixed, *, alpha, heads, dils):
    tm = x_ref.shape[0]
    for gi, (o_ref, l_ref) in enumerate(((o0_ref, l0_ref), (o1_ref, l1_ref), (o2_ref, l2_ref))):
        dil = dils[gi]
        for r in range(dil):
            dst = pl.ds(r, tm // dil, stride=dil) if dil > 1 else slice(None)
            for h in range(heads):
                onat[gi * heads + h, dst, :] = o_ref[r, :, h * HEAD_DIM:(h + 1) * HEAD_DIM]
            lnat[gi, dst, :] = l_ref[r]
    l0, l1, l2 = lnat[0], lnat[1], lnat[2]
    m = jnp.maximum(jnp.maximum(l0, l1), l2)
    e0, e1, e2 = jnp.exp(l0 - m), jnp.exp(l1 - m), jnp.exp(l2 - m)
    den = e0 + e1 + e2
    w0, w1, w2 = e0 / den, e1 / den, e2 / den
    for h in range(heads):
        cs = slice(h * HEAD_DIM, (h + 1) * HEAD_DIM)
        o = w0[:, h:h + 1] * onat[h] + w1[:, h:h + 1] * onat[heads + h] + w2[:, h:h + 1] * onat[2 * heads + h]
        mixed[:, cs] = o.astype(BF16)
    mix = jnp.dot(mixed[...], w_ref[...], preferred_element_type=F32)
    out_ref[...] = _ln(alpha * x_ref[...] + mix, g_ref[...], b_ref[...])


def _attn_out(outs, lses, x3, w_o, g1, b1, alpha, tm):
    b, s, d = x3.shape
    gw = w_o.shape[0]
    dils = tuple(o.shape[1] for o in outs)
    ospecs = [pl.BlockSpec((None, dil, tm // dil, gw), lambda bi, i: (bi, 0, i, 0)) for dil in dils]
    lspecs = [pl.BlockSpec((None, dil, tm // dil, LANES), lambda bi, i: (bi, 0, i, 0)) for dil in dils]
    xrow = pl.BlockSpec((None, tm, d), lambda bi, i: (bi, i, 0))
    n = len(dils)
    return pl.pallas_call(
        functools.partial(_attn_out_kernel, alpha=alpha, heads=gw // HEAD_DIM, dils=dils),
        grid=(b, s // tm),
        in_specs=ospecs + lspecs + [xrow, _full((gw, d)), _full((1, d)), _full((1, d))],
        out_specs=xrow,
        out_shape=jax.ShapeDtypeStruct((b, s, d), F32),
        scratch_shapes=[pltpu.VMEM((n * gw // HEAD_DIM, tm, HEAD_DIM), F32), pltpu.VMEM((n, tm, LANES), F32),
                        pltpu.VMEM((tm, gw), BF16)],
        compiler_params=_cparams(("arbitrary", "arbitrary")),
        name="attn_out_ln",
    )(*outs, *lses, x3, w_o, g1, b1)


def _pack_bf16_pairs(v):
    half = v.shape[1] // 2
    hi = lax.bitcast_convert_type(v[:, :half].astype(BF16).astype(F32), U32)
    lo = lax.bitcast_convert_type(v[:, half:].astype(BF16).astype(F32), U32)
    return hi | (lo >> 16)


def _unpack_bf16_pairs(p):
    first = lax.bitcast_convert_type(p & jnp.uint32(0xFFFF0000), F32)
    second = lax.bitcast_convert_type(p << 16, F32)
    return first, second


def _router_kernel(x_ref, rw_ref, rb_ref, lp_ref, lpt_ref, gate_ref, cnt_ref, base_ref, tot_ref, carry):
    @pl.when(pl.program_id(0) == 0)
    def _():
        carry[...] = jnp.zeros(carry.shape, F32)

    tm = x_ref.shape[0]
    ne = rw_ref.shape[1]
    logits = lax.dot_general(x_ref[...], rw_ref[...], (((1,), (0,)), ((), ())),
                             precision=lax.Precision.HIGHEST, preferred_element_type=F32)
    scores = jax.nn.sigmoid(logits)
    sel = scores + rb_ref[...]
    lane = lax.broadcasted_iota(I32, (tm, ne), 1)
    chosen = jnp.zeros((tm, ne), F32)
    gsum = jnp.zeros((tm, 1), F32)
    picks = []
    for _ in range(TOP_K):
        best = jnp.max(sel, axis=1, keepdims=True)
        j = jnp.min(jnp.where(sel == best, lane, ne), axis=1, keepdims=True)
        onehot = lane == j
        sc = jnp.sum(jnp.where(onehot, scores, 0.0), axis=1, keepdims=True)
        sel = jnp.where(onehot, -jnp.inf, sel)
        chosen = chosen + onehot.astype(F32)
        gsum = gsum + sc
        picks.append((onehot, sc))
    r = lax.broadcasted_iota(I32, (tm, tm), 0)
    c = lax.broadcasted_iota(I32, (tm, tm), 1)
    rank = jnp.dot((c < r).astype(BF16), chosen.astype(BF16), preferred_element_type=F32)
    cnt = jnp.sum(chosen, axis=0, keepdims=True)
    cnt_al = jnp.floor((cnt + (SEG_ALIGN - 1)) / SEG_ALIGN) * SEG_ALIGN
    er = lax.broadcasted_iota(I32, (ne, ne), 0)
    ec = lax.broadcasted_iota(I32, (ne, ne), 1)
    seg_off = jnp.dot(jnp.broadcast_to(cnt_al, (8, ne)).astype(BF16), (er < ec).astype(BF16),
                      preferred_element_type=F32)[0:1]
    slot = rank + seg_off
    out_lane = lax.broadcasted_iota(I32, (tm, LANES), 1)
    lp_t = jnp.zeros((tm, LANES), I32)
    gate_t = jnp.zeros((tm, LANES), F32)
    for k, (onehot, sc) in enumerate(picks):
        row = jnp.sum(jnp.where(onehot, slot, 0.0), axis=1, keepdims=True)
        lp_t = jnp.where(out_lane == k, row.astype(I32), lp_t)
        gate_t = jnp.where(out_lane == k, sc / gsum * ROUTED_SCALE, gate_t)
    lp_ref[...] = lp_t
    lpt_ref[...] = lp_t.T[0:TOP_K]
    gate_ref[...] = gate_t
    cnt_ref[...] = cnt_al
    base_ref[...] = carry[...]
    carry[...] = carry[...] + cnt_al
    tot_ref[...] = carry[...]


def _router(xt, router_w, router_bias, tm):
    t, d = xt.shape
    ne = router_w.shape[1]
    n_tiles = t // tm
    wide = pl.BlockSpec((tm, LANES), lambda i: (i, 0))
    per_tile = pl.BlockSpec((None, 1, ne), lambda i: (i, 0, 0))
    return pl.pallas_call(
        _router_kernel,
        grid=(n_tiles,),
        in_specs=[pl.BlockSpec((tm, d), lambda i: (i, 0)), _full((d, ne)), _full((1, ne))],
        out_specs=[wide, pl.BlockSpec((None, TOP_K, tm), lambda i: (i, 0, 0)), wide, per_tile, per_tile, _full((1, ne))],
        out_shape=[jax.ShapeDtypeStruct((t, LANES), I32), jax.ShapeDtypeStruct((n_tiles, TOP_K, tm), I32),
                   jax.ShapeDtypeStruct((t, LANES), F32), jax.ShapeDtypeStruct((n_tiles, 1, ne), F32),
                   jax.ShapeDtypeStruct((n_tiles, 1, ne), F32), jax.ShapeDtypeStruct((1, ne), F32)],
        scratch_shapes=[pltpu.VMEM((1, ne), F32)],
        compiler_params=_cparams(("arbitrary",)),
        name="moe_router",
    )(xt, router_w, router_bias.reshape(1, ne))


def _segment_loop(tbl, base, ne, make_copy, action):
    def body(e, off):
        n = pl.multiple_of(tbl[base + e], SEG_ALIGN)
        g = pl.multiple_of(tbl[base + ne + e], SEG_ALIGN)
        off = pl.multiple_of(off, SEG_ALIGN)

        @pl.when(n > 0)
        def _():
            action(make_copy(off, g, n))

        return off + n

    lax.fori_loop(0, ne, body, 0)


def _start(cp):
    cp.start()


def _wait(cp):
    cp.wait()


def _dispatch_kernel(fs_ref, fn_ref, tbl_hbm, lpt_ref, x_ref, xs_hbm, tbl, sorted_buf, zbuf, sem_tbl, sem, *, ne):
    i = pl.program_id(0)
    last = pl.num_programs(0) - 1
    slot = lax.rem(i, 2)
    tt = x_ref.shape[0]
    width = tbl.shape[0] // 2
    fetch = pltpu.make_async_copy(tbl_hbm.at[pl.ds(i * width, width)], tbl.at[pl.ds(slot * width, width)], sem_tbl)
    fetch.start()

    @pl.when(i == 0)
    def _():
        zbuf[...] = jnp.zeros(zbuf.shape, zbuf.dtype)

        def visit(action):
            def body(e, carry):
                n = pl.multiple_of(fn_ref[e], SEG_ALIGN)
                s = pl.multiple_of(fs_ref[e], SEG_ALIGN)

                @pl.when(n > 0)
                def _():
                    action(pltpu.make_async_copy(zbuf.at[pl.ds(0, n), :], xs_hbm.at[pl.ds(s, n), :], sem.at[0]))

                return carry
            return body

        lax.fori_loop(0, ne, visit(_start), 0)
        lax.fori_loop(0, ne, visit(_wait), 0)

    fetch.wait()
    xb = x_ref[...].astype(BF16)
    lpt = lpt_ref[...]
    n_chunks = lax.shift_right_logical(tbl[slot * width + 2 * ne] + (SORT_CHUNK - 1), SORT_CHUNK.bit_length() - 1)

    def chunk(ci, carry):
        j0 = pl.multiple_of(ci * SORT_CHUNK, SORT_CHUNK)
        rows = j0 + lax.broadcasted_iota(I32, (SORT_CHUNK, tt), 0)
        hit = lpt[0:1, :] == rows
        for k in range(1, TOP_K):
            hit = jnp.logical_or(hit, lpt[k:k + 1, :] == rows)
        srt = jnp.dot(jnp.where(hit, 1.0, 0.0).astype(BF16), xb, preferred_element_type=F32)
        sorted_buf[slot, pl.ds(j0, SORT_CHUNK), :] = _pack_bf16_pairs(srt)
        return carry

    lax.fori_loop(0, n_chunks, chunk, 0)

    def writes(sl):
        return lambda off, g, n: pltpu.make_async_copy(
            sorted_buf.at[sl, pl.ds(off, n), :], xs_hbm.at[pl.ds(g, n), :], sem.at[sl])

    _segment_loop(tbl, slot * width, ne, writes(slot), _start)

    @pl.when(i > 0)
    def _():
        _segment_loop(tbl, (1 - slot) * width, ne, writes(1 - slot), _wait)

    @pl.when(i == last)
    def _():
        _segment_loop(tbl, slot * width, ne, writes(slot), _wait)


def _sorted_rows(tt, ne):
    return -(-(tt * TOP_K + ne * SEG_ALIGN) // SORT_CHUNK) * SORT_CHUNK


def _dispatch(xt, lpt, table, fill_start, fill_n, n_rows, tt, br):
    t, d = xt.shape
    ne = fill_start.shape[0]
    grid_spec = pltpu.PrefetchScalarGridSpec(
        num_scalar_prefetch=2,
        grid=(t // tt,),
        in_specs=[pl.BlockSpec(memory_space=pl.ANY),
                  pl.BlockSpec((None, TOP_K, tt), lambda i, *_: (i, 0, 0)),
                  pl.BlockSpec((tt, d), lambda i, *_: (i, 0))],
        out_specs=pl.BlockSpec(memory_space=pl.ANY),
        scratch_shapes=[pltpu.SMEM((2 * (2 * ne + LANES),), I32), pltpu.VMEM((2, _sorted_rows(tt, ne), d // 2), U32),
                        pltpu.VMEM((br, d // 2), U32), pltpu.SemaphoreType.DMA, pltpu.SemaphoreType.DMA((2,))],
    )
    return pl.pallas_call(
        functools.partial(_dispatch_kernel, ne=ne),
        grid_spec=grid_spec,
        out_shape=jax.ShapeDtypeStruct((n_rows, d // 2), U32),
        compiler_params=_cparams(("arbitrary",)),
        name="moe_dispatch",
    )(fill_start, fill_n, table, lpt, xt)


def _expert_kernel(be_ref, nu_ref, xs_ref, wg_ref, wu_ref, wd_ref, y_ref, wg_b, wu_b, wd_b):
    i = pl.program_id(0)
    live = i < nu_ref[0]

    @pl.when(jnp.logical_and(live, jnp.logical_or(i == 0, be_ref[i] != be_ref[jnp.maximum(i - 1, 0)])))
    def _():
        wg_b[...] = wg_ref[...].astype(BF16)
        wu_b[...] = wu_ref[...].astype(BF16)
        wd_b[...] = wd_ref[...].astype(BF16)

    @pl.when(live)
    def _():
        first, second = _unpack_bf16_pairs(xs_ref[...])
        xb = jnp.concatenate([first.astype(BF16), second.astype(BF16)], axis=1)
        g = jnp.dot(xb, wg_b[...], preferred_element_type=F32)
        u = jnp.dot(xb, wu_b[...], preferred_element_type=F32)
        h = g * jax.nn.sigmoid(g) * u
        y = jnp.dot(h.astype(BF16), wd_b[...], preferred_element_type=F32)
        y_ref[...] = _pack_bf16_pairs(y)


def _experts(xs, blk_exp, n_used, w_gate, w_up, w_down, br):
    p, dh = xs.shape
    d, f = w_gate.shape[1:]
    n_blocks = p // br

    def row_map(i, be, nu):
        return (jnp.minimum(i, nu[0] - 1), 0)

    def w_map(i, be, nu):
        return (be[jnp.minimum(i, nu[0] - 1)], 0, 0)

    grid_spec = pltpu.PrefetchScalarGridSpec(
        num_scalar_prefetch=2,
        grid=(n_blocks,),
        in_specs=[pl.BlockSpec((br, dh), row_map),
                  pl.BlockSpec((None, d, f), w_map),
                  pl.BlockSpec((None, d, f), w_map),
                  pl.BlockSpec((None, f, d), w_map)],
        out_specs=pl.BlockSpec((br, dh), row_map),
        scratch_shapes=[pltpu.VMEM((d, f), BF16), pltpu.VMEM((d, f), BF16), pltpu.VMEM((f, d), BF16)],
    )
    return pl.pallas_call(
        _expert_kernel,
        grid_spec=grid_spec,
        out_shape=jax.ShapeDtypeStruct((p, dh), U32),
        compiler_params=_cparams(("arbitrary",)),
        name="moe_experts",
    )(blk_exp, n_used, xs, w_gate, w_up, w_down)


def _combine_kernel(tbl_hbm, y_hbm, lp_ref, gate_ref, x_ref, sg_ref, su_ref, sd_ref, g_ref, b_ref, o_ref,
                    tbl, ysorted, acc, lpb, gb, sem_tbl, sem, *, ne, alpha):
    i = pl.program_id(0)
    last = pl.num_programs(0) - 1
    slot = lax.rem(i, 2)
    tt = x_ref.shape[0]
    half = x_ref.shape[1] // 2
    width = tbl.shape[0] // 2

    def gathers(sl):
        return lambda off, g, n: pltpu.make_async_copy(
            y_hbm.at[pl.ds(g, n), :], ysorted.at[sl, pl.ds(off, n), :], sem.at[sl])

    def prefetch(step, sl):
        fetch = pltpu.make_async_copy(tbl_hbm.at[pl.ds(step * width, width)], tbl.at[pl.ds(sl * width, width)], sem_tbl)
        fetch.start()
        fetch.wait()
        _segment_loop(tbl, sl * width, ne, gathers(sl), _start)

    @pl.when(i == 0)
    def _():
        prefetch(0, 0)

    @pl.when(i < last)
    def _():
        prefetch(i + 1, 1 - slot)

    x = x_ref[...]
    xb = x.astype(BF16)
    g = jnp.dot(xb, sg_ref[...], preferred_element_type=F32)
    u = jnp.dot(xb, su_ref[...], preferred_element_type=F32)
    h = g * jax.nn.sigmoid(g) * u
    acc[...] = jnp.dot(h.astype(BF16), sd_ref[...], preferred_element_type=F32)
    lp = lp_ref[...]
    gate = gate_ref[...]
    for k in range(TOP_K):
        lpb[k] = jnp.broadcast_to(lp[:, k:k + 1], (tt, LANES))
        gb[k] = jnp.broadcast_to(gate[:, k:k + 1], (tt, LANES))
    _segment_loop(tbl, slot * width, ne, gathers(slot), _wait)
    total = tbl[slot * width + 2 * ne]
    n_chunks = lax.shift_right_logical(total + (SORT_CHUNK - 1), SORT_CHUNK.bit_length() - 1)
    lane = lax.broadcasted_iota(I32, (tt, LANES), 1)

    def chunk(ci, carry):
        j0 = pl.multiple_of(ci * SORT_CHUNK, SORT_CHUNK)
        parts = []
        for c0 in range(0, SORT_CHUNK, LANES):
            cols = lane + (j0 + c0)
            gm = jnp.zeros((tt, LANES), F32)
            for k in range(TOP_K):
                gm = jnp.where(lpb[k] == cols, gb[k], gm)
            parts.append(gm)
        gmat = jnp.concatenate(parts, axis=1)
        g_hi = gmat.astype(BF16)
        g_lo = (gmat - g_hi.astype(F32)).astype(BF16)
        rows = j0 + lax.broadcasted_iota(I32, (SORT_CHUNK, half), 0)
        ys = jnp.where(rows < total, ysorted[slot, pl.ds(j0, SORT_CHUNK), :], jnp.uint32(0))
        first, second = _unpack_bf16_pairs(ys)
        first = first.astype(BF16)
        second = second.astype(BF16)
        acc[:, :half] += (jnp.dot(g_hi, first, preferred_element_type=F32)
                          + jnp.dot(g_lo, first, preferred_element_type=F32))
        acc[:, half:] += (jnp.dot(g_hi, second, preferred_element_type=F32)
                          + jnp.dot(g_lo, second, preferred_element_type=F32))
        return carry

    lax.fori_loop(0, n_chunks, chunk, 0)
    o_ref[...] = _ln(alpha * x + acc[...], g_ref[...], b_ref[...])


def _combine(table, y, lp, gate, xt, sw_gate, sw_up, sw_down, g2, b2, alpha, tt, ne):
    t, d = xt.shape
    f = sw_gate.shape[1]
    row = pl.BlockSpec((tt, d), lambda i: (i, 0))
    wide = pl.BlockSpec((tt, LANES), lambda i: (i, 0))
    return pl.pallas_call(
        functools.partial(_combine_kernel, ne=ne, alpha=alpha),
        grid=(t // tt,),
        in_specs=[pl.BlockSpec(memory_space=pl.ANY), pl.BlockSpec(memory_space=pl.ANY), wide, wide, row,
                  _full((d, f)), _full((d, f)), _full((f, d)), _full((1, d)), _full((1, d))],
        out_specs=row,
        out_shape=jax.ShapeDtypeStruct((t, d), F32),
        scratch_shapes=[pltpu.SMEM((2 * (2 * ne + LANES),), I32), pltpu.VMEM((2, _sorted_rows(tt, ne), d // 2), U32),
                        pltpu.VMEM((tt, d), F32), pltpu.VMEM((TOP_K, tt, LANES), I32), pltpu.VMEM((TOP_K, tt, LANES), F32),
                        pltpu.SemaphoreType.DMA, pltpu.SemaphoreType.DMA((2,))],
        compiler_params=_cparams(("arbitrary",)),
        name="moe_combine_ln",
    )(table, y, lp, gate, xt, sw_gate, sw_up, sw_down, g2, b2)


def _moe_layer(xt, router_w, router_bias, w_gate, w_up, w_down, sw_gate, sw_up, sw_down, g2, b2, alpha, cfg):
    t, d = xt.shape
    ne = router_w.shape[1]
    br = cfg["moe_rows"]
    tt = cfg["moe_tile"]
    n_tiles = t // tt
    lp, lpt, gate, cnt, base, tot = _router(xt, router_w, router_bias, tt)
    cnt = cnt.reshape(n_tiles, ne).astype(I32)
    tot = tot.reshape(ne).astype(I32)
    padded = (tot + br - 1) // br * br
    pends = jnp.cumsum(padded)
    pstarts = pends - padded
    seg_start = pstarts[None, :] + base.reshape(n_tiles, ne).astype(I32)
    tile_rows = jnp.sum(cnt, axis=1, keepdims=True)
    table = jnp.concatenate([cnt, seg_start, tile_rows, jnp.zeros((n_tiles, LANES - 1), I32)], axis=1).reshape(-1)
    n_blocks = -(-(t * TOP_K + ne * n_tiles * SEG_ALIGN) // br) + ne
    blk_start = jnp.arange(n_blocks, dtype=I32) * br
    blk_exp = jnp.minimum(jnp.sum(pends[None, :] <= blk_start[:, None], axis=1), ne - 1).astype(I32)
    n_used = (pends[-1:] // br).astype(I32)
    xs = _dispatch(xt, lpt, table, pstarts + tot, padded - tot, n_blocks * br, tt, br)
    y = _experts(xs, blk_exp, n_used, w_gate, w_up, w_down, br)
    return _combine(table, y, lp, gate, xt, sw_gate.astype(BF16), sw_up.astype(BF16), sw_down.astype(BF16),
                    g2, b2, alpha, tt, ne)


def _config(b, s, d):
    return dict(row_tm=min(512, s), conv_tm=min(512, s), conv_cw=min(256, d), proj_rows=min(512, s), attn_out_tm=256,
                moe_rows=512, moe_tile=min(256, b * s))


def kernel(x, positions, ln1_g, ln1_b, ln2_g, ln2_b, conv_w_in, conv_b_in, conv_w_dw, conv_b_dw, conv_ln_g, conv_ln_b, conv_w_out, conv_b_out, w_kv, attn_w_q, attn_w_o, router_w, router_bias, exp_w_gate, exp_w_up, exp_w_down, sh_w_gate, sh_w_up, sh_w_down):
    b, s, d = x.shape
    t = b * s
    depth = ln1_g.shape[0]
    n_conv = conv_w_in.shape[0]
    alpha = (2.0 * depth) ** 0.25
    cfg = _config(b, s, d)
    n_groups = len(WINDOW_DILATIONS)
    gw = attn_w_o.shape[1]
    q_width = n_groups * gw
    vec = lambda a: a.reshape(1, -1)

    xt = x.reshape(t, d)
    kv = None
    tables = None
    for layer in range(depth):
        g1, b1 = vec(ln1_g[layer]), vec(ln1_b[layer])
        if layer < n_conv:
            h = _glu(xt, conv_w_in[layer].astype(BF16), vec(conv_b_in[layer]), cfg["row_tm"])
            c = _dwconv(h, conv_w_dw[layer], vec(conv_b_dw[layer]), b, s, cfg["conv_tm"], cfg["conv_cw"])
            xt = _conv_out(c, xt, vec(conv_ln_g[layer]), vec(conv_ln_b[layer]), conv_w_out[layer].astype(BF16),
                           vec(conv_b_out[layer]), g1, b1, alpha, cfg["row_tm"])
        else:
            j = layer - n_conv
            if tables is None:
                cos, sin = _rope_tables(positions, cfg["row_tm"])
                tables = (cos.reshape(b, s, LANES), sin.reshape(b, s, LANES))
            x3 = xt.reshape(b, s, d)
            wq = attn_w_q[j]
            outs, lses = [], []
            new_kv = []
            for g, (win, dil) in enumerate(WINDOW_DILATIONS):
                assert win // dil == ATTN_BLOCK
                cols = slice(g * gw, (g + 1) * gw)
                if kv is None:
                    w = jnp.concatenate([wq[:, cols], w_kv[:, cols], w_kv[:, q_width:][:, cols]], axis=1).astype(BF16)
                    q, kg, vg = _project(x3, *tables, w, (True, True, False), dil, cfg["proj_rows"])
                    new_kv.append((kg, vg))
                else:
                    (q,) = _project(x3, *tables, wq[:, cols].astype(BF16), (True,), dil, cfg["proj_rows"])
                    kg, vg = kv[g]
                o, lse = _attention(q, kg, vg, dil, min(512, s // dil))
                outs.append(o)
                lses.append(lse)
            if kv is None:
                kv = new_kv
            xt = _attn_out(outs, lses, x3, attn_w_o[j].astype(BF16), g1, b1, alpha, cfg["attn_out_tm"]).reshape(t, d)
        xt = _moe_layer(xt, router_w[layer], router_bias[layer], exp_w_gate[layer], exp_w_up[layer],
                        exp_w_down[layer], sh_w_gate[layer], sh_w_up[layer], sh_w_down[layer],
                        vec(ln2_g[layer]), vec(ln2_b[layer]), alpha, cfg)
    return xt.reshape(b, s, d)
```

```python
import functools
import math

import jax
import jax.numpy as jnp
from jax import lax
from jax.experimental import pallas as pl
from jax.experimental.pallas import tpu as pltpu

F32 = jnp.float32
BF16 = jnp.bfloat16
I32 = jnp.int32
U32 = jnp.uint32

LANES = 128
HEAD_DIM = 128
ROT_DIM = HEAD_DIM // 4
ROPE_THETA = 500000.0
ATTN_BLOCK = 128
WINDOW_DILATIONS = ((128, 1), (512, 4), (2048, 16))
NEG_INF = -1e30
TOP_K = 8
ROUTED_SCALE = 2.5
LN_EPS = 1e-5
CONV_HALO = 32
CONV_ROWS = 64
VMEM_LIMIT = 56 * 1024 * 1024
SEG_ALIGN = 8
SORT_CHUNK = 256


def _cparams(sem):
    return pltpu.CompilerParams(dimension_semantics=sem, vmem_limit_bytes=VMEM_LIMIT)


def _ln(y, g, b):
    mu = jnp.mean(y, axis=-1, keepdims=True)
    d = y - mu
    var = jnp.mean(d * d, axis=-1, keepdims=True)
    return d * lax.rsqrt(var + LN_EPS) * g + b


def _full(shape):
    n = len(shape)
    return pl.BlockSpec(shape, lambda *_: (0,) * n)


def _glu_kernel(x_ref, w_ref, b_ref, o_ref):
    d = o_ref.shape[-1]
    h = jnp.dot(x_ref[...].astype(BF16), w_ref[...], preferred_element_type=F32) + b_ref[...]
    o_ref[...] = h[:, :d] * jax.nn.sigmoid(h[:, d:])


def _glu(xt, w_in, b_in, tm):
    t, d = xt.shape
    return pl.pallas_call(
        _glu_kernel,
        grid=(t // tm,),
        in_specs=[pl.BlockSpec((tm, d), lambda i: (i, 0)), _full((d, 2 * d)), _full((1, 2 * d))],
        out_specs=pl.BlockSpec((tm, d), lambda i: (i, 0)),
        out_shape=jax.ShapeDtypeStruct((t, d), F32),
        compiler_params=_cparams(("arbitrary",)),
        name="conv_glu",
    )(xt, w_in, b_in)


def _dwconv_kernel(h_ref, w_ref, b_ref, o_ref, buf, win, *, rows):
    s = pl.program_id(2)
    tm = h_ref.shape[0]
    width = w_ref.shape[0]
    off = CONV_HALO - (width - 1)

    @pl.when(s == 0)
    def _():
        buf[0:CONV_HALO, :] = jnp.zeros((CONV_HALO, buf.shape[1]), F32)

    @pl.when(s > 0)
    def _():
        buf[0:CONV_HALO, :] = buf[tm:tm + CONV_HALO, :]

    buf[CONV_HALO:CONV_HALO + tm, :] = h_ref[...]
    sub = 8
    for c0 in range(0, buf.shape[1], LANES):
        cs = slice(c0, c0 + LANES)
        for r0 in range(0, tm, rows):
            acc = None
            for phase in range(min(sub, width)):
                n_taps = (width - 1 - phase) // sub + 1
                start = r0 + off + phase
                span = rows + sub * (n_taps - 1)
                win[0:span, :] = buf[start:start + span, cs]
                for a in range(n_taps):
                    k = sub * a + phase
                    term = win[sub * a:sub * a + rows, :] * w_ref[k:k + 1, cs]
                    acc = term if acc is None else acc + term
            o_ref[r0:r0 + rows, cs] = acc + b_ref[:, cs]


def _dwconv(h, w_dw, b_dw, b, s, tm, cw):
    t, d = h.shape
    h3 = h.reshape(b, s, d)
    width = w_dw.shape[0]
    out = pl.pallas_call(
        functools.partial(_dwconv_kernel, rows=CONV_ROWS),
        grid=(b, d // cw, s // tm),
        in_specs=[pl.BlockSpec((None, tm, cw), lambda bi, c, si: (bi, si, c)),
                  pl.BlockSpec((width, cw), lambda bi, c, si: (0, c)),
                  pl.BlockSpec((1, cw), lambda bi, c, si: (0, c))],
        out_specs=pl.BlockSpec((None, tm, cw), lambda bi, c, si: (bi, si, c)),
        out_shape=jax.ShapeDtypeStruct((b, s, d), F32),
        scratch_shapes=[pltpu.VMEM((CONV_HALO + tm, cw), F32), pltpu.VMEM((CONV_ROWS + CONV_HALO, LANES), F32)],
        compiler_params=_cparams(("arbitrary", "arbitrary", "arbitrary")),
        name="conv_depthwise",
    )(h3, w_dw, b_dw)
    return out.reshape(t, d)


def _conv_out_kernel(c_ref, x_ref, cg_ref, cb_ref, w_ref, bo_ref, g_ref, b_ref, o_ref, *, alpha):
    u = _ln(c_ref[...], cg_ref[...], cb_ref[...])
    u = u * jax.nn.sigmoid(u)
    mix = jnp.dot(u.astype(BF16), w_ref[...], preferred_element_type=F32) + bo_ref[...]
    o_ref[...] = _ln(alpha * x_ref[...] + mix, g_ref[...], b_ref[...])


def _conv_out(c, xt, cg, cb, w_out, b_out, g1, b1, alpha, tm):
    t, d = xt.shape
    row = pl.BlockSpec((tm, d), lambda i: (i, 0))
    vec = _full((1, d))
    return pl.pallas_call(
        functools.partial(_conv_out_kernel, alpha=alpha),
        grid=(t // tm,),
        in_specs=[row, row, vec, vec, _full((d, d)), vec, vec, vec],
        out_specs=row,
        out_shape=jax.ShapeDtypeStruct((t, d), F32),
        compiler_params=_cparams(("arbitrary",)),
        name="conv_out_ln",
    )(c, xt, cg, cb, w_out, b_out, g1, b1)


def _rope_kernel(pos_ref, invf_ref, c_ref, s_ref):
    ang = pos_ref[...].astype(F32) * invf_ref[...]
    lane = lax.broadcasted_iota(I32, ang.shape, 1)
    half = ROT_DIM // 2
    c_ref[...] = jnp.where(lane < ROT_DIM, jnp.cos(ang), 1.0)
    sn = jnp.sin(ang)
    s_ref[...] = jnp.where(lane < half, -sn, jnp.where(lane < ROT_DIM, sn, 0.0))


def _rope_tables(positions, tm):
    t = positions.size
    half = ROT_DIM // 2
    inv_freq = ROPE_THETA ** (-jnp.arange(half, dtype=F32) * 2.0 / ROT_DIM)
    invf = jnp.zeros((1, LANES), F32).at[0, :half].set(inv_freq).at[0, half:ROT_DIM].set(inv_freq)
    out = jax.ShapeDtypeStruct((t, LANES), F32)
    return pl.pallas_call(
        _rope_kernel,
        grid=(t // tm,),
        in_specs=[pl.BlockSpec((tm, 1), lambda i: (i, 0)), _full((1, LANES))],
        out_specs=[pl.BlockSpec((tm, LANES), lambda i: (i, 0))] * 2,
        out_shape=[out, out],
        compiler_params=_cparams(("arbitrary",)),
        name="rope_tables",
    )(positions.reshape(t, 1), invf)


def _proj_kernel(x_ref, c_ref, s_ref, *rest, rot, heads, dil):
    n_out = len(rot)
    w_refs, o_refs = rest[:n_out], rest[n_out:2 * n_out]
    lhs, tabs, xc = rest[2 * n_out:]
    tm = o_refs[0].shape[1]
    n_lane_tiles = xc.shape[0]
    if dil > 1:
        for c in range(n_lane_tiles):
            xc[c] = x_ref[:, c * LANES:(c + 1) * LANES]
    for r in range(dil):
        rows = slice(r * tm, (r + 1) * tm)
        if dil > 1:
            src = pl.ds(r, tm, stride=dil)
            for c in range(n_lane_tiles):
                lhs[rows, c * LANES:(c + 1) * LANES] = xc[c, src, :].astype(BF16)
        else:
            src = slice(None)
            lhs[rows, :] = x_ref[...].astype(BF16)
        tabs[0, rows, :] = c_ref[src, :]
        tabs[1, rows, :] = s_ref[src, :]
    cos = tabs[0]
    sin = tabs[1]
    lane = lax.broadcasted_iota(I32, cos.shape, 1)
    first = lane < ROT_DIM // 2
    for j, o_ref in enumerate(o_refs):
        y = jnp.dot(lhs[...], w_refs[j][...], preferred_element_type=F32)
        for h in range(heads):
            t = y[:, h * HEAD_DIM:(h + 1) * HEAD_DIM]
            if rot[j]:
                partner = jnp.where(first, pltpu.roll(t, HEAD_DIM - ROT_DIM // 2, 1), pltpu.roll(t, ROT_DIM // 2, 1))
                t = t * cos + partner * sin
            t = t.astype(BF16)
            for r in range(dil):
                o_ref[r, :, h * HEAD_DIM:(h + 1) * HEAD_DIM] = t[r * tm:(r + 1) * tm]


def _project(x3, cos3, sin3, weights, rot, dil, rows, gw):
    b, s, d = x3.shape
    l = s // dil
    tm = rows // dil
    n_out = len(rot)
    out = jax.ShapeDtypeStruct((b, dil, l, gw), BF16)
    tab = pl.BlockSpec((None, rows, LANES), lambda bi, i: (bi, i, 0))
    w_specs = [pl.BlockSpec((d, gw), lambda bi, i, blk=blk: (0, blk)) for _, blk in weights]
    return pl.pallas_call(
        functools.partial(_proj_kernel, rot=rot, heads=gw // HEAD_DIM, dil=dil),
        grid=(b, s // rows),
        in_specs=[pl.BlockSpec((None, rows, d), lambda bi, i: (bi, i, 0)), tab, tab] + w_specs,
        out_specs=[pl.BlockSpec((None, dil, tm, gw), lambda bi, i: (bi, 0, i, 0))] * n_out,
        out_shape=[out] * n_out,
        scratch_shapes=[pltpu.VMEM((rows, d), BF16), pltpu.VMEM((2, rows, LANES), F32),
                        pltpu.VMEM((d // LANES, rows, LANES), F32)],
        compiler_params=_cparams(("arbitrary", "arbitrary")),
        name=f"proj_dil{dil}_n{n_out}",
    )(x3, cos3, sin3, *[w for w, _ in weights])


def _attn_kernel(q_ref, kc_ref, kh_ref, vc_ref, vh_ref, o_ref, lse_ref, kbuf, vbuf, *, heads):
    i = pl.program_id(2)
    tq = q_ref.shape[0]
    blk = ATTN_BLOCK
    kbuf[0:blk, :] = kh_ref[...]
    kbuf[blk:blk + tq, :] = kc_ref[...]
    vbuf[0:blk, :] = vh_ref[...]
    vbuf[blk:blk + tq, :] = vc_ref[...]
    scale = 1.0 / math.sqrt(HEAD_DIM)
    row = lax.broadcasted_iota(I32, (blk, 2 * blk), 0)
    col = lax.broadcasted_iota(I32, (blk, 2 * blk), 1)
    band = jnp.logical_and(col >= row, col <= row + blk)
    first_band = jnp.logical_and(band, jnp.logical_or(col >= blk, i > 0))
    lane = lax.broadcasted_iota(I32, (blk, LANES), 1)
    dims = (((1,), (1,)), ((), ()))
    for n in range(tq // blk):
        rs = slice(n * blk, (n + 1) * blk)
        ks = slice(n * blk, (n + 2) * blk)
        mask = first_band if n == 0 else band
        lse_tile = jnp.zeros((blk, LANES), F32)
        for h in range(heads):
            cs = slice(h * HEAD_DIM, (h + 1) * HEAD_DIM)
            s = lax.dot_general(q_ref[rs, cs], kbuf[ks, cs], dims, preferred_element_type=F32) * scale
            s = jnp.where(mask, s, NEG_INF)
            m = jnp.max(s, axis=1, keepdims=True)
            p = jnp.exp(s - m)
            den = jnp.sum(p, axis=1, keepdims=True)
            o = jnp.dot(p.astype(BF16), vbuf[ks, cs], preferred_element_type=F32)
            o_ref[rs, cs] = o / den
            lse_tile = jnp.where(lane == h, m + jnp.log(den), lse_tile)
        lse_ref[rs, :] = lse_tile


def _attention(q, k, v, dil, tq):
    b, _, l, gw = q.shape
    nb = tq // ATTN_BLOCK
    cur = pl.BlockSpec((None, None, tq, gw), lambda bi, r, i: (bi, r, i, 0))
    halo = pl.BlockSpec((None, None, ATTN_BLOCK, gw), lambda bi, r, i: (bi, r, jnp.maximum(i * nb - 1, 0), 0))
    o, lse = pl.pallas_call(
        functools.partial(_attn_kernel, heads=gw // HEAD_DIM),
        grid=(b, dil, l // tq),
        in_specs=[cur, cur, halo, cur, halo],
        out_specs=[pl.BlockSpec((None, None, tq, gw), lambda bi, r, i: (bi, r, i, 0)),
                   pl.BlockSpec((None, None, tq, LANES), lambda bi, r, i: (bi, r, i, 0))],
        out_shape=[jax.ShapeDtypeStruct((b, dil, l, gw), F32),
                   jax.ShapeDtypeStruct((b, dil, l, LANES), F32)],
        scratch_shapes=[pltpu.VMEM((ATTN_BLOCK + tq, gw), BF16), pltpu.VMEM((ATTN_BLOCK + tq, gw), BF16)],
        compiler_params=_cparams(("arbitrary", "arbitrary", "arbitrary")),
        name=f"attn_dil{dil}",
    )(q, k, k, v, v)
    return o, lse


def _attn_out_kernel(o0_ref, o1_ref, o2_ref, l0_ref, l1_ref, l2_ref, x_ref, w_ref, g_ref, b_ref,
                     out_ref, onat, lnat, mixed, *, alpha, heads, dils):
    tm = x_ref.shape[0]
    for gi, (o_ref, l_ref) in enumerate(((o0_ref, l0_ref), (o1_ref, l1_ref), (o2_ref, l2_ref))):
        dil = dils[gi]
        for r in range(dil):
            dst = pl.ds(r, tm // dil, stride=dil) if dil > 1 else slice(None)
            for h in range(heads):
                onat[gi * heads + h, dst, :] = o_ref[r, :, h * HEAD_DIM:(h + 1) * HEAD_DIM]
            lnat[gi, dst, :] = l_ref[r]
    l0, l1, l2 = lnat[0], lnat[1], lnat[2]
    m = jnp.maximum(jnp.maximum(l0, l1), l2)
    e0, e1, e2 = jnp.exp(l0 - m), jnp.exp(l1 - m), jnp.exp(l2 - m)
    den = e0 + e1 + e2
    w0, w1, w2 = e0 / den, e1 / den, e2 / den
    for h in range(heads):
        cs = slice(h * HEAD_DIM, (h + 1) * HEAD_DIM)
        o = w0[:, h:h + 1] * onat[h] + w1[:, h:h + 1] * onat[heads + h] + w2[:, h:h + 1] * onat[2 * heads + h]
        mixed[:, cs] = o.astype(BF16)
    mix = jnp.dot(mixed[...], w_ref[...], preferred_element_type=F32)
    out_ref[...] = _ln(alpha * x_ref[...] + mix, g_ref[...], b_ref[...])


def _attn_out(outs, lses, x3, w_o, g1, b1, alpha, tm):
    b, s, d = x3.shape
    gw = w_o.shape[0]
    dils = tuple(o.shape[1] for o in outs)
    ospecs = [pl.BlockSpec((None, dil, tm // dil, gw), lambda bi, i: (bi, 0, i, 0)) for dil in dils]
    lspecs = [pl.BlockSpec((None, dil, tm // dil, LANES), lambda bi, i: (bi, 0, i, 0)) for dil in dils]
    xrow = pl.BlockSpec((None, tm, d), lambda bi, i: (bi, i, 0))
    n = len(dils)
    return pl.pallas_call(
        functools.partial(_attn_out_kernel, alpha=alpha, heads=gw // HEAD_DIM, dils=dils),
        grid=(b, s // tm),
        in_specs=ospecs + lspecs + [xrow, _full((gw, d)), _full((1, d)), _full((1, d))],
        out_specs=xrow,
        out_shape=jax.ShapeDtypeStruct((b, s, d), F32),
        scratch_shapes=[pltpu.VMEM((n * gw // HEAD_DIM, tm, HEAD_DIM), F32), pltpu.VMEM((n, tm, LANES), F32),
                        pltpu.VMEM((tm, gw), BF16)],
        compiler_params=_cparams(("arbitrary", "arbitrary")),
        name="attn_out_ln",
    )(*outs, *lses, x3, w_o, g1, b1)


def _pack_bf16_pairs(v):
    half = v.shape[1] // 2
    hi = lax.bitcast_convert_type(v[:, :half].astype(BF16).astype(F32), U32)
    lo = lax.bitcast_convert_type(v[:, half:].astype(BF16).astype(F32), U32)
    return hi | (lo >> 16)


def _unpack_bf16_pairs(p):
    first = lax.bitcast_convert_type(p & jnp.uint32(0xFFFF0000), F32)
    second = lax.bitcast_convert_type(p << 16, F32)
    return first, second


def _router_kernel(x_ref, rw_ref, rb_ref, lp_ref, lpt_ref, gate_ref, cnt_ref, base_ref, tot_ref, carry):
    @pl.when(pl.program_id(0) == 0)
    def _():
        carry[...] = jnp.zeros(carry.shape, F32)

    tm = x_ref.shape[0]
    ne = rw_ref.shape[1]
    logits = lax.dot_general(x_ref[...], rw_ref[...], (((1,), (0,)), ((), ())),
                             precision=lax.Precision.HIGHEST, preferred_element_type=F32)
    scores = jax.nn.sigmoid(logits)
    sel = scores + rb_ref[...]
    lane = lax.broadcasted_iota(I32, (tm, ne), 1)
    chosen = jnp.zeros((tm, ne), F32)
    gsum = jnp.zeros((tm, 1), F32)
    picks = []
    for _ in range(TOP_K):
        best = jnp.max(sel, axis=1, keepdims=True)
        j = jnp.min(jnp.where(sel == best, lane, ne), axis=1, keepdims=True)
        onehot = lane == j
        sc = jnp.sum(jnp.where(onehot, scores, 0.0), axis=1, keepdims=True)
        sel = jnp.where(onehot, -jnp.inf, sel)
        chosen = chosen + onehot.astype(F32)
        gsum = gsum + sc
        picks.append((onehot, sc))
    r = lax.broadcasted_iota(I32, (tm, tm), 0)
    c = lax.broadcasted_iota(I32, (tm, tm), 1)
    rank = jnp.dot((c < r).astype(BF16), chosen.astype(BF16), preferred_element_type=F32)
    cnt = jnp.sum(chosen, axis=0, keepdims=True)
    cnt_al = jnp.floor((cnt + (SEG_ALIGN - 1)) / SEG_ALIGN) * SEG_ALIGN
    er = lax.broadcasted_iota(I32, (ne, ne), 0)
    ec = lax.broadcasted_iota(I32, (ne, ne), 1)
    seg_off = jnp.dot(jnp.broadcast_to(cnt_al, (8, ne)).astype(BF16), (er < ec).astype(BF16),
                      preferred_element_type=F32)[0:1]
    slot = rank + seg_off
    out_lane = lax.broadcasted_iota(I32, (tm, LANES), 1)
    lp_t = jnp.zeros((tm, LANES), I32)
    gate_t = jnp.zeros((tm, LANES), F32)
    for k, (onehot, sc) in enumerate(picks):
        row = jnp.sum(jnp.where(onehot, slot, 0.0), axis=1, keepdims=True)
        lp_t = jnp.where(out_lane == k, row.astype(I32), lp_t)
        gate_t = jnp.where(out_lane == k, sc / gsum * ROUTED_SCALE, gate_t)
    lp_ref[...] = lp_t
    lpt_ref[...] = lp_t.T[0:TOP_K]
    gate_ref[...] = gate_t
    cnt_ref[...] = cnt_al
    base_ref[...] = carry[...]
    carry[...] = carry[...] + cnt_al
    tot_ref[...] = carry[...]


def _router(xt, router_w, router_bias, tm):
    t, d = xt.shape
    ne = router_w.shape[1]
    n_tiles = t // tm
    wide = pl.BlockSpec((tm, LANES), lambda i: (i, 0))
    per_tile = pl.BlockSpec((None, 1, ne), lambda i: (i, 0, 0))
    return pl.pallas_call(
        _router_kernel,
        grid=(n_tiles,),
        in_specs=[pl.BlockSpec((tm, d), lambda i: (i, 0)), _full((d, ne)), _full((1, ne))],
        out_specs=[wide, pl.BlockSpec((None, TOP_K, tm), lambda i: (i, 0, 0)), wide, per_tile, per_tile, _full((1, ne))],
        out_shape=[jax.ShapeDtypeStruct((t, LANES), I32), jax.ShapeDtypeStruct((n_tiles, TOP_K, tm), I32),
                   jax.ShapeDtypeStruct((t, LANES), F32), jax.ShapeDtypeStruct((n_tiles, 1, ne), F32),
                   jax.ShapeDtypeStruct((n_tiles, 1, ne), F32), jax.ShapeDtypeStruct((1, ne), F32)],
        scratch_shapes=[pltpu.VMEM((1, ne), F32)],
        compiler_params=_cparams(("arbitrary",)),
        name="moe_router",
    )(xt, router_w, router_bias.reshape(1, ne))


def _segment_loop(tbl, base, ne, make_copy, action):
    def body(e, off):
        n = pl.multiple_of(tbl[base + e], SEG_ALIGN)
        g = pl.multiple_of(tbl[base + ne + e], SEG_ALIGN)
        off = pl.multiple_of(off, SEG_ALIGN)

        @pl.when(n > 0)
        def _():
            action(make_copy(off, g, n))

        return off + n

    lax.fori_loop(0, ne, body, 0)


def _start(cp):
    cp.start()


def _wait(cp):
    cp.wait()


def _dispatch_kernel(fs_ref, fn_ref, tbl, lpt_ref, x_ref, xs_hbm, sorted_buf, zbuf, sem, *, ne):
    i = pl.program_id(0)
    last = pl.num_programs(0) - 1
    slot = lax.rem(i, 2)
    tt = x_ref.shape[0]
    width = _table_width(ne)

    @pl.when(i == 0)
    def _():
        zbuf[...] = jnp.zeros(zbuf.shape, zbuf.dtype)

        def visit(action):
            def body(e, carry):
                n = pl.multiple_of(fn_ref[e], SEG_ALIGN)
                s = pl.multiple_of(fs_ref[e], SEG_ALIGN)

                @pl.when(n > 0)
                def _():
                    action(pltpu.make_async_copy(zbuf.at[pl.ds(0, n), :], xs_hbm.at[pl.ds(s, n), :], sem.at[0]))

                return carry
            return body

        lax.fori_loop(0, ne, visit(_start), 0)
        lax.fori_loop(0, ne, visit(_wait), 0)

    xb = x_ref[...].astype(BF16)
    lpt = lpt_ref[...]
    n_chunks = lax.shift_right_logical(tbl[i * width + 2 * ne] + (SORT_CHUNK - 1), SORT_CHUNK.bit_length() - 1)

    def chunk(ci, carry):
        j0 = pl.multiple_of(ci * SORT_CHUNK, SORT_CHUNK)
        rows = j0 + lax.broadcasted_iota(I32, (SORT_CHUNK, tt), 0)
        hit = lpt[0:1, :] == rows
        for k in range(1, TOP_K):
            hit = jnp.logical_or(hit, lpt[k:k + 1, :] == rows)
        srt = jnp.dot(jnp.where(hit, 1.0, 0.0).astype(BF16), xb, preferred_element_type=F32)
        sorted_buf[slot, pl.ds(j0, SORT_CHUNK), :] = _pack_bf16_pairs(srt)
        return carry

    lax.fori_loop(0, n_chunks, chunk, 0)

    def writes(sl):
        return lambda off, g, n: pltpu.make_async_copy(
            sorted_buf.at[sl, pl.ds(off, n), :], xs_hbm.at[pl.ds(g, n), :], sem.at[sl])

    _segment_loop(tbl, i * width, ne, writes(slot), _start)

    @pl.when(i > 0)
    def _():
        _segment_loop(tbl, (i - 1) * width, ne, writes(1 - slot), _wait)

    @pl.when(i == last)
    def _():
        _segment_loop(tbl, i * width, ne, writes(slot), _wait)


def _sorted_rows(tt, ne):
    return -(-(tt * TOP_K + ne * SEG_ALIGN) // SORT_CHUNK) * SORT_CHUNK


def _table_width(ne):
    return 2 * ne + SEG_ALIGN


def _dispatch(xt, lpt, table, fill_start, fill_n, n_rows, tt, br):
    t, d = xt.shape
    ne = fill_start.shape[0]
    grid_spec = pltpu.PrefetchScalarGridSpec(
        num_scalar_prefetch=3,
        grid=(t // tt,),
        in_specs=[pl.BlockSpec((None, TOP_K, tt), lambda i, *_: (i, 0, 0)),
                  pl.BlockSpec((tt, d), lambda i, *_: (i, 0))],
        out_specs=pl.BlockSpec(memory_space=pl.ANY),
        scratch_shapes=[pltpu.VMEM((2, _sorted_rows(tt, ne), d // 2), U32),
                        pltpu.VMEM((br, d // 2), U32), pltpu.SemaphoreType.DMA((2,))],
    )
    return pl.pallas_call(
        functools.partial(_dispatch_kernel, ne=ne),
        grid_spec=grid_spec,
        out_shape=jax.ShapeDtypeStruct((n_rows, d // 2), U32),
        compiler_params=_cparams(("arbitrary",)),
        name="moe_dispatch",
    )(fill_start, fill_n, table, lpt, xt)


def _expert_kernel(be_ref, nu_ref, xs_ref, wg_ref, wu_ref, wd_ref, y_ref, wg_b, wu_b, wd_b):
    i = pl.program_id(0)
    live = i < nu_ref[0]

    @pl.when(jnp.logical_and(live, jnp.logical_or(i == 0, be_ref[i] != be_ref[jnp.maximum(i - 1, 0)])))
    def _():
        wg_b[...] = wg_ref[...].astype(BF16)
        wu_b[...] = wu_ref[...].astype(BF16)
        wd_b[...] = wd_ref[...].astype(BF16)

    @pl.when(live)
    def _():
        first, second = _unpack_bf16_pairs(xs_ref[...])
        xb = jnp.concatenate([first.astype(BF16), second.astype(BF16)], axis=1)
        g = jnp.dot(xb, wg_b[...], preferred_element_type=F32)
        u = jnp.dot(xb, wu_b[...], preferred_element_type=F32)
        h = g * jax.nn.sigmoid(g) * u
        y = jnp.dot(h.astype(BF16), wd_b[...], preferred_element_type=F32)
        y_ref[...] = _pack_bf16_pairs(y)


def _experts(xs, blk_exp, n_used, w_gate, w_up, w_down, br):
    p, dh = xs.shape
    d, f = w_gate.shape[1:]
    n_blocks = p // br

    def row_map(i, be, nu):
        return (jnp.minimum(i, nu[0] - 1), 0)

    def w_map(i, be, nu):
        return (be[jnp.minimum(i, nu[0] - 1)], 0, 0)

    grid_spec = pltpu.PrefetchScalarGridSpec(
        num_scalar_prefetch=2,
        grid=(n_blocks,),
        in_specs=[pl.BlockSpec((br, dh), row_map),
                  pl.BlockSpec((None, d, f), w_map),
                  pl.BlockSpec((None, d, f), w_map),
                  pl.BlockSpec((None, f, d), w_map)],
        out_specs=pl.BlockSpec((br, dh), row_map),
        scratch_shapes=[pltpu.VMEM((d, f), BF16), pltpu.VMEM((d, f), BF16), pltpu.VMEM((f, d), BF16)],
    )
    return pl.pallas_call(
        _expert_kernel,
        grid_spec=grid_spec,
        out_shape=jax.ShapeDtypeStruct((p, dh), U32),
        compiler_params=_cparams(("arbitrary",)),
        name="moe_experts",
    )(blk_exp, n_used, xs, w_gate, w_up, w_down)


def _combine_kernel(tbl, y_hbm, lp_ref, gate_ref, x_ref, sg_ref, su_ref, sd_ref, g_ref, b_ref, o_ref,
                    ysorted, acc, lpb, gb, sem, *, ne, alpha):
    i = pl.program_id(0)
    last = pl.num_programs(0) - 1
    slot = lax.rem(i, 2)
    tt = x_ref.shape[0]
    half = x_ref.shape[1] // 2
    width = _table_width(ne)

    def gathers(sl):
        return lambda off, g, n: pltpu.make_async_copy(
            y_hbm.at[pl.ds(g, n), :], ysorted.at[sl, pl.ds(off, n), :], sem.at[sl])

    @pl.when(i == 0)
    def _():
        _segment_loop(tbl, 0, ne, gathers(0), _start)

    @pl.when(i < last)
    def _():
        _segment_loop(tbl, (i + 1) * width, ne, gathers(1 - slot), _start)

    x = x_ref[...]
    xb = x.astype(BF16)
    g = jnp.dot(xb, sg_ref[...], preferred_element_type=F32)
    u = jnp.dot(xb, su_ref[...], preferred_element_type=F32)
    h = g * jax.nn.sigmoid(g) * u
    acc[...] = jnp.dot(h.astype(BF16), sd_ref[...], preferred_element_type=F32)
    lp = lp_ref[...]
    gate = gate_ref[...]
    for k in range(TOP_K):
        lpb[k] = jnp.broadcast_to(lp[:, k:k + 1], (tt, LANES))
        gb[k] = jnp.broadcast_to(gate[:, k:k + 1], (tt, LANES))
    _segment_loop(tbl, i * width, ne, gathers(slot), _wait)
    total = tbl[i * width + 2 * ne]
    n_chunks = lax.shift_right_logical(total + (SORT_CHUNK - 1), SORT_CHUNK.bit_length() - 1)
    lane = lax.broadcasted_iota(I32, (tt, LANES), 1)

    def chunk(ci, carry):
        j0 = pl.multiple_of(ci * SORT_CHUNK, SORT_CHUNK)
        parts = []
        for c0 in range(0, SORT_CHUNK, LANES):
            cols = lane + (j0 + c0)
            gm = jnp.zeros((tt, LANES), F32)
            for k in range(TOP_K):
                gm = jnp.where(lpb[k] == cols, gb[k], gm)
            parts.append(gm)
        gmat = jnp.concatenate(parts, axis=1)
        g_hi = gmat.astype(BF16)
        g_lo = (gmat - g_hi.astype(F32)).astype(BF16)
        rows = j0 + lax.broadcasted_iota(I32, (SORT_CHUNK, half), 0)
        ys = jnp.where(rows < total, ysorted[slot, pl.ds(j0, SORT_CHUNK), :], jnp.uint32(0))
        first, second = _unpack_bf16_pairs(ys)
        first = first.astype(BF16)
        second = second.astype(BF16)
        acc[:, :half] += (jnp.dot(g_hi, first, preferred_element_type=F32)
                          + jnp.dot(g_lo, first, preferred_element_type=F32))
        acc[:, half:] += (jnp.dot(g_hi, second, preferred_element_type=F32)
                          + jnp.dot(g_lo, second, preferred_element_type=F32))
        return carry

    lax.fori_loop(0, n_chunks, chunk, 0)
    o_ref[...] = _ln(alpha * x + acc[...], g_ref[...], b_ref[...])


def _combine(table, y, lp, gate, xt, sw_gate, sw_up, sw_down, g2, b2, alpha, tt, ne):
    t, d = xt.shape
    f = sw_gate.shape[1]
    row = pl.BlockSpec((tt, d), lambda i, *_: (i, 0))
    wide = pl.BlockSpec((tt, LANES), lambda i, *_: (i, 0))
    grid_spec = pltpu.PrefetchScalarGridSpec(
        num_scalar_prefetch=1,
        grid=(t // tt,),
        in_specs=[pl.BlockSpec(memory_space=pl.ANY), wide, wide, row,
                  _full((d, f)), _full((d, f)), _full((f, d)), _full((1, d)), _full((1, d))],
        out_specs=row,
        scratch_shapes=[pltpu.VMEM((2, _sorted_rows(tt, ne), d // 2), U32),
                        pltpu.VMEM((tt, d), F32), pltpu.VMEM((TOP_K, tt, LANES), I32), pltpu.VMEM((TOP_K, tt, LANES), F32),
                        pltpu.SemaphoreType.DMA((2,))],
    )
    return pl.pallas_call(
        functools.partial(_combine_kernel, ne=ne, alpha=alpha),
        grid_spec=grid_spec,
        out_shape=jax.ShapeDtypeStruct((t, d), F32),
        compiler_params=_cparams(("arbitrary",)),
        name="moe_combine_ln",
    )(table, y, lp, gate, xt, sw_gate, sw_up, sw_down, g2, b2)


def _moe_layer(xt, router_w, router_bias, w_gate, w_up, w_down, sw_gate, sw_up, sw_down, g2, b2, alpha, cfg):
    t, d = xt.shape
    ne = router_w.shape[1]
    br = cfg["moe_rows"]
    tt = cfg["moe_tile"]
    n_tiles = t // tt
    lp, lpt, gate, cnt, base, tot = _router(xt, router_w, router_bias, tt)
    cnt = cnt.reshape(n_tiles, ne).astype(I32)
    tot = tot.reshape(ne).astype(I32)
    padded = (tot + br - 1) // br * br
    pends = jnp.cumsum(padded)
    pstarts = pends - padded
    seg_start = pstarts[None, :] + base.reshape(n_tiles, ne).astype(I32)
    tile_rows = jnp.sum(cnt, axis=1, keepdims=True)
    table = jnp.concatenate([cnt, seg_start, tile_rows, jnp.zeros((n_tiles, SEG_ALIGN - 1), I32)], axis=1).reshape(-1)
    n_blocks = -(-(t * TOP_K + ne * n_tiles * SEG_ALIGN) // br) + ne
    blk_start = jnp.arange(n_blocks, dtype=I32) * br
    blk_exp = jnp.minimum(jnp.sum(pends[None, :] <= blk_start[:, None], axis=1), ne - 1).astype(I32)
    n_used = (pends[-1:] // br).astype(I32)
    xs = _dispatch(xt, lpt, table, pstarts + tot, padded - tot, n_blocks * br, tt, br)
    y = _experts(xs, blk_exp, n_used, w_gate, w_up, w_down, br)
    return _combine(table, y, lp, gate, xt, sw_gate.astype(BF16), sw_up.astype(BF16), sw_down.astype(BF16),
                    g2, b2, alpha, tt, ne)


def _config(b, s, d):
    return dict(row_tm=min(512, s), conv_tm=min(512, s), conv_cw=min(256, d), proj_rows=min(512, s), attn_out_tm=256,
                moe_rows=512, moe_tile=min(256, b * s))


def kernel(x, positions, ln1_g, ln1_b, ln2_g, ln2_b, conv_w_in, conv_b_in, conv_w_dw, conv_b_dw, conv_ln_g, conv_ln_b, conv_w_out, conv_b_out, w_kv, attn_w_q, attn_w_o, router_w, router_bias, exp_w_gate, exp_w_up, exp_w_down, sh_w_gate, sh_w_up, sh_w_down):
    b, s, d = x.shape
    t = b * s
    depth = ln1_g.shape[0]
    n_conv = conv_w_in.shape[0]
    alpha = (2.0 * depth) ** 0.25
    cfg = _config(b, s, d)
    n_groups = len(WINDOW_DILATIONS)
    gw = attn_w_o.shape[1]
    vec = lambda a: a.reshape(1, -1)

    xt = x.reshape(t, d)
    kv = None
    tables = None
    for layer in range(depth):
        g1, b1 = vec(ln1_g[layer]), vec(ln1_b[layer])
        if layer < n_conv:
            h = _glu(xt, conv_w_in[layer].astype(BF16), vec(conv_b_in[layer]), cfg["row_tm"])
            c = _dwconv(h, conv_w_dw[layer], vec(conv_b_dw[layer]), b, s, cfg["conv_tm"], cfg["conv_cw"])
            xt = _conv_out(c, xt, vec(conv_ln_g[layer]), vec(conv_ln_b[layer]), conv_w_out[layer].astype(BF16),
                           vec(conv_b_out[layer]), g1, b1, alpha, cfg["row_tm"])
        else:
            j = layer - n_conv
            if tables is None:
                cos, sin = _rope_tables(positions, cfg["row_tm"])
                tables = (cos.reshape(b, s, LANES), sin.reshape(b, s, LANES))
            x3 = xt.reshape(b, s, d)
            wq = attn_w_q[j].astype(BF16)
            outs, lses = [], []
            new_kv = []
            for g, (win, dil) in enumerate(WINDOW_DILATIONS):
                assert win // dil == ATTN_BLOCK
                if kv is None:
                    wkv = w_kv.astype(BF16)
                    q, kg, vg = _project(x3, *tables, [(wq, g), (wkv, g), (wkv, n_groups + g)], (True, True, False),
                                         dil, cfg["proj_rows"], gw)
                    new_kv.append((kg, vg))
                else:
                    (q,) = _project(x3, *tables, [(wq, g)], (True,), dil, cfg["proj_rows"], gw)
                    kg, vg = kv[g]
                o, lse = _attention(q, kg, vg, dil, min(512, s // dil))
                outs.append(o)
                lses.append(lse)
            if kv is None:
                kv = new_kv
            xt = _attn_out(outs, lses, x3, attn_w_o[j].astype(BF16), g1, b1, alpha, cfg["attn_out_tm"]).reshape(t, d)
        xt = _moe_layer(xt, router_w[layer], router_bias[layer], exp_w_gate[layer], exp_w_up[layer],
                        exp_w_down[layer], sh_w_gate[layer], sh_w_up[layer], sh_w_down[layer],
                        vec(ln2_g[layer]), vec(ln2_b[layer]), alpha, cfg)
    return xt.reshape(b, s, d)
```

```python
import functools
import math

import jax
import jax.numpy as jnp
from jax import lax
from jax.experimental import pallas as pl
from jax.experimental.pallas import tpu as pltpu

F32 = jnp.float32
BF16 = jnp.bfloat16
I32 = jnp.int32
U32 = jnp.uint32

LANES = 128
HEAD_DIM = 128
ROT_DIM = HEAD_DIM // 4
ROPE_THETA = 500000.0
ATTN_BLOCK = 128
WINDOW_DILATIONS = ((128, 1), (512, 4), (2048, 16))
NEG_INF = -1e30
TOP_K = 8
ROUTED_SCALE = 2.5
LN_EPS = 1e-5
CONV_HALO = 32
CONV_ROWS = 64
VMEM_LIMIT = 56 * 1024 * 1024
SEG_ALIGN = 8
SORT_CHUNK = 256


def _cparams(sem):
    return pltpu.CompilerParams(dimension_semantics=sem, vmem_limit_bytes=VMEM_LIMIT)


def _ln(y, g, b):
    mu = jnp.mean(y, axis=-1, keepdims=True)
    d = y - mu
    var = jnp.mean(d * d, axis=-1, keepdims=True)
    return d * lax.rsqrt(var + LN_EPS) * g + b


def _full(shape):
    n = len(shape)
    return pl.BlockSpec(shape, lambda *_: (0,) * n)


def _glu_kernel(x_ref, w_ref, b_ref, o_ref):
    d = o_ref.shape[-1]
    h = jnp.dot(x_ref[...].astype(BF16), w_ref[...], preferred_element_type=F32) + b_ref[...]
    o_ref[...] = h[:, :d] * jax.nn.sigmoid(h[:, d:])


def _glu(xt, w_in, b_in, tm):
    t, d = xt.shape
    return pl.pallas_call(
        _glu_kernel,
        grid=(t // tm,),
        in_specs=[pl.BlockSpec((tm, d), lambda i: (i, 0)), _full((d, 2 * d)), _full((1, 2 * d))],
        out_specs=pl.BlockSpec((tm, d), lambda i: (i, 0)),
        out_shape=jax.ShapeDtypeStruct((t, d), F32),
        compiler_params=_cparams(("arbitrary",)),
        name="conv_glu",
    )(xt, w_in, b_in)


def _dwconv_kernel(h_ref, w_ref, b_ref, o_ref, buf, win, *, rows):
    s = pl.program_id(2)
    tm = h_ref.shape[0]
    width = w_ref.shape[0]
    off = CONV_HALO - (width - 1)

    @pl.when(s == 0)
    def _():
        buf[0:CONV_HALO, :] = jnp.zeros((CONV_HALO, buf.shape[1]), F32)

    @pl.when(s > 0)
    def _():
        buf[0:CONV_HALO, :] = buf[tm:tm + CONV_HALO, :]

    buf[CONV_HALO:CONV_HALO + tm, :] = h_ref[...]
    sub = 8
    for c0 in range(0, buf.shape[1], LANES):
        cs = slice(c0, c0 + LANES)
        for r0 in range(0, tm, rows):
            acc = None
            for phase in range(min(sub, width)):
                n_taps = (width - 1 - phase) // sub + 1
                start = r0 + off + phase
                span = rows + sub * (n_taps - 1)
                win[0:span, :] = buf[start:start + span, cs]
                for a in range(n_taps):
                    k = sub * a + phase
                    term = win[sub * a:sub * a + rows, :] * w_ref[k:k + 1, cs]
                    acc = term if acc is None else acc + term
            o_ref[r0:r0 + rows, cs] = acc + b_ref[:, cs]


def _dwconv(h, w_dw, b_dw, b, s, tm, cw):
    t, d = h.shape
    h3 = h.reshape(b, s, d)
    width = w_dw.shape[0]
    out = pl.pallas_call(
        functools.partial(_dwconv_kernel, rows=CONV_ROWS),
        grid=(b, d // cw, s // tm),
        in_specs=[pl.BlockSpec((None, tm, cw), lambda bi, c, si: (bi, si, c)),
                  pl.BlockSpec((width, cw), lambda bi, c, si: (0, c)),
                  pl.BlockSpec((1, cw), lambda bi, c, si: (0, c))],
        out_specs=pl.BlockSpec((None, tm, cw), lambda bi, c, si: (bi, si, c)),
        out_shape=jax.ShapeDtypeStruct((b, s, d), F32),
        scratch_shapes=[pltpu.VMEM((CONV_HALO + tm, cw), F32), pltpu.VMEM((CONV_ROWS + CONV_HALO, LANES), F32)],
        compiler_params=_cparams(("arbitrary", "arbitrary", "arbitrary")),
        name="conv_depthwise",
    )(h3, w_dw, b_dw)
    return out.reshape(t, d)


def _conv_out_kernel(c_ref, x_ref, cg_ref, cb_ref, w_ref, bo_ref, g_ref, b_ref, o_ref, *, alpha):
    u = _ln(c_ref[...], cg_ref[...], cb_ref[...])
    u = u * jax.nn.sigmoid(u)
    mix = jnp.dot(u.astype(BF16), w_ref[...], preferred_element_type=F32) + bo_ref[...]
    o_ref[...] = _ln(alpha * x_ref[...] + mix, g_ref[...], b_ref[...])


def _conv_out(c, xt, cg, cb, w_out, b_out, g1, b1, alpha, tm):
    t, d = xt.shape
    row = pl.BlockSpec((tm, d), lambda i: (i, 0))
    vec = _full((1, d))
    return pl.pallas_call(
        functools.partial(_conv_out_kernel, alpha=alpha),
        grid=(t // tm,),
        in_specs=[row, row, vec, vec, _full((d, d)), vec, vec, vec],
        out_specs=row,
        out_shape=jax.ShapeDtypeStruct((t, d), F32),
        compiler_params=_cparams(("arbitrary",)),
        name="conv_out_ln",
    )(c, xt, cg, cb, w_out, b_out, g1, b1)


def _rope_kernel(pos_ref, invf_ref, c_ref, s_ref):
    ang = pos_ref[...].astype(F32) * invf_ref[...]
    lane = lax.broadcasted_iota(I32, ang.shape, 1)
    half = ROT_DIM // 2
    c_ref[...] = jnp.where(lane < ROT_DIM, jnp.cos(ang), 1.0)
    sn = jnp.sin(ang)
    s_ref[...] = jnp.where(lane < half, -sn, jnp.where(lane < ROT_DIM, sn, 0.0))


def _rope_tables(positions, tm):
    t = positions.size
    half = ROT_DIM // 2
    inv_freq = ROPE_THETA ** (-jnp.arange(half, dtype=F32) * 2.0 / ROT_DIM)
    invf = jnp.zeros((1, LANES), F32).at[0, :half].set(inv_freq).at[0, half:ROT_DIM].set(inv_freq)
    out = jax.ShapeDtypeStruct((t, LANES), F32)
    return pl.pallas_call(
        _rope_kernel,
        grid=(t // tm,),
        in_specs=[pl.BlockSpec((tm, 1), lambda i: (i, 0)), _full((1, LANES))],
        out_specs=[pl.BlockSpec((tm, LANES), lambda i: (i, 0))] * 2,
        out_shape=[out, out],
        compiler_params=_cparams(("arbitrary",)),
        name="rope_tables",
    )(positions.reshape(t, 1), invf)


def _proj_kernel(x_ref, c_ref, s_ref, *rest, rot, heads, dil):
    n_out = len(rot)
    w_refs, o_refs = rest[:n_out], rest[n_out:2 * n_out]
    lhs, tabs, xc = rest[2 * n_out:]
    tm = o_refs[0].shape[1]
    n_lane_tiles = xc.shape[0]
    if dil > 1:
        for c in range(n_lane_tiles):
            xc[c] = x_ref[:, c * LANES:(c + 1) * LANES]
    for r in range(dil):
        rows = slice(r * tm, (r + 1) * tm)
        if dil > 1:
            src = pl.ds(r, tm, stride=dil)
            for c in range(n_lane_tiles):
                lhs[rows, c * LANES:(c + 1) * LANES] = xc[c, src, :].astype(BF16)
        else:
            src = slice(None)
            lhs[rows, :] = x_ref[...].astype(BF16)
        tabs[0, rows, :] = c_ref[src, :]
        tabs[1, rows, :] = s_ref[src, :]
    cos = tabs[0]
    sin = tabs[1]
    lane = lax.broadcasted_iota(I32, cos.shape, 1)
    first = lane < ROT_DIM // 2
    for j, o_ref in enumerate(o_refs):
        y = jnp.dot(lhs[...], w_refs[j][...], preferred_element_type=F32)
        for h in range(heads):
            t = y[:, h * HEAD_DIM:(h + 1) * HEAD_DIM]
            if rot[j]:
                partner = jnp.where(first, pltpu.roll(t, HEAD_DIM - ROT_DIM // 2, 1), pltpu.roll(t, ROT_DIM // 2, 1))
                t = t * cos + partner * sin
            t = t.astype(BF16)
            for r in range(dil):
                o_ref[r, :, h * HEAD_DIM:(h + 1) * HEAD_DIM] = t[r * tm:(r + 1) * tm]


def _project(x3, cos3, sin3, weights, rot, dil, rows, gw):
    b, s, d = x3.shape
    l = s // dil
    tm = rows // dil
    n_out = len(rot)
    out = jax.ShapeDtypeStruct((b, dil, l, gw), BF16)
    tab = pl.BlockSpec((None, rows, LANES), lambda bi, i: (bi, i, 0))
    w_specs = [pl.BlockSpec((d, gw), lambda bi, i, blk=blk: (0, blk)) for _, blk in weights]
    return pl.pallas_call(
        functools.partial(_proj_kernel, rot=rot, heads=gw // HEAD_DIM, dil=dil),
        grid=(b, s // rows),
        in_specs=[pl.BlockSpec((None, rows, d), lambda bi, i: (bi, i, 0)), tab, tab] + w_specs,
        out_specs=[pl.BlockSpec((None, dil, tm, gw), lambda bi, i: (bi, 0, i, 0))] * n_out,
        out_shape=[out] * n_out,
        scratch_shapes=[pltpu.VMEM((rows, d), BF16), pltpu.VMEM((2, rows, LANES), F32),
                        pltpu.VMEM((d // LANES, rows, LANES), F32)],
        compiler_params=_cparams(("arbitrary", "arbitrary")),
        name=f"proj_dil{dil}_n{n_out}",
    )(x3, cos3, sin3, *[w for w, _ in weights])


def _attn_kernel(q_ref, kc_ref, kh_ref, vc_ref, vh_ref, o_ref, lse_ref, kbuf, vbuf, *, heads):
    i = pl.program_id(2)
    tq = q_ref.shape[0]
    blk = ATTN_BLOCK
    kbuf[0:blk, :] = kh_ref[...]
    kbuf[blk:blk + tq, :] = kc_ref[...]
    vbuf[0:blk, :] = vh_ref[...]
    vbuf[blk:blk + tq, :] = vc_ref[...]
    scale = 1.0 / math.sqrt(HEAD_DIM)
    row = lax.broadcasted_iota(I32, (blk, 2 * blk), 0)
    col = lax.broadcasted_iota(I32, (blk, 2 * blk), 1)
    band = jnp.logical_and(col >= row, col <= row + blk)
    first_band = jnp.logical_and(band, jnp.logical_or(col >= blk, i > 0))
    lane = lax.broadcasted_iota(I32, (blk, LANES), 1)
    dims = (((1,), (1,)), ((), ()))
    for n in range(tq // blk):
        rs = slice(n * blk, (n + 1) * blk)
        ks = slice(n * blk, (n + 2) * blk)
        mask = first_band if n == 0 else band
        lse_tile = jnp.zeros((blk, LANES), F32)
        for h in range(heads):
            cs = slice(h * HEAD_DIM, (h + 1) * HEAD_DIM)
            s = lax.dot_general(q_ref[rs, cs], kbuf[ks, cs], dims, preferred_element_type=F32) * scale
            s = jnp.where(mask, s, NEG_INF)
            m = jnp.max(s, axis=1, keepdims=True)
            p = jnp.exp(s - m)
            den = jnp.sum(p, axis=1, keepdims=True)
            o = jnp.dot(p.astype(BF16), vbuf[ks, cs], preferred_element_type=F32)
            o_ref[rs, cs] = o / den
            lse_tile = jnp.where(lane == h, m + jnp.log(den), lse_tile)
        lse_ref[rs, :] = lse_tile


def _attention(q, k, v, dil, tq):
    b, _, l, gw = q.shape
    nb = tq // ATTN_BLOCK
    cur = pl.BlockSpec((None, None, tq, gw), lambda bi, r, i: (bi, r, i, 0))
    halo = pl.BlockSpec((None, None, ATTN_BLOCK, gw), lambda bi, r, i: (bi, r, jnp.maximum(i * nb - 1, 0), 0))
    o, lse = pl.pallas_call(
        functools.partial(_attn_kernel, heads=gw // HEAD_DIM),
        grid=(b, dil, l // tq),
        in_specs=[cur, cur, halo, cur, halo],
        out_specs=[pl.BlockSpec((None, None, tq, gw), lambda bi, r, i: (bi, r, i, 0)),
                   pl.BlockSpec((None, None, tq, LANES), lambda bi, r, i: (bi, r, i, 0))],
        out_shape=[jax.ShapeDtypeStruct((b, dil, l, gw), F32),
                   jax.ShapeDtypeStruct((b, dil, l, LANES), F32)],
        scratch_shapes=[pltpu.VMEM((ATTN_BLOCK + tq, gw), BF16), pltpu.VMEM((ATTN_BLOCK + tq, gw), BF16)],
        compiler_params=_cparams(("arbitrary", "arbitrary", "arbitrary")),
        name=f"attn_dil{dil}",
    )(q, k, k, v, v)
    return o, lse


def _attn_out_kernel(o0_ref, o1_ref, o2_ref, l0_ref, l1_ref, l2_ref, x_ref, w_ref, g_ref, b_ref,
                     out_ref, onat, lnat, mixed, *, alpha, heads, dils):
    tm = x_ref.shape[0]
    for gi, (o_ref, l_ref) in enumerate(((o0_ref, l0_ref), (o1_ref, l1_ref), (o2_ref, l2_ref))):
        dil = dils[gi]
        for r in range(dil):
            dst = pl.ds(r, tm // dil, stride=dil) if dil > 1 else slice(None)
            for h in range(heads):
                onat[gi * heads + h, dst, :] = o_ref[r, :, h * HEAD_DIM:(h + 1) * HEAD_DIM]
            lnat[gi, dst, :] = l_ref[r]
    l0, l1, l2 = lnat[0], lnat[1], lnat[2]
    m = jnp.maximum(jnp.maximum(l0, l1), l2)
    e0, e1, e2 = jnp.exp(l0 - m), jnp.exp(l1 - m), jnp.exp(l2 - m)
    den = e0 + e1 + e2
    w0, w1, w2 = e0 / den, e1 / den, e2 / den
    for h in range(heads):
        cs = slice(h * HEAD_DIM, (h + 1) * HEAD_DIM)
        o = w0[:, h:h + 1] * onat[h] + w1[:, h:h + 1] * onat[heads + h] + w2[:, h:h + 1] * onat[2 * heads + h]
        mixed[:, cs] = o.astype(BF16)
    mix = jnp.dot(mixed[...], w_ref[...], preferred_element_type=F32)
    out_ref[...] = _ln(alpha * x_ref[...] + mix, g_ref[...], b_ref[...])


def _attn_out(outs, lses, x3, w_o, g1, b1, alpha, tm):
    b, s, d = x3.shape
    gw = w_o.shape[0]
    dils = tuple(o.shape[1] for o in outs)
    ospecs = [pl.BlockSpec((None, dil, tm // dil, gw), lambda bi, i: (bi, 0, i, 0)) for dil in dils]
    lspecs = [pl.BlockSpec((None, dil, tm // dil, LANES), lambda bi, i: (bi, 0, i, 0)) for dil in dils]
    xrow = pl.BlockSpec((None, tm, d), lambda bi, i: (bi, i, 0))
    n = len(dils)
    return pl.pallas_call(
        functools.partial(_attn_out_kernel, alpha=alpha, heads=gw // HEAD_DIM, dils=dils),
        grid=(b, s // tm),
        in_specs=ospecs + lspecs + [xrow, _full((gw, d)), _full((1, d)), _full((1, d))],
        out_specs=xrow,
        out_shape=jax.ShapeDtypeStruct((b, s, d), F32),
        scratch_shapes=[pltpu.VMEM((n * gw // HEAD_DIM, tm, HEAD_DIM), F32), pltpu.VMEM((n, tm, LANES), F32),
                        pltpu.VMEM((tm, gw), BF16)],
        compiler_params=_cparams(("arbitrary", "arbitrary")),
        name="attn_out_ln",
    )(*outs, *lses, x3, w_o, g1, b1)


def _pack_bf16_pairs(v, holds_bf16=False):
    half = v.shape[1] // 2
    first, second = v[:, :half], v[:, half:]
    if not holds_bf16:
        first, second = first.astype(BF16).astype(F32), second.astype(BF16).astype(F32)
    return lax.bitcast_convert_type(first, U32) | (lax.bitcast_convert_type(second, U32) >> 16)


def _unpack_bf16_pairs(p):
    first = lax.bitcast_convert_type(p & jnp.uint32(0xFFFF0000), F32)
    second = lax.bitcast_convert_type(p << 16, F32)
    return first, second


def _router_kernel(x_ref, rwt_ref, rb_ref, lp_ref, lpt_ref, gate_ref, cnt_ref, base_ref, tot_ref, carry):
    @pl.when(pl.program_id(0) == 0)
    def _():
        carry[...] = jnp.zeros(carry.shape, F32)

    tm = x_ref.shape[0]
    ne = rwt_ref.shape[0]
    logits = lax.dot_general(rwt_ref[...], x_ref[...], (((1,), (1,)), ((), ())),
                             precision=lax.Precision.HIGHEST, preferred_element_type=F32)
    scores = jax.nn.sigmoid(logits)
    sel = scores + rb_ref[...]
    eidx = lax.broadcasted_iota(I32, (ne, tm), 0)
    chosen = jnp.zeros((ne, tm), F32)
    gsum = jnp.zeros((1, tm), F32)
    picks = []
    for _ in range(TOP_K):
        best = jnp.max(sel, axis=0, keepdims=True)
        j = jnp.min(jnp.where(sel == best, eidx, ne), axis=0, keepdims=True)
        onehot = eidx == j
        sc = jnp.sum(jnp.where(onehot, scores, 0.0), axis=0, keepdims=True)
        sel = jnp.where(onehot, -jnp.inf, sel)
        chosen = chosen + onehot.astype(F32)
        gsum = gsum + sc
        picks.append((onehot, sc))
    r = lax.broadcasted_iota(I32, (tm, tm), 0)
    c = lax.broadcasted_iota(I32, (tm, tm), 1)
    rank = jnp.dot(chosen.astype(BF16), (r < c).astype(BF16), preferred_element_type=F32)
    cnt = jnp.sum(chosen, axis=1, keepdims=True)
    cnt_al = jnp.floor((cnt + (SEG_ALIGN - 1)) / SEG_ALIGN) * SEG_ALIGN
    er = lax.broadcasted_iota(I32, (ne, ne), 0)
    ec = lax.broadcasted_iota(I32, (ne, ne), 1)
    seg_off = jnp.dot((ec < er).astype(BF16), jnp.broadcast_to(cnt_al, (ne, LANES)).astype(BF16),
                      preferred_element_type=F32)[:, 0:1]
    slot = rank + seg_off
    krow = lax.broadcasted_iota(I32, (TOP_K, tm), 0)
    lpt = jnp.zeros((TOP_K, tm), I32)
    gt = jnp.zeros((TOP_K, tm), F32)
    for k, (onehot, sc) in enumerate(picks):
        row = jnp.sum(jnp.where(onehot, slot, 0.0), axis=0, keepdims=True)
        lpt = jnp.where(krow == k, row.astype(I32), lpt)
        gt = jnp.where(krow == k, sc / gsum * ROUTED_SCALE, gt)
    lpt_ref[...] = lpt
    lp_ref[...] = jnp.concatenate([lpt, jnp.zeros((LANES - TOP_K, tm), I32)], axis=0).T
    gate_ref[...] = jnp.concatenate([gt, jnp.zeros((LANES - TOP_K, tm), F32)], axis=0).T
    cnt_ref[...] = cnt_al
    base_ref[...] = carry[...]
    carry[...] = carry[...] + cnt_al
    tot_ref[...] = carry[...]


def _router(xt, router_w, router_bias, tm):
    t, d = xt.shape
    ne = router_w.shape[1]
    n_tiles = t // tm
    wide = pl.BlockSpec((tm, LANES), lambda i: (i, 0))
    per_tile = pl.BlockSpec((None, ne, 1), lambda i: (i, 0, 0))
    return pl.pallas_call(
        _router_kernel,
        grid=(n_tiles,),
        in_specs=[pl.BlockSpec((tm, d), lambda i: (i, 0)), _full((ne, d)), _full((ne, 1))],
        out_specs=[wide, pl.BlockSpec((None, TOP_K, tm), lambda i: (i, 0, 0)), wide, per_tile, per_tile, _full((ne, 1))],
        out_shape=[jax.ShapeDtypeStruct((t, LANES), I32), jax.ShapeDtypeStruct((n_tiles, TOP_K, tm), I32),
                   jax.ShapeDtypeStruct((t, LANES), F32), jax.ShapeDtypeStruct((n_tiles, ne, 1), F32),
                   jax.ShapeDtypeStruct((n_tiles, ne, 1), F32), jax.ShapeDtypeStruct((ne, 1), F32)],
        scratch_shapes=[pltpu.VMEM((ne, 1), F32)],
        compiler_params=_cparams(("arbitrary",)),
        name="moe_router",
    )(xt, router_w.T, router_bias.reshape(ne, 1))


def _segment_loop(tbl, base, ne, make_copy, action):
    def body(e, off):
        n = pl.multiple_of(tbl[base + e], SEG_ALIGN)
        g = pl.multiple_of(tbl[base + ne + e], SEG_ALIGN)
        off = pl.multiple_of(off, SEG_ALIGN)

        @pl.when(n > 0)
        def _():
            action(make_copy(off, g, n))

        return off + n

    lax.fori_loop(0, ne, body, 0)


def _start(cp):
    cp.start()


def _wait(cp):
    cp.wait()


def _dispatch_kernel(fs_ref, fn_ref, tbl, lpt_ref, x_ref, xs_hbm, sorted_buf, zbuf, sem, *, ne):
    i = pl.program_id(0)
    last = pl.num_programs(0) - 1
    slot = lax.rem(i, 2)
    tt = x_ref.shape[0]
    width = _table_width(ne)

    @pl.when(i == 0)
    def _():
        zbuf[...] = jnp.zeros(zbuf.shape, zbuf.dtype)

        def visit(action):
            def body(e, carry):
                n = pl.multiple_of(fn_ref[e], SEG_ALIGN)
                s = pl.multiple_of(fs_ref[e], SEG_ALIGN)

                @pl.when(n > 0)
                def _():
                    action(pltpu.make_async_copy(zbuf.at[pl.ds(0, n), :], xs_hbm.at[pl.ds(s, n), :], sem.at[0]))

                return carry
            return body

        lax.fori_loop(0, ne, visit(_start), 0)
        lax.fori_loop(0, ne, visit(_wait), 0)

    xb = x_ref[...].astype(BF16)
    lpt = lpt_ref[...]
    n_chunks = lax.shift_right_logical(tbl[i * width + 2 * ne] + (SORT_CHUNK - 1), SORT_CHUNK.bit_length() - 1)

    def chunk(ci, carry):
        j0 = pl.multiple_of(ci * SORT_CHUNK, SORT_CHUNK)
        rows = j0 + lax.broadcasted_iota(I32, (SORT_CHUNK, tt), 0)
        hit = lpt[0:1, :] == rows
        for k in range(1, TOP_K):
            hit = jnp.logical_or(hit, lpt[k:k + 1, :] == rows)
        srt = jnp.dot(jnp.where(hit, 1.0, 0.0).astype(BF16), xb, preferred_element_type=F32)
        sorted_buf[slot, pl.ds(j0, SORT_CHUNK), :] = _pack_bf16_pairs(srt, holds_bf16=True)
        return carry

    lax.fori_loop(0, n_chunks, chunk, 0)

    def writes(sl):
        return lambda off, g, n: pltpu.make_async_copy(
            sorted_buf.at[sl, pl.ds(off, n), :], xs_hbm.at[pl.ds(g, n), :], sem.at[sl])

    _segment_loop(tbl, i * width, ne, writes(slot), _start)

    @pl.when(i > 0)
    def _():
        _segment_loop(tbl, (i - 1) * width, ne, writes(1 - slot), _wait)

    @pl.when(i == last)
    def _():
        _segment_loop(tbl, i * width, ne, writes(slot), _wait)


def _sorted_rows(tt, ne):
    return -(-(tt * TOP_K + ne * SEG_ALIGN) // SORT_CHUNK) * SORT_CHUNK


def _table_width(ne):
    return 2 * ne + SEG_ALIGN


def _dispatch(xt, lpt, table, fill_start, fill_n, n_rows, tt, br):
    t, d = xt.shape
    ne = fill_start.shape[0]
    grid_spec = pltpu.PrefetchScalarGridSpec(
        num_scalar_prefetch=3,
        grid=(t // tt,),
        in_specs=[pl.BlockSpec((None, TOP_K, tt), lambda i, *_: (i, 0, 0)),
                  pl.BlockSpec((tt, d), lambda i, *_: (i, 0))],
        out_specs=pl.BlockSpec(memory_space=pl.ANY),
        scratch_shapes=[pltpu.VMEM((2, _sorted_rows(tt, ne), d // 2), U32),
                        pltpu.VMEM((br, d // 2), U32), pltpu.SemaphoreType.DMA((2,))],
    )
    return pl.pallas_call(
        functools.partial(_dispatch_kernel, ne=ne),
        grid_spec=grid_spec,
        out_shape=jax.ShapeDtypeStruct((n_rows, d // 2), U32),
        compiler_params=_cparams(("arbitrary",)),
        name="moe_dispatch",
    )(fill_start, fill_n, table, lpt, xt)


def _expert_kernel(be_ref, nu_ref, xs_ref, wg_ref, wu_ref, wd_ref, y_ref, wg_b, wu_b, wd_b):
    i = pl.program_id(0)
    live = i < nu_ref[0]

    @pl.when(jnp.logical_and(live, jnp.logical_or(i == 0, be_ref[i] != be_ref[jnp.maximum(i - 1, 0)])))
    def _():
        wg_b[...] = wg_ref[...].astype(BF16)
        wu_b[...] = wu_ref[...].astype(BF16)
        wd_b[...] = wd_ref[...].astype(BF16)

    @pl.when(live)
    def _():
        first, second = _unpack_bf16_pairs(xs_ref[...])
        xb = jnp.concatenate([first.astype(BF16), second.astype(BF16)], axis=1)
        g = jnp.dot(xb, wg_b[...], preferred_element_type=F32)
        u = jnp.dot(xb, wu_b[...], preferred_element_type=F32)
        h = g * jax.nn.sigmoid(g) * u
        y = jnp.dot(h.astype(BF16), wd_b[...], preferred_element_type=F32)
        y_ref[...] = _pack_bf16_pairs(y)


def _experts(xs, blk_exp, n_used, w_gate, w_up, w_down, layer, br):
    p, dh = xs.shape
    d, f = w_gate.shape[2:]
    n_blocks = p // br

    def row_map(i, be, nu):
        return (jnp.minimum(i, nu[0] - 1), 0)

    def w_map(i, be, nu):
        return (layer, be[jnp.minimum(i, nu[0] - 1)], 0, 0)

    grid_spec = pltpu.PrefetchScalarGridSpec(
        num_scalar_prefetch=2,
        grid=(n_blocks,),
        in_specs=[pl.BlockSpec((br, dh), row_map),
                  pl.BlockSpec((None, None, d, f), w_map),
                  pl.BlockSpec((None, None, d, f), w_map),
                  pl.BlockSpec((None, None, f, d), w_map)],
        out_specs=pl.BlockSpec((br, dh), row_map),
        scratch_shapes=[pltpu.VMEM((d, f), BF16), pltpu.VMEM((d, f), BF16), pltpu.VMEM((f, d), BF16)],
    )
    return pl.pallas_call(
        _expert_kernel,
        grid_spec=grid_spec,
        out_shape=jax.ShapeDtypeStruct((p, dh), U32),
        compiler_params=_cparams(("arbitrary",)),
        name="moe_experts",
    )(blk_exp, n_used, xs, w_gate, w_up, w_down)


def _combine_kernel(tbl, y_hbm, lp_ref, gate_ref, x_ref, sg_ref, su_ref, sd_ref, g_ref, b_ref, o_ref,
                    ysorted, acc, lpb, gb, sem, *, ne, alpha):
    i = pl.program_id(0)
    last = pl.num_programs(0) - 1
    slot = lax.rem(i, 2)
    tt = x_ref.shape[0]
    half = x_ref.shape[1] // 2
    width = _table_width(ne)

    def gathers(sl):
        return lambda off, g, n: pltpu.make_async_copy(
            y_hbm.at[pl.ds(g, n), :], ysorted.at[sl, pl.ds(off, n), :], sem.at[sl])

    @pl.when(i == 0)
    def _():
        _segment_loop(tbl, 0, ne, gathers(0), _start)

    @pl.when(i < last)
    def _():
        _segment_loop(tbl, (i + 1) * width, ne, gathers(1 - slot), _start)

    x = x_ref[...]
    xb = x.astype(BF16)
    g = jnp.dot(xb, sg_ref[...], preferred_element_type=F32)
    u = jnp.dot(xb, su_ref[...], preferred_element_type=F32)
    h = g * jax.nn.sigmoid(g) * u
    acc[...] = jnp.dot(h.astype(BF16), sd_ref[...], preferred_element_type=F32)
    lp = lp_ref[...]
    gate = gate_ref[...]
    for k in range(TOP_K):
        lpb[k] = jnp.broadcast_to(lp[:, k:k + 1], (tt, LANES))
        gb[k] = jnp.broadcast_to(gate[:, k:k + 1], (tt, LANES))
    _segment_loop(tbl, i * width, ne, gathers(slot), _wait)
    total = tbl[i * width + 2 * ne]
    n_chunks = lax.shift_right_logical(total + (SORT_CHUNK - 1), SORT_CHUNK.bit_length() - 1)
    lane = lax.broadcasted_iota(I32, (tt, LANES), 1)

    def chunk(ci, carry):
        j0 = pl.multiple_of(ci * SORT_CHUNK, SORT_CHUNK)
        parts = []
        for c0 in range(0, SORT_CHUNK, LANES):
            cols = lane + (j0 + c0)
            gm = jnp.zeros((tt, LANES), F32)
            for k in range(TOP_K):
                gm = jnp.where(lpb[k] == cols, gb[k], gm)
            parts.append(gm)
        gmat = jnp.concatenate(parts, axis=1)
        g_hi = gmat.astype(BF16)
        g_lo = (gmat - g_hi.astype(F32)).astype(BF16)
        rows = j0 + lax.broadcasted_iota(I32, (SORT_CHUNK, half), 0)
        ys = jnp.where(rows < total, ysorted[slot, pl.ds(j0, SORT_CHUNK), :], jnp.uint32(0))
        first, second = _unpack_bf16_pairs(ys)
        first = first.astype(BF16)
        second = second.astype(BF16)
        acc[:, :half] += (jnp.dot(g_hi, first, preferred_element_type=F32)
                          + jnp.dot(g_lo, first, preferred_element_type=F32))
        acc[:, half:] += (jnp.dot(g_hi, second, preferred_element_type=F32)
                          + jnp.dot(g_lo, second, preferred_element_type=F32))
        return carry

    lax.fori_loop(0, n_chunks, chunk, 0)
    o_ref[...] = _ln(alpha * x + acc[...], g_ref[...], b_ref[...])


def _combine(table, y, lp, gate, xt, sw_gate, sw_up, sw_down, g2, b2, alpha, tt, ne):
    t, d = xt.shape
    f = sw_gate.shape[1]
    row = pl.BlockSpec((tt, d), lambda i, *_: (i, 0))
    wide = pl.BlockSpec((tt, LANES), lambda i, *_: (i, 0))
    grid_spec = pltpu.PrefetchScalarGridSpec(
        num_scalar_prefetch=1,
        grid=(t // tt,),
        in_specs=[pl.BlockSpec(memory_space=pl.ANY), wide, wide, row,
                  _full((d, f)), _full((d, f)), _full((f, d)), _full((1, d)), _full((1, d))],
        out_specs=row,
        scratch_shapes=[pltpu.VMEM((2, _sorted_rows(tt, ne), d // 2), U32),
                        pltpu.VMEM((tt, d), F32), pltpu.VMEM((TOP_K, tt, LANES), I32), pltpu.VMEM((TOP_K, tt, LANES), F32),
                        pltpu.SemaphoreType.DMA((2,))],
    )
    return pl.pallas_call(
        functools.partial(_combine_kernel, ne=ne, alpha=alpha),
        grid_spec=grid_spec,
        out_shape=jax.ShapeDtypeStruct((t, d), F32),
        compiler_params=_cparams(("arbitrary",)),
        name="moe_combine_ln",
    )(table, y, lp, gate, xt, sw_gate, sw_up, sw_down, g2, b2)


def _moe_layer(xt, router_w, router_bias, w_gate, w_up, w_down, layer, sw_gate, sw_up, sw_down, g2, b2, alpha, cfg):
    t, d = xt.shape
    ne = router_w.shape[1]
    br = cfg["moe_rows"]
    tt = cfg["moe_tile"]
    n_tiles = t // tt
    lp, lpt, gate, cnt, base, tot = _router(xt, router_w, router_bias, tt)
    cnt = cnt.reshape(n_tiles, ne).astype(I32)
    tot = tot.reshape(ne).astype(I32)
    padded = (tot + br - 1) // br * br
    pends = jnp.cumsum(padded)
    pstarts = pends - padded
    seg_start = pstarts[None, :] + base.reshape(n_tiles, ne).astype(I32)
    tile_rows = jnp.sum(cnt, axis=1, keepdims=True)
    table = jnp.concatenate([cnt, seg_start, tile_rows, jnp.zeros((n_tiles, SEG_ALIGN - 1), I32)], axis=1).reshape(-1)
    n_blocks = -(-(t * TOP_K + ne * n_tiles * SEG_ALIGN) // br) + ne
    blk_start = jnp.arange(n_blocks, dtype=I32) * br
    blk_exp = jnp.minimum(jnp.sum(pends[None, :] <= blk_start[:, None], axis=1), ne - 1).astype(I32)
    n_used = (pends[-1:] // br).astype(I32)
    xs = _dispatch(xt, lpt, table, pstarts + tot, padded - tot, n_blocks * br, tt, br)
    y = _experts(xs, blk_exp, n_used, w_gate, w_up, w_down, layer, br)
    return _combine(table, y, lp, gate, xt, sw_gate.astype(BF16), sw_up.astype(BF16), sw_down.astype(BF16),
                    g2, b2, alpha, tt, ne)


def _config(b, s, d):
    return dict(row_tm=min(512, s), conv_tm=min(512, s), conv_cw=min(256, d), proj_rows=min(512, s), attn_out_tm=256,
                moe_rows=512, moe_tile=min(256, b * s))


def kernel(x, positions, ln1_g, ln1_b, ln2_g, ln2_b, conv_w_in, conv_b_in, conv_w_dw, conv_b_dw, conv_ln_g, conv_ln_b, conv_w_out, conv_b_out, w_kv, attn_w_q, attn_w_o, router_w, router_bias, exp_w_gate, exp_w_up, exp_w_down, sh_w_gate, sh_w_up, sh_w_down):
    b, s, d = x.shape
    t = b * s
    depth = ln1_g.shape[0]
    n_conv = conv_w_in.shape[0]
    alpha = (2.0 * depth) ** 0.25
    cfg = _config(b, s, d)
    n_groups = len(WINDOW_DILATIONS)
    gw = attn_w_o.shape[1]
    vec = lambda a: a.reshape(1, -1)

    xt = x.reshape(t, d)
    kv = None
    tables = None
    for layer in range(depth):
        g1, b1 = vec(ln1_g[layer]), vec(ln1_b[layer])
        if layer < n_conv:
            h = _glu(xt, conv_w_in[layer].astype(BF16), vec(conv_b_in[layer]), cfg["row_tm"])
            c = _dwconv(h, conv_w_dw[layer], vec(conv_b_dw[layer]), b, s, cfg["conv_tm"], cfg["conv_cw"])
            xt = _conv_out(c, xt, vec(conv_ln_g[layer]), vec(conv_ln_b[layer]), conv_w_out[layer].astype(BF16),
                           vec(conv_b_out[layer]), g1, b1, alpha, cfg["row_tm"])
        else:
            j = layer - n_conv
            if tables is None:
                cos, sin = _rope_tables(positions, cfg["row_tm"])
                tables = (cos.reshape(b, s, LANES), sin.reshape(b, s, LANES))
            x3 = xt.reshape(b, s, d)
            wq = attn_w_q[j].astype(BF16)
            outs, lses = [], []
            new_kv = []
            for g, (win, dil) in enumerate(WINDOW_DILATIONS):
                assert win // dil == ATTN_BLOCK
                if kv is None:
                    wkv = w_kv.astype(BF16)
                    q, kg, vg = _project(x3, *tables, [(wq, g), (wkv, g), (wkv, n_groups + g)], (True, True, False),
                                         dil, cfg["proj_rows"], gw)
                    new_kv.append((kg, vg))
                else:
                    (q,) = _project(x3, *tables, [(wq, g)], (True,), dil, cfg["proj_rows"], gw)
                    kg, vg = kv[g]
                o, lse = _attention(q, kg, vg, dil, min(512, s // dil))
                outs.append(o)
                lses.append(lse)
            if kv is None:
                kv = new_kv
            xt = _attn_out(outs, lses, x3, attn_w_o[j].astype(BF16), g1, b1, alpha, cfg["attn_out_tm"]).reshape(t, d)
        xt = _moe_layer(xt, router_w[layer], router_bias[layer], exp_w_gate, exp_w_up, exp_w_down, layer,
                        sh_w_gate[layer], sh_w_up[layer], sh_w_down[layer],
                        vec(ln2_g[layer]), vec(ln2_b[layer]), alpha, cfg)
    return xt.reshape(b, s, d)
```

```python
import functools
import math

import jax
import jax.numpy as jnp
from jax import lax
from jax.experimental import pallas as pl
from jax.experimental.pallas import tpu as pltpu

F32 = jnp.float32
BF16 = jnp.bfloat16
I32 = jnp.int32
U32 = jnp.uint32

LANES = 128
HEAD_DIM = 128
ROT_DIM = HEAD_DIM // 4
ROPE_THETA = 500000.0
ATTN_BLOCK = 128
WINDOW_DILATIONS = ((128, 1), (512, 4), (2048, 16))
NEG_INF = -1e30
TOP_K = 8
ROUTED_SCALE = 2.5
LN_EPS = 1e-5
CONV_HALO = 32
CONV_ROWS = 64
VMEM_LIMIT = 56 * 1024 * 1024
SEG_ALIGN = 8
SORT_CHUNK = 256


def _cparams(sem):
    return pltpu.CompilerParams(dimension_semantics=sem, vmem_limit_bytes=VMEM_LIMIT)


def _ln(y, g, b):
    mu = jnp.mean(y, axis=-1, keepdims=True)
    d = y - mu
    var = jnp.mean(d * d, axis=-1, keepdims=True)
    return d * lax.rsqrt(var + LN_EPS) * g + b


def _full(shape):
    n = len(shape)
    return pl.BlockSpec(shape, lambda *_: (0,) * n)


def _glu_kernel(x_ref, w_ref, b_ref, o_ref):
    d = o_ref.shape[-1]
    h = jnp.dot(x_ref[...].astype(BF16), w_ref[...], preferred_element_type=F32) + b_ref[...]
    o_ref[...] = h[:, :d] * jax.nn.sigmoid(h[:, d:])


def _glu(xt, w_in, b_in, tm):
    t, d = xt.shape
    return pl.pallas_call(
        _glu_kernel,
        grid=(t // tm,),
        in_specs=[pl.BlockSpec((tm, d), lambda i: (i, 0)), _full((d, 2 * d)), _full((1, 2 * d))],
        out_specs=pl.BlockSpec((tm, d), lambda i: (i, 0)),
        out_shape=jax.ShapeDtypeStruct((t, d), F32),
        compiler_params=_cparams(("arbitrary",)),
        name="conv_glu",
    )(xt, w_in, b_in)


def _dwconv_kernel(h_ref, w_ref, b_ref, o_ref, buf, win, *, rows):
    s = pl.program_id(2)
    tm = h_ref.shape[0]
    width = w_ref.shape[0]
    off = CONV_HALO - (width - 1)

    @pl.when(s == 0)
    def _():
        buf[0:CONV_HALO, :] = jnp.zeros((CONV_HALO, buf.shape[1]), F32)

    @pl.when(s > 0)
    def _():
        buf[0:CONV_HALO, :] = buf[tm:tm + CONV_HALO, :]

    buf[CONV_HALO:CONV_HALO + tm, :] = h_ref[...]
    sub = 8
    for c0 in range(0, buf.shape[1], LANES):
        cs = slice(c0, c0 + LANES)
        for r0 in range(0, tm, rows):
            acc = None
            for phase in range(min(sub, width)):
                n_taps = (width - 1 - phase) // sub + 1
                start = r0 + off + phase
                span = rows + sub * (n_taps - 1)
                win[0:span, :] = buf[start:start + span, cs]
                for a in range(n_taps):
                    k = sub * a + phase
                    term = win[sub * a:sub * a + rows, :] * w_ref[k:k + 1, cs]
                    acc = term if acc is None else acc + term
            o_ref[r0:r0 + rows, cs] = acc + b_ref[:, cs]


def _dwconv(h, w_dw, b_dw, b, s, tm, cw):
    t, d = h.shape
    h3 = h.reshape(b, s, d)
    width = w_dw.shape[0]
    out = pl.pallas_call(
        functools.partial(_dwconv_kernel, rows=CONV_ROWS),
        grid=(b, d // cw, s // tm),
        in_specs=[pl.BlockSpec((None, tm, cw), lambda bi, c, si: (bi, si, c)),
                  pl.BlockSpec((width, cw), lambda bi, c, si: (0, c)),
                  pl.BlockSpec((1, cw), lambda bi, c, si: (0, c))],
        out_specs=pl.BlockSpec((None, tm, cw), lambda bi, c, si: (bi, si, c)),
        out_shape=jax.ShapeDtypeStruct((b, s, d), F32),
        scratch_shapes=[pltpu.VMEM((CONV_HALO + tm, cw), F32), pltpu.VMEM((CONV_ROWS + CONV_HALO, LANES), F32)],
        compiler_params=_cparams(("arbitrary", "arbitrary", "arbitrary")),
        name="conv_depthwise",
    )(h3, w_dw, b_dw)
    return out.reshape(t, d)


def _conv_out_kernel(c_ref, x_ref, cg_ref, cb_ref, w_ref, bo_ref, g_ref, b_ref, o_ref, *, alpha):
    u = _ln(c_ref[...], cg_ref[...], cb_ref[...])
    u = u * jax.nn.sigmoid(u)
    mix = jnp.dot(u.astype(BF16), w_ref[...], preferred_element_type=F32) + bo_ref[...]
    o_ref[...] = _ln(alpha * x_ref[...] + mix, g_ref[...], b_ref[...])


def _conv_out(c, xt, cg, cb, w_out, b_out, g1, b1, alpha, tm):
    t, d = xt.shape
    row = pl.BlockSpec((tm, d), lambda i: (i, 0))
    vec = _full((1, d))
    return pl.pallas_call(
        functools.partial(_conv_out_kernel, alpha=alpha),
        grid=(t // tm,),
        in_specs=[row, row, vec, vec, _full((d, d)), vec, vec, vec],
        out_specs=row,
        out_shape=jax.ShapeDtypeStruct((t, d), F32),
        compiler_params=_cparams(("arbitrary",)),
        name="conv_out_ln",
    )(c, xt, cg, cb, w_out, b_out, g1, b1)


def _rope_kernel(pos_ref, invf_ref, c_ref, s_ref):
    ang = pos_ref[...].astype(F32) * invf_ref[...]
    lane = lax.broadcasted_iota(I32, ang.shape, 1)
    half = ROT_DIM // 2
    c_ref[...] = jnp.where(lane < ROT_DIM, jnp.cos(ang), 1.0)
    sn = jnp.sin(ang)
    s_ref[...] = jnp.where(lane < half, -sn, jnp.where(lane < ROT_DIM, sn, 0.0))


def _rope_tables(positions, tm):
    t = positions.size
    half = ROT_DIM // 2
    inv_freq = ROPE_THETA ** (-jnp.arange(half, dtype=F32) * 2.0 / ROT_DIM)
    invf = jnp.zeros((1, LANES), F32).at[0, :half].set(inv_freq).at[0, half:ROT_DIM].set(inv_freq)
    out = jax.ShapeDtypeStruct((t, LANES), F32)
    return pl.pallas_call(
        _rope_kernel,
        grid=(t // tm,),
        in_specs=[pl.BlockSpec((tm, 1), lambda i: (i, 0)), _full((1, LANES))],
        out_specs=[pl.BlockSpec((tm, LANES), lambda i: (i, 0))] * 2,
        out_shape=[out, out],
        compiler_params=_cparams(("arbitrary",)),
        name="rope_tables",
    )(positions.reshape(t, 1), invf)


def _proj_kernel(x_ref, c_ref, s_ref, *rest, rot, heads, dil):
    n_out = len(rot)
    w_refs, o_refs = rest[:n_out], rest[n_out:2 * n_out]
    lhs, tabs, xc = rest[2 * n_out:]
    tm = o_refs[0].shape[1]
    n_lane_tiles = xc.shape[0]
    if dil > 1:
        for c in range(n_lane_tiles):
            xc[c] = x_ref[:, c * LANES:(c + 1) * LANES]
    for r in range(dil):
        rows = slice(r * tm, (r + 1) * tm)
        if dil > 1:
            src = pl.ds(r, tm, stride=dil)
            for c in range(n_lane_tiles):
                lhs[rows, c * LANES:(c + 1) * LANES] = xc[c, src, :].astype(BF16)
        else:
            src = slice(None)
            lhs[rows, :] = x_ref[...].astype(BF16)
        tabs[0, rows, :] = c_ref[src, :]
        tabs[1, rows, :] = s_ref[src, :]
    cos = tabs[0]
    sin = tabs[1]
    lane = lax.broadcasted_iota(I32, cos.shape, 1)
    first = lane < ROT_DIM // 2
    for j, o_ref in enumerate(o_refs):
        y = jnp.dot(lhs[...], w_refs[j][...], preferred_element_type=F32)
        for h in range(heads):
            t = y[:, h * HEAD_DIM:(h + 1) * HEAD_DIM]
            if rot[j]:
                partner = jnp.where(first, pltpu.roll(t, HEAD_DIM - ROT_DIM // 2, 1), pltpu.roll(t, ROT_DIM // 2, 1))
                t = t * cos + partner * sin
            t = t.astype(BF16)
            for r in range(dil):
                o_ref[r, :, h * HEAD_DIM:(h + 1) * HEAD_DIM] = t[r * tm:(r + 1) * tm]


def _project(x3, cos3, sin3, weights, rot, dil, rows, gw):
    b, s, d = x3.shape
    l = s // dil
    tm = rows // dil
    n_out = len(rot)
    out = jax.ShapeDtypeStruct((b, dil, l, gw), BF16)
    tab = pl.BlockSpec((None, rows, LANES), lambda bi, i: (bi, i, 0))
    w_specs = [pl.BlockSpec((d, gw), lambda bi, i, blk=blk: (0, blk)) for _, blk in weights]
    return pl.pallas_call(
        functools.partial(_proj_kernel, rot=rot, heads=gw // HEAD_DIM, dil=dil),
        grid=(b, s // rows),
        in_specs=[pl.BlockSpec((None, rows, d), lambda bi, i: (bi, i, 0)), tab, tab] + w_specs,
        out_specs=[pl.BlockSpec((None, dil, tm, gw), lambda bi, i: (bi, 0, i, 0))] * n_out,
        out_shape=[out] * n_out,
        scratch_shapes=[pltpu.VMEM((rows, d), BF16), pltpu.VMEM((2, rows, LANES), F32),
                        pltpu.VMEM((d // LANES, rows, LANES), F32)],
        compiler_params=_cparams(("arbitrary", "arbitrary")),
        name=f"proj_dil{dil}_n{n_out}",
    )(x3, cos3, sin3, *[w for w, _ in weights])


def _attn_kernel(q_ref, kc_ref, kh_ref, vc_ref, vh_ref, o_ref, lse_ref, kbuf, vbuf, *, heads):
    i = pl.program_id(2)
    tq = q_ref.shape[0]
    blk = ATTN_BLOCK
    kbuf[0:blk, :] = kh_ref[...]
    kbuf[blk:blk + tq, :] = kc_ref[...]
    vbuf[0:blk, :] = vh_ref[...]
    vbuf[blk:blk + tq, :] = vc_ref[...]
    scale = 1.0 / math.sqrt(HEAD_DIM)
    row = lax.broadcasted_iota(I32, (blk, 2 * blk), 0)
    col = lax.broadcasted_iota(I32, (blk, 2 * blk), 1)
    band = jnp.logical_and(col >= row, col <= row + blk)
    first_band = jnp.logical_and(band, jnp.logical_or(col >= blk, i > 0))
    lane = lax.broadcasted_iota(I32, (blk, LANES), 1)
    dims = (((1,), (1,)), ((), ()))
    for n in range(tq // blk):
        rs = slice(n * blk, (n + 1) * blk)
        ks = slice(n * blk, (n + 2) * blk)
        mask = first_band if n == 0 else band
        lse_tile = jnp.zeros((blk, LANES), F32)
        for h in range(heads):
            cs = slice(h * HEAD_DIM, (h + 1) * HEAD_DIM)
            s = lax.dot_general(q_ref[rs, cs], kbuf[ks, cs], dims, preferred_element_type=F32) * scale
            s = jnp.where(mask, s, NEG_INF)
            m = jnp.max(s, axis=1, keepdims=True)
            p = jnp.exp(s - m)
            den = jnp.sum(p, axis=1, keepdims=True)
            o = jnp.dot(p.astype(BF16), vbuf[ks, cs], preferred_element_type=F32)
            o_ref[rs, cs] = o / den
            lse_tile = jnp.where(lane == h, m + jnp.log(den), lse_tile)
        lse_ref[rs, :] = lse_tile


def _attention(q, k, v, dil, tq):
    b, _, l, gw = q.shape
    nb = tq // ATTN_BLOCK
    cur = pl.BlockSpec((None, None, tq, gw), lambda bi, r, i: (bi, r, i, 0))
    halo = pl.BlockSpec((None, None, ATTN_BLOCK, gw), lambda bi, r, i: (bi, r, jnp.maximum(i * nb - 1, 0), 0))
    o, lse = pl.pallas_call(
        functools.partial(_attn_kernel, heads=gw // HEAD_DIM),
        grid=(b, dil, l // tq),
        in_specs=[cur, cur, halo, cur, halo],
        out_specs=[pl.BlockSpec((None, None, tq, gw), lambda bi, r, i: (bi, r, i, 0)),
                   pl.BlockSpec((None, None, tq, LANES), lambda bi, r, i: (bi, r, i, 0))],
        out_shape=[jax.ShapeDtypeStruct((b, dil, l, gw), F32),
                   jax.ShapeDtypeStruct((b, dil, l, LANES), F32)],
        scratch_shapes=[pltpu.VMEM((ATTN_BLOCK + tq, gw), BF16), pltpu.VMEM((ATTN_BLOCK + tq, gw), BF16)],
        compiler_params=_cparams(("arbitrary", "arbitrary", "arbitrary")),
        name=f"attn_dil{dil}",
    )(q, k, k, v, v)
    return o, lse


def _attn_out_kernel(o0_ref, o1_ref, o2_ref, l0_ref, l1_ref, l2_ref, x_ref, w_ref, g_ref, b_ref,
                     out_ref, onat, lnat, mixed, *, alpha, heads, dils):
    tm = x_ref.shape[0]
    for gi, (o_ref, l_ref) in enumerate(((o0_ref, l0_ref), (o1_ref, l1_ref), (o2_ref, l2_ref))):
        dil = dils[gi]
        for r in range(dil):
            dst = pl.ds(r, tm // dil, stride=dil) if dil > 1 else slice(None)
            for h in range(heads):
                onat[gi * heads + h, dst, :] = o_ref[r, :, h * HEAD_DIM:(h + 1) * HEAD_DIM]
            lnat[gi, dst, :] = l_ref[r]
    l0, l1, l2 = lnat[0], lnat[1], lnat[2]
    m = jnp.maximum(jnp.maximum(l0, l1), l2)
    e0, e1, e2 = jnp.exp(l0 - m), jnp.exp(l1 - m), jnp.exp(l2 - m)
    den = e0 + e1 + e2
    w0, w1, w2 = e0 / den, e1 / den, e2 / den
    for h in range(heads):
        cs = slice(h * HEAD_DIM, (h + 1) * HEAD_DIM)
        o = w0[:, h:h + 1] * onat[h] + w1[:, h:h + 1] * onat[heads + h] + w2[:, h:h + 1] * onat[2 * heads + h]
        mixed[:, cs] = o.astype(BF16)
    mix = jnp.dot(mixed[...], w_ref[...], preferred_element_type=F32)
    out_ref[...] = _ln(alpha * x_ref[...] + mix, g_ref[...], b_ref[...])


def _attn_out(outs, lses, x3, w_o, g1, b1, alpha, tm):
    b, s, d = x3.shape
    gw = w_o.shape[0]
    dils = tuple(o.shape[1] for o in outs)
    ospecs = [pl.BlockSpec((None, dil, tm // dil, gw), lambda bi, i: (bi, 0, i, 0)) for dil in dils]
    lspecs = [pl.BlockSpec((None, dil, tm // dil, LANES), lambda bi, i: (bi, 0, i, 0)) for dil in dils]
    xrow = pl.BlockSpec((None, tm, d), lambda bi, i: (bi, i, 0))
    n = len(dils)
    return pl.pallas_call(
        functools.partial(_attn_out_kernel, alpha=alpha, heads=gw // HEAD_DIM, dils=dils),
        grid=(b, s // tm),
        in_specs=ospecs + lspecs + [xrow, _full((gw, d)), _full((1, d)), _full((1, d))],
        out_specs=xrow,
        out_shape=jax.ShapeDtypeStruct((b, s, d), F32),
        scratch_shapes=[pltpu.VMEM((n * gw // HEAD_DIM, tm, HEAD_DIM), F32), pltpu.VMEM((n, tm, LANES), F32),
                        pltpu.VMEM((tm, gw), BF16)],
        compiler_params=_cparams(("arbitrary", "arbitrary")),
        name="attn_out_ln",
    )(*outs, *lses, x3, w_o, g1, b1)


def _pack_bf16_pairs(v, holds_bf16=False):
    half = v.shape[1] // 2
    first, second = v[:, :half], v[:, half:]
    if not holds_bf16:
        first, second = first.astype(BF16).astype(F32), second.astype(BF16).astype(F32)
    return lax.bitcast_convert_type(first, U32) | (lax.bitcast_convert_type(second, U32) >> 16)


def _unpack_bf16_pairs(p):
    first = lax.bitcast_convert_type(p & jnp.uint32(0xFFFF0000), F32)
    second = lax.bitcast_convert_type(p << 16, F32)
    return first, second


def _router_kernel(x_ref, rwt_ref, rb_ref, lp_ref, lpt_ref, gate_ref, cnt_ref, base_ref, tot_ref, carry):
    @pl.when(pl.program_id(0) == 0)
    def _():
        carry[...] = jnp.zeros(carry.shape, F32)

    tm = x_ref.shape[0]
    ne = rwt_ref.shape[0]
    logits = lax.dot_general(rwt_ref[...], x_ref[...], (((1,), (1,)), ((), ())),
                             precision=lax.Precision.HIGHEST, preferred_element_type=F32)
    scores = jax.nn.sigmoid(logits)
    sel = scores + rb_ref[...]
    eidx = lax.broadcasted_iota(I32, (ne, tm), 0)
    chosen = jnp.zeros((ne, tm), F32)
    gsum = jnp.zeros((1, tm), F32)
    picks = []
    for _ in range(TOP_K):
        best = jnp.max(sel, axis=0, keepdims=True)
        j = jnp.min(jnp.where(sel == best, eidx, ne), axis=0, keepdims=True)
        onehot = eidx == j
        sc = jnp.sum(jnp.where(onehot, scores, 0.0), axis=0, keepdims=True)
        sel = jnp.where(onehot, -jnp.inf, sel)
        chosen = chosen + onehot.astype(F32)
        gsum = gsum + sc
        picks.append((onehot, sc))
    r = lax.broadcasted_iota(I32, (tm, tm), 0)
    c = lax.broadcasted_iota(I32, (tm, tm), 1)
    rank = jnp.dot(chosen.astype(BF16), (r < c).astype(BF16), preferred_element_type=F32)
    cnt = jnp.sum(chosen, axis=1, keepdims=True)
    cnt_al = jnp.floor((cnt + (SEG_ALIGN - 1)) / SEG_ALIGN) * SEG_ALIGN
    er = lax.broadcasted_iota(I32, (ne, ne), 0)
    ec = lax.broadcasted_iota(I32, (ne, ne), 1)
    seg_off = jnp.dot((ec < er).astype(BF16), jnp.broadcast_to(cnt_al, (ne, LANES)).astype(BF16),
                      preferred_element_type=F32)[:, 0:1]
    slot = rank + seg_off
    krow = lax.broadcasted_iota(I32, (TOP_K, tm), 0)
    lpt = jnp.zeros((TOP_K, tm), I32)
    gt = jnp.zeros((TOP_K, tm), F32)
    for k, (onehot, sc) in enumerate(picks):
        row = jnp.sum(jnp.where(onehot, slot, 0.0), axis=0, keepdims=True)
        lpt = jnp.where(krow == k, row.astype(I32), lpt)
        gt = jnp.where(krow == k, sc / gsum * ROUTED_SCALE, gt)
    lpt_ref[...] = lpt
    lp_ref[...] = jnp.concatenate([lpt, jnp.zeros((LANES - TOP_K, tm), I32)], axis=0).T
    gate_ref[...] = jnp.concatenate([gt, jnp.zeros((LANES - TOP_K, tm), F32)], axis=0).T
    cnt_ref[...] = cnt_al
    base_ref[...] = carry[...]
    carry[...] = carry[...] + cnt_al
    tot_ref[...] = carry[...]


def _router(xt, router_w, router_bias, tm):
    t, d = xt.shape
    ne = router_w.shape[1]
    n_tiles = t // tm
    wide = pl.BlockSpec((tm, LANES), lambda i: (i, 0))
    per_tile = pl.BlockSpec((None, ne, 1), lambda i: (i, 0, 0))
    return pl.pallas_call(
        _router_kernel,
        grid=(n_tiles,),
        in_specs=[pl.BlockSpec((tm, d), lambda i: (i, 0)), _full((ne, d)), _full((ne, 1))],
        out_specs=[wide, pl.BlockSpec((None, TOP_K, tm), lambda i: (i, 0, 0)), wide, per_tile, per_tile, _full((ne, 1))],
        out_shape=[jax.ShapeDtypeStruct((t, LANES), I32), jax.ShapeDtypeStruct((n_tiles, TOP_K, tm), I32),
                   jax.ShapeDtypeStruct((t, LANES), F32), jax.ShapeDtypeStruct((n_tiles, ne, 1), F32),
                   jax.ShapeDtypeStruct((n_tiles, ne, 1), F32), jax.ShapeDtypeStruct((ne, 1), F32)],
        scratch_shapes=[pltpu.VMEM((ne, 1), F32)],
        compiler_params=_cparams(("arbitrary",)),
        name="moe_router",
    )(xt, router_w.T, router_bias.reshape(ne, 1))


def _segment_loop(tbl, base, ne, make_copy, action):
    def body(e, off):
        n = pl.multiple_of(tbl[base + e], SEG_ALIGN)
        g = pl.multiple_of(tbl[base + ne + e], SEG_ALIGN)
        off = pl.multiple_of(off, SEG_ALIGN)

        @pl.when(n > 0)
        def _():
            action(make_copy(off, g, n))

        return off + n

    lax.fori_loop(0, ne, body, 0)


def _start(cp):
    cp.start()


def _wait(cp):
    cp.wait()


def _dispatch_kernel(fs_ref, fn_ref, tbl, lpt_ref, x_ref, xs_hbm, sorted_buf, zbuf, sem, *, ne):
    i = pl.program_id(0)
    last = pl.num_programs(0) - 1
    slot = lax.rem(i, 2)
    tt = x_ref.shape[0]
    width = _table_width(ne)

    @pl.when(i == 0)
    def _():
        zbuf[...] = jnp.zeros(zbuf.shape, zbuf.dtype)

        def visit(action):
            def body(e, carry):
                n = pl.multiple_of(fn_ref[e], SEG_ALIGN)
                s = pl.multiple_of(fs_ref[e], SEG_ALIGN)

                @pl.when(n > 0)
                def _():
                    action(pltpu.make_async_copy(zbuf.at[pl.ds(0, n), :], xs_hbm.at[pl.ds(s, n), :], sem.at[0]))

                return carry
            return body

        lax.fori_loop(0, ne, visit(_start), 0)
        lax.fori_loop(0, ne, visit(_wait), 0)

    xb = x_ref[...].astype(BF16)
    lpt = lpt_ref[...]
    n_chunks = lax.shift_right_logical(tbl[i * width + 2 * ne] + (SORT_CHUNK - 1), SORT_CHUNK.bit_length() - 1)

    def chunk(ci, carry):
        j0 = pl.multiple_of(ci * SORT_CHUNK, SORT_CHUNK)
        rows = j0 + lax.broadcasted_iota(I32, (SORT_CHUNK, tt), 0)
        hit = lpt[0:1, :] == rows
        for k in range(1, TOP_K):
            hit = jnp.logical_or(hit, lpt[k:k + 1, :] == rows)
        srt = jnp.dot(jnp.where(hit, 1.0, 0.0).astype(BF16), xb, preferred_element_type=F32)
        sorted_buf[slot, pl.ds(j0, SORT_CHUNK), :] = _pack_bf16_pairs(srt, holds_bf16=True)
        return carry

    lax.fori_loop(0, n_chunks, chunk, 0)

    def writes(sl):
        return lambda off, g, n: pltpu.make_async_copy(
            sorted_buf.at[sl, pl.ds(off, n), :], xs_hbm.at[pl.ds(g, n), :], sem.at[sl])

    _segment_loop(tbl, i * width, ne, writes(slot), _start)

    @pl.when(i > 0)
    def _():
        _segment_loop(tbl, (i - 1) * width, ne, writes(1 - slot), _wait)

    @pl.when(i == last)
    def _():
        _segment_loop(tbl, i * width, ne, writes(slot), _wait)


def _sorted_rows(tt, ne):
    return -(-(tt * TOP_K + ne * SEG_ALIGN) // SORT_CHUNK) * SORT_CHUNK


def _table_width(ne):
    return 2 * ne + SEG_ALIGN


def _dispatch(xt, lpt, table, fill_start, fill_n, n_rows, tt, br):
    t, d = xt.shape
    ne = fill_start.shape[0]
    grid_spec = pltpu.PrefetchScalarGridSpec(
        num_scalar_prefetch=3,
        grid=(t // tt,),
        in_specs=[pl.BlockSpec((None, TOP_K, tt), lambda i, *_: (i, 0, 0)),
                  pl.BlockSpec((tt, d), lambda i, *_: (i, 0))],
        out_specs=pl.BlockSpec(memory_space=pl.ANY),
        scratch_shapes=[pltpu.VMEM((2, _sorted_rows(tt, ne), d // 2), U32),
                        pltpu.VMEM((br, d // 2), U32), pltpu.SemaphoreType.DMA((2,))],
    )
    return pl.pallas_call(
        functools.partial(_dispatch_kernel, ne=ne),
        grid_spec=grid_spec,
        out_shape=jax.ShapeDtypeStruct((n_rows, d // 2), U32),
        compiler_params=_cparams(("arbitrary",)),
        name="moe_dispatch",
    )(fill_start, fill_n, table, lpt, xt)


def _expert_kernel(be_ref, nu_ref, xs_ref, wg_ref, wu_ref, wd_ref, y_ref, wg_b, wu_b, wd_b):
    i = pl.program_id(0)
    live = i < nu_ref[0]

    @pl.when(jnp.logical_and(live, jnp.logical_or(i == 0, be_ref[i] != be_ref[jnp.maximum(i - 1, 0)])))
    def _():
        wg_b[...] = wg_ref[...].astype(BF16)
        wu_b[...] = wu_ref[...].astype(BF16)
        wd_b[...] = wd_ref[...].astype(BF16)

    @pl.when(live)
    def _():
        first, second = _unpack_bf16_pairs(xs_ref[...])
        xb = jnp.concatenate([first.astype(BF16), second.astype(BF16)], axis=1)
        g = jnp.dot(xb, wg_b[...], preferred_element_type=F32)
        u = jnp.dot(xb, wu_b[...], preferred_element_type=F32)
        h = g * jax.nn.sigmoid(g) * u
        y = jnp.dot(h.astype(BF16), wd_b[...], preferred_element_type=F32)
        y_ref[...] = _pack_bf16_pairs(y)


def _experts(xs, blk_exp, n_used, w_gate, w_up, w_down, layer, br):
    p, dh = xs.shape
    d, f = w_gate.shape[2:]
    n_blocks = p // br

    def row_map(i, be, nu):
        return (jnp.minimum(i, nu[0] - 1), 0)

    def w_map(i, be, nu):
        return (layer, be[jnp.minimum(i, nu[0] - 1)], 0, 0)

    grid_spec = pltpu.PrefetchScalarGridSpec(
        num_scalar_prefetch=2,
        grid=(n_blocks,),
        in_specs=[pl.BlockSpec((br, dh), row_map),
                  pl.BlockSpec((None, None, d, f), w_map),
                  pl.BlockSpec((None, None, d, f), w_map),
                  pl.BlockSpec((None, None, f, d), w_map)],
        out_specs=pl.BlockSpec((br, dh), row_map),
        scratch_shapes=[pltpu.VMEM((d, f), BF16), pltpu.VMEM((d, f), BF16), pltpu.VMEM((f, d), BF16)],
    )
    return pl.pallas_call(
        _expert_kernel,
        grid_spec=grid_spec,
        out_shape=jax.ShapeDtypeStruct((p, dh), U32),
        compiler_params=_cparams(("arbitrary",)),
        name="moe_experts",
    )(blk_exp, n_used, xs, w_gate, w_up, w_down)


def _combine_kernel(tbl, y_hbm, lp_ref, gate_ref, x_ref, sg_ref, su_ref, sd_ref, g_ref, b_ref, o_ref,
                    ysorted, acc, lpb, gb, sem, *, ne, alpha):
    i = pl.program_id(0)
    last = pl.num_programs(0) - 1
    slot = lax.rem(i, 2)
    tt = x_ref.shape[0]
    half = x_ref.shape[1] // 2
    width = _table_width(ne)

    def gathers(sl):
        return lambda off, g, n: pltpu.make_async_copy(
            y_hbm.at[pl.ds(g, n), :], ysorted.at[sl, pl.ds(off, n), :], sem.at[sl])

    @pl.when(i == 0)
    def _():
        _segment_loop(tbl, 0, ne, gathers(0), _start)

    @pl.when(i < last)
    def _():
        _segment_loop(tbl, (i + 1) * width, ne, gathers(1 - slot), _start)

    x = x_ref[...]
    xb = x.astype(BF16)
    g = jnp.dot(xb, sg_ref[...], preferred_element_type=F32)
    u = jnp.dot(xb, su_ref[...], preferred_element_type=F32)
    h = g * jax.nn.sigmoid(g) * u
    acc[...] = jnp.dot(h.astype(BF16), sd_ref[...], preferred_element_type=F32)
    lp = lp_ref[...]
    gate = gate_ref[...]
    for k in range(TOP_K):
        lpb[k] = jnp.broadcast_to(lp[:, k:k + 1], (tt, LANES))
        gb[k] = jnp.broadcast_to(gate[:, k:k + 1], (tt, LANES))
    _segment_loop(tbl, i * width, ne, gathers(slot), _wait)
    total = tbl[i * width + 2 * ne]
    n_chunks = lax.shift_right_logical(total + (SORT_CHUNK - 1), SORT_CHUNK.bit_length() - 1)
    lane = lax.broadcasted_iota(I32, (tt, LANES), 1)

    def chunk(ci, carry):
        j0 = pl.multiple_of(ci * SORT_CHUNK, SORT_CHUNK)
        parts = []
        for c0 in range(0, SORT_CHUNK, LANES):
            cols = lane + (j0 + c0)
            gm = jnp.zeros((tt, LANES), F32)
            for k in range(TOP_K):
                gm = jnp.where(lpb[k] == cols, gb[k], gm)
            parts.append(gm)
        gmat = jnp.concatenate(parts, axis=1).astype(BF16)
        rows = j0 + lax.broadcasted_iota(I32, (SORT_CHUNK, half), 0)
        ys = jnp.where(rows < total, ysorted[slot, pl.ds(j0, SORT_CHUNK), :], jnp.uint32(0))
        first, second = _unpack_bf16_pairs(ys)
        first = first.astype(BF16)
        second = second.astype(BF16)
        acc[:, :half] += jnp.dot(gmat, first, preferred_element_type=F32)
        acc[:, half:] += jnp.dot(gmat, second, preferred_element_type=F32)
        return carry

    lax.fori_loop(0, n_chunks, chunk, 0)
    o_ref[...] = _ln(alpha * x + acc[...], g_ref[...], b_ref[...])


def _combine(table, y, lp, gate, xt, sw_gate, sw_up, sw_down, g2, b2, alpha, tt, ne):
    t, d = xt.shape
    f = sw_gate.shape[1]
    row = pl.BlockSpec((tt, d), lambda i, *_: (i, 0))
    wide = pl.BlockSpec((tt, LANES), lambda i, *_: (i, 0))
    grid_spec = pltpu.PrefetchScalarGridSpec(
        num_scalar_prefetch=1,
        grid=(t // tt,),
        in_specs=[pl.BlockSpec(memory_space=pl.ANY), wide, wide, row,
                  _full((d, f)), _full((d, f)), _full((f, d)), _full((1, d)), _full((1, d))],
        out_specs=row,
        scratch_shapes=[pltpu.VMEM((2, _sorted_rows(tt, ne), d // 2), U32),
                        pltpu.VMEM((tt, d), F32), pltpu.VMEM((TOP_K, tt, LANES), I32), pltpu.VMEM((TOP_K, tt, LANES), F32),
                        pltpu.SemaphoreType.DMA((2,))],
    )
    return pl.pallas_call(
        functools.partial(_combine_kernel, ne=ne, alpha=alpha),
        grid_spec=grid_spec,
        out_shape=jax.ShapeDtypeStruct((t, d), F32),
        compiler_params=_cparams(("arbitrary",)),
        name="moe_combine_ln",
    )(table, y, lp, gate, xt, sw_gate, sw_up, sw_down, g2, b2)


def _moe_layer(xt, router_w, router_bias, w_gate, w_up, w_down, layer, sw_gate, sw_up, sw_down, g2, b2, alpha, cfg):
    t, d = xt.shape
    ne = router_w.shape[1]
    br = cfg["moe_rows"]
    tt = cfg["moe_tile"]
    n_tiles = t // tt
    lp, lpt, gate, cnt, base, tot = _router(xt, router_w, router_bias, tt)
    cnt = cnt.reshape(n_tiles, ne).astype(I32)
    tot = tot.reshape(ne).astype(I32)
    padded = (tot + br - 1) // br * br
    pends = jnp.cumsum(padded)
    pstarts = pends - padded
    seg_start = pstarts[None, :] + base.reshape(n_tiles, ne).astype(I32)
    tile_rows = jnp.sum(cnt, axis=1, keepdims=True)
    table = jnp.concatenate([cnt, seg_start, tile_rows, jnp.zeros((n_tiles, SEG_ALIGN - 1), I32)], axis=1).reshape(-1)
    n_blocks = -(-(t * TOP_K + ne * n_tiles * SEG_ALIGN) // br) + ne
    blk_start = jnp.arange(n_blocks, dtype=I32) * br
    blk_exp = jnp.minimum(jnp.sum(pends[None, :] <= blk_start[:, None], axis=1), ne - 1).astype(I32)
    n_used = (pends[-1:] // br).astype(I32)
    xs = _dispatch(xt, lpt, table, pstarts + tot, padded - tot, n_blocks * br, tt, br)
    y = _experts(xs, blk_exp, n_used, w_gate, w_up, w_down, layer, br)
    return _combine(table, y, lp, gate, xt, sw_gate.astype(BF16), sw_up.astype(BF16), sw_down.astype(BF16),
                    g2, b2, alpha, tt, ne)


def _config(b, s, d):
    return dict(row_tm=min(512, s), conv_tm=min(512, s), conv_cw=min(256, d), proj_rows=min(512, s), attn_out_tm=256,
                moe_rows=1024, moe_tile=min(256, b * s))


def kernel(x, positions, ln1_g, ln1_b, ln2_g, ln2_b, conv_w_in, conv_b_in, conv_w_dw, conv_b_dw, conv_ln_g, conv_ln_b, conv_w_out, conv_b_out, w_kv, attn_w_q, attn_w_o, router_w, router_bias, exp_w_gate, exp_w_up, exp_w_down, sh_w_gate, sh_w_up, sh_w_down):
    b, s, d = x.shape
    t = b * s
    depth = ln1_g.shape[0]
    n_conv = conv_w_in.shape[0]
    alpha = (2.0 * depth) ** 0.25
    cfg = _config(b, s, d)
    n_groups = len(WINDOW_DILATIONS)
    gw = attn_w_o.shape[1]
    vec = lambda a: a.reshape(1, -1)

    xt = x.reshape(t, d)
    kv = None
    tables = None
    for layer in range(depth):
        g1, b1 = vec(ln1_g[layer]), vec(ln1_b[layer])
        if layer < n_conv:
            h = _glu(xt, conv_w_in[layer].astype(BF16), vec(conv_b_in[layer]), cfg["row_tm"])
            c = _dwconv(h, conv_w_dw[layer], vec(conv_b_dw[layer]), b, s, cfg["conv_tm"], cfg["conv_cw"])
            xt = _conv_out(c, xt, vec(conv_ln_g[layer]), vec(conv_ln_b[layer]), conv_w_out[layer].astype(BF16),
                           vec(conv_b_out[layer]), g1, b1, alpha, cfg["row_tm"])
        else:
            j = layer - n_conv
            if tables is None:
                cos, sin = _rope_tables(positions, cfg["row_tm"])
                tables = (cos.reshape(b, s, LANES), sin.reshape(b, s, LANES))
            x3 = xt.reshape(b, s, d)
            wq = attn_w_q[j].astype(BF16)
            outs, lses = [], []
            new_kv = []
            for g, (win, dil) in enumerate(WINDOW_DILATIONS):
                assert win // dil == ATTN_BLOCK
                if kv is None:
                    wkv = w_kv.astype(BF16)
                    q, kg, vg = _project(x3, *tables, [(wq, g), (wkv, g), (wkv, n_groups + g)], (True, True, False),
                                         dil, cfg["proj_rows"], gw)
                    new_kv.append((kg, vg))
                else:
                    (q,) = _project(x3, *tables, [(wq, g)], (True,), dil, cfg["proj_rows"], gw)
                    kg, vg = kv[g]
                o, lse = _attention(q, kg, vg, dil, min(512, s // dil))
                outs.append(o)
                lses.append(lse)
            if kv is None:
                kv = new_kv
            xt = _attn_out(outs, lses, x3, attn_w_o[j].astype(BF16), g1, b1, alpha, cfg["attn_out_tm"]).reshape(t, d)
        xt = _moe_layer(xt, router_w[layer], router_bias[layer], exp_w_gate, exp_w_up, exp_w_down, layer,
                        sh_w_gate[layer], sh_w_up[layer], sh_w_down[layer],
                        vec(ln2_g[layer]), vec(ln2_b[layer]), alpha, cfg)
    return xt.reshape(b, s, d)
```

```python
import functools
import math

import jax
import jax.numpy as jnp
from jax import lax
from jax.experimental import pallas as pl
from jax.experimental.pallas import tpu as pltpu

F32 = jnp.float32
BF16 = jnp.bfloat16
I32 = jnp.int32
U32 = jnp.uint32

LANES = 128
HEAD_DIM = 128
ROT_DIM = HEAD_DIM // 4
ROPE_THETA = 500000.0
ATTN_BLOCK = 128
WINDOW_DILATIONS = ((128, 1), (512, 4), (2048, 16))
NEG_INF = -1e30
TOP_K = 8
ROUTED_SCALE = 2.5
LN_EPS = 1e-5
CONV_HALO = 32
CONV_ROWS = 64
VMEM_LIMIT = 56 * 1024 * 1024
SEG_ALIGN = 8
SORT_CHUNK = 256


def _cparams(sem):
    return pltpu.CompilerParams(dimension_semantics=sem, vmem_limit_bytes=VMEM_LIMIT)


def _ln(y, g, b):
    mu = jnp.mean(y, axis=-1, keepdims=True)
    d = y - mu
    var = jnp.mean(d * d, axis=-1, keepdims=True)
    return d * lax.rsqrt(var + LN_EPS) * g + b


def _full(shape):
    n = len(shape)
    return pl.BlockSpec(shape, lambda *_: (0,) * n)


def _glu_kernel(x_ref, w_ref, b_ref, o_ref):
    d = o_ref.shape[-1]
    h = jnp.dot(x_ref[...].astype(BF16), w_ref[...], preferred_element_type=F32) + b_ref[...]
    o_ref[...] = h[:, :d] * jax.nn.sigmoid(h[:, d:])


def _glu(xt, w_in, b_in, tm):
    t, d = xt.shape
    return pl.pallas_call(
        _glu_kernel,
        grid=(t // tm,),
        in_specs=[pl.BlockSpec((tm, d), lambda i: (i, 0)), _full((d, 2 * d)), _full((1, 2 * d))],
        out_specs=pl.BlockSpec((tm, d), lambda i: (i, 0)),
        out_shape=jax.ShapeDtypeStruct((t, d), F32),
        compiler_params=_cparams(("arbitrary",)),
        name="conv_glu",
    )(xt, w_in, b_in)


def _dwconv_kernel(h_ref, w_ref, b_ref, o_ref, buf, win, *, rows):
    s = pl.program_id(2)
    tm = h_ref.shape[0]
    width = w_ref.shape[0]
    off = CONV_HALO - (width - 1)

    @pl.when(s == 0)
    def _():
        buf[0:CONV_HALO, :] = jnp.zeros((CONV_HALO, buf.shape[1]), F32)

    @pl.when(s > 0)
    def _():
        buf[0:CONV_HALO, :] = buf[tm:tm + CONV_HALO, :]

    buf[CONV_HALO:CONV_HALO + tm, :] = h_ref[...]
    sub = 8
    for c0 in range(0, buf.shape[1], LANES):
        cs = slice(c0, c0 + LANES)
        for r0 in range(0, tm, rows):
            acc = None
            for phase in range(min(sub, width)):
                n_taps = (width - 1 - phase) // sub + 1
                start = r0 + off + phase
                span = rows + sub * (n_taps - 1)
                win[0:span, :] = buf[start:start + span, cs]
                for a in range(n_taps):
                    k = sub * a + phase
                    term = win[sub * a:sub * a + rows, :] * w_ref[k:k + 1, cs]
                    acc = term if acc is None else acc + term
            o_ref[r0:r0 + rows, cs] = acc + b_ref[:, cs]


def _dwconv(h, w_dw, b_dw, b, s, tm, cw):
    t, d = h.shape
    h3 = h.reshape(b, s, d)
    width = w_dw.shape[0]
    out = pl.pallas_call(
        functools.partial(_dwconv_kernel, rows=CONV_ROWS),
        grid=(b, d // cw, s // tm),
        in_specs=[pl.BlockSpec((None, tm, cw), lambda bi, c, si: (bi, si, c)),
                  pl.BlockSpec((width, cw), lambda bi, c, si: (0, c)),
                  pl.BlockSpec((1, cw), lambda bi, c, si: (0, c))],
        out_specs=pl.BlockSpec((None, tm, cw), lambda bi, c, si: (bi, si, c)),
        out_shape=jax.ShapeDtypeStruct((b, s, d), F32),
        scratch_shapes=[pltpu.VMEM((CONV_HALO + tm, cw), F32), pltpu.VMEM((CONV_ROWS + CONV_HALO, LANES), F32)],
        compiler_params=_cparams(("arbitrary", "arbitrary", "arbitrary")),
        name="conv_depthwise",
    )(h3, w_dw, b_dw)
    return out.reshape(t, d)


def _conv_out_kernel(c_ref, x_ref, cg_ref, cb_ref, w_ref, bo_ref, g_ref, b_ref, o_ref, *, alpha):
    u = _ln(c_ref[...], cg_ref[...], cb_ref[...])
    u = u * jax.nn.sigmoid(u)
    mix = jnp.dot(u.astype(BF16), w_ref[...], preferred_element_type=F32) + bo_ref[...]
    o_ref[...] = _ln(alpha * x_ref[...] + mix, g_ref[...], b_ref[...])


def _conv_out(c, xt, cg, cb, w_out, b_out, g1, b1, alpha, tm):
    t, d = xt.shape
    row = pl.BlockSpec((tm, d), lambda i: (i, 0))
    vec = _full((1, d))
    return pl.pallas_call(
        functools.partial(_conv_out_kernel, alpha=alpha),
        grid=(t // tm,),
        in_specs=[row, row, vec, vec, _full((d, d)), vec, vec, vec],
        out_specs=row,
        out_shape=jax.ShapeDtypeStruct((t, d), F32),
        compiler_params=_cparams(("arbitrary",)),
        name="conv_out_ln",
    )(c, xt, cg, cb, w_out, b_out, g1, b1)


def _rope_kernel(pos_ref, invf_ref, c_ref, s_ref):
    ang = pos_ref[...].astype(F32) * invf_ref[...]
    lane = lax.broadcasted_iota(I32, ang.shape, 1)
    half = ROT_DIM // 2
    c_ref[...] = jnp.where(lane < ROT_DIM, jnp.cos(ang), 1.0)
    sn = jnp.sin(ang)
    s_ref[...] = jnp.where(lane < half, -sn, jnp.where(lane < ROT_DIM, sn, 0.0))


def _rope_tables(positions, tm):
    t = positions.size
    half = ROT_DIM // 2
    inv_freq = ROPE_THETA ** (-jnp.arange(half, dtype=F32) * 2.0 / ROT_DIM)
    invf = jnp.zeros((1, LANES), F32).at[0, :half].set(inv_freq).at[0, half:ROT_DIM].set(inv_freq)
    out = jax.ShapeDtypeStruct((t, LANES), F32)
    return pl.pallas_call(
        _rope_kernel,
        grid=(t // tm,),
        in_specs=[pl.BlockSpec((tm, 1), lambda i: (i, 0)), _full((1, LANES))],
        out_specs=[pl.BlockSpec((tm, LANES), lambda i: (i, 0))] * 2,
        out_shape=[out, out],
        compiler_params=_cparams(("arbitrary",)),
        name="rope_tables",
    )(positions.reshape(t, 1), invf)


def _proj_kernel(x_ref, c_ref, s_ref, *rest, rot, heads, dil):
    n_out = len(rot)
    w_refs, o_refs = rest[:n_out], rest[n_out:2 * n_out]
    lhs, tabs, xc = rest[2 * n_out:]
    tm = o_refs[0].shape[1]
    n_lane_tiles = xc.shape[0]
    if dil > 1:
        for c in range(n_lane_tiles):
            xc[c] = x_ref[:, c * LANES:(c + 1) * LANES]
    for r in range(dil):
        rows = slice(r * tm, (r + 1) * tm)
        if dil > 1:
            src = pl.ds(r, tm, stride=dil)
            for c in range(n_lane_tiles):
                lhs[rows, c * LANES:(c + 1) * LANES] = xc[c, src, :].astype(BF16)
        else:
            src = slice(None)
            lhs[rows, :] = x_ref[...].astype(BF16)
        tabs[0, rows, :] = c_ref[src, :]
        tabs[1, rows, :] = s_ref[src, :]
    cos = tabs[0]
    sin = tabs[1]
    lane = lax.broadcasted_iota(I32, cos.shape, 1)
    first = lane < ROT_DIM // 2
    for j, o_ref in enumerate(o_refs):
        y = jnp.dot(lhs[...], w_refs[j][...], preferred_element_type=F32)
        for h in range(heads):
            t = y[:, h * HEAD_DIM:(h + 1) * HEAD_DIM]
            if rot[j]:
                partner = jnp.where(first, pltpu.roll(t, HEAD_DIM - ROT_DIM // 2, 1), pltpu.roll(t, ROT_DIM // 2, 1))
                t = t * cos + partner * sin
            t = t.astype(BF16)
            for r in range(dil):
                o_ref[r, :, h * HEAD_DIM:(h + 1) * HEAD_DIM] = t[r * tm:(r + 1) * tm]


def _project(x3, cos3, sin3, weights, rot, dil, rows, gw):
    b, s, d = x3.shape
    l = s // dil
    tm = rows // dil
    n_out = len(rot)
    out = jax.ShapeDtypeStruct((b, dil, l, gw), BF16)
    tab = pl.BlockSpec((None, rows, LANES), lambda bi, i: (bi, i, 0))
    w_specs = [pl.BlockSpec((d, gw), lambda bi, i, blk=blk: (0, blk)) for _, blk in weights]
    return pl.pallas_call(
        functools.partial(_proj_kernel, rot=rot, heads=gw // HEAD_DIM, dil=dil),
        grid=(b, s // rows),
        in_specs=[pl.BlockSpec((None, rows, d), lambda bi, i: (bi, i, 0)), tab, tab] + w_specs,
        out_specs=[pl.BlockSpec((None, dil, tm, gw), lambda bi, i: (bi, 0, i, 0))] * n_out,
        out_shape=[out] * n_out,
        scratch_shapes=[pltpu.VMEM((rows, d), BF16), pltpu.VMEM((2, rows, LANES), F32),
                        pltpu.VMEM((d // LANES, rows, LANES), F32)],
        compiler_params=_cparams(("arbitrary", "arbitrary")),
        name=f"proj_dil{dil}_n{n_out}",
    )(x3, cos3, sin3, *[w for w, _ in weights])


def _attn_kernel(q_ref, kc_ref, kh_ref, vc_ref, vh_ref, o_ref, lse_ref, kbuf, vbuf, *, heads):
    i = pl.program_id(2)
    tq = q_ref.shape[0]
    blk = ATTN_BLOCK
    kbuf[0:blk, :] = kh_ref[...]
    kbuf[blk:blk + tq, :] = kc_ref[...]
    vbuf[0:blk, :] = vh_ref[...]
    vbuf[blk:blk + tq, :] = vc_ref[...]
    scale = 1.0 / math.sqrt(HEAD_DIM)
    row = lax.broadcasted_iota(I32, (blk, 2 * blk), 0)
    col = lax.broadcasted_iota(I32, (blk, 2 * blk), 1)
    band = jnp.logical_and(col >= row, col <= row + blk)
    first_band = jnp.logical_and(band, jnp.logical_or(col >= blk, i > 0))
    lane = lax.broadcasted_iota(I32, (blk, LANES), 1)
    dims = (((1,), (1,)), ((), ()))
    for n in range(tq // blk):
        rs = slice(n * blk, (n + 1) * blk)
        ks = slice(n * blk, (n + 2) * blk)
        mask = first_band if n == 0 else band
        lse_tile = jnp.zeros((blk, LANES), F32)
        for h in range(heads):
            cs = slice(h * HEAD_DIM, (h + 1) * HEAD_DIM)
            s = lax.dot_general(q_ref[rs, cs], kbuf[ks, cs], dims, preferred_element_type=F32) * scale
            s = jnp.where(mask, s, NEG_INF)
            m = jnp.max(s, axis=1, keepdims=True)
            p = jnp.exp(s - m)
            den = jnp.sum(p, axis=1, keepdims=True)
            o = jnp.dot(p.astype(BF16), vbuf[ks, cs], preferred_element_type=F32)
            o_ref[rs, cs] = o / den
            lse_tile = jnp.where(lane == h, m + jnp.log(den), lse_tile)
        lse_ref[rs, :] = lse_tile


def _attention(q, k, v, dil, tq):
    b, _, l, gw = q.shape
    nb = tq // ATTN_BLOCK
    cur = pl.BlockSpec((None, None, tq, gw), lambda bi, r, i: (bi, r, i, 0))
    halo = pl.BlockSpec((None, None, ATTN_BLOCK, gw), lambda bi, r, i: (bi, r, jnp.maximum(i * nb - 1, 0), 0))
    o, lse = pl.pallas_call(
        functools.partial(_attn_kernel, heads=gw // HEAD_DIM),
        grid=(b, dil, l // tq),
        in_specs=[cur, cur, halo, cur, halo],
        out_specs=[pl.BlockSpec((None, None, tq, gw), lambda bi, r, i: (bi, r, i, 0)),
                   pl.BlockSpec((None, None, tq, LANES), lambda bi, r, i: (bi, r, i, 0))],
        out_shape=[jax.ShapeDtypeStruct((b, dil, l, gw), F32),
                   jax.ShapeDtypeStruct((b, dil, l, LANES), F32)],
        scratch_shapes=[pltpu.VMEM((ATTN_BLOCK + tq, gw), BF16), pltpu.VMEM((ATTN_BLOCK + tq, gw), BF16)],
        compiler_params=_cparams(("arbitrary", "arbitrary", "arbitrary")),
        name=f"attn_dil{dil}",
    )(q, k, k, v, v)
    return o, lse


def _attn_out_kernel(o0_ref, o1_ref, o2_ref, l0_ref, l1_ref, l2_ref, x_ref, w_ref, g_ref, b_ref,
                     out_ref, onat, lnat, mixed, *, alpha, heads, dils):
    tm = x_ref.shape[0]
    for gi, (o_ref, l_ref) in enumerate(((o0_ref, l0_ref), (o1_ref, l1_ref), (o2_ref, l2_ref))):
        dil = dils[gi]
        for r in range(dil):
            dst = pl.ds(r, tm // dil, stride=dil) if dil > 1 else slice(None)
            for h in range(heads):
                onat[gi * heads + h, dst, :] = o_ref[r, :, h * HEAD_DIM:(h + 1) * HEAD_DIM]
            lnat[gi, dst, :] = l_ref[r]
    l0, l1, l2 = lnat[0], lnat[1], lnat[2]
    m = jnp.maximum(jnp.maximum(l0, l1), l2)
    e0, e1, e2 = jnp.exp(l0 - m), jnp.exp(l1 - m), jnp.exp(l2 - m)
    den = e0 + e1 + e2
    w0, w1, w2 = e0 / den, e1 / den, e2 / den
    for h in range(heads):
        cs = slice(h * HEAD_DIM, (h + 1) * HEAD_DIM)
        o = w0[:, h:h + 1] * onat[h] + w1[:, h:h + 1] * onat[heads + h] + w2[:, h:h + 1] * onat[2 * heads + h]
        mixed[:, cs] = o.astype(BF16)
    mix = jnp.dot(mixed[...], w_ref[...], preferred_element_type=F32)
    out_ref[...] = _ln(alpha * x_ref[...] + mix, g_ref[...], b_ref[...])


def _attn_out(outs, lses, x3, w_o, g1, b1, alpha, tm):
    b, s, d = x3.shape
    gw = w_o.shape[0]
    dils = tuple(o.shape[1] for o in outs)
    ospecs = [pl.BlockSpec((None, dil, tm // dil, gw), lambda bi, i: (bi, 0, i, 0)) for dil in dils]
    lspecs = [pl.BlockSpec((None, dil, tm // dil, LANES), lambda bi, i: (bi, 0, i, 0)) for dil in dils]
    xrow = pl.BlockSpec((None, tm, d), lambda bi, i: (bi, i, 0))
    n = len(dils)
    return pl.pallas_call(
        functools.partial(_attn_out_kernel, alpha=alpha, heads=gw // HEAD_DIM, dils=dils),
        grid=(b, s // tm),
        in_specs=ospecs + lspecs + [xrow, _full((gw, d)), _full((1, d)), _full((1, d))],
        out_specs=xrow,
        out_shape=jax.ShapeDtypeStruct((b, s, d), F32),
        scratch_shapes=[pltpu.VMEM((n * gw // HEAD_DIM, tm, HEAD_DIM), F32), pltpu.VMEM((n, tm, LANES), F32),
                        pltpu.VMEM((tm, gw), BF16)],
        compiler_params=_cparams(("arbitrary", "arbitrary")),
        name="attn_out_ln",
    )(*outs, *lses, x3, w_o, g1, b1)


def _pack_bf16_pairs(v, holds_bf16=False):
    half = v.shape[1] // 2
    first, second = v[:, :half], v[:, half:]
    if not holds_bf16:
        first, second = first.astype(BF16).astype(F32), second.astype(BF16).astype(F32)
    return lax.bitcast_convert_type(first, U32) | (lax.bitcast_convert_type(second, U32) >> 16)


def _unpack_bf16_pairs(p):
    first = lax.bitcast_convert_type(p & jnp.uint32(0xFFFF0000), F32)
    second = lax.bitcast_convert_type(p << 16, F32)
    return first, second


def _router_kernel(x_ref, rwt_ref, rb_ref, lp_ref, lpt_ref, gate_ref, cnt_ref, base_ref, tot_ref, carry):
    @pl.when(pl.program_id(0) == 0)
    def _():
        carry[...] = jnp.zeros(carry.shape, F32)

    tm = x_ref.shape[0]
    ne = rwt_ref.shape[0]
    logits = lax.dot_general(rwt_ref[...], x_ref[...], (((1,), (1,)), ((), ())),
                             precision=lax.Precision.HIGHEST, preferred_element_type=F32)
    scores = jax.nn.sigmoid(logits)
    sel = scores + rb_ref[...]
    eidx = lax.broadcasted_iota(I32, (ne, tm), 0)
    chosen = jnp.zeros((ne, tm), F32)
    gsum = jnp.zeros((1, tm), F32)
    picks = []
    for _ in range(TOP_K):
        best = jnp.max(sel, axis=0, keepdims=True)
        j = jnp.min(jnp.where(sel == best, eidx, ne), axis=0, keepdims=True)
        onehot = eidx == j
        sc = jnp.sum(jnp.where(onehot, scores, 0.0), axis=0, keepdims=True)
        sel = jnp.where(onehot, -jnp.inf, sel)
        chosen = chosen + onehot.astype(F32)
        gsum = gsum + sc
        picks.append((onehot, sc))
    r = lax.broadcasted_iota(I32, (tm, tm), 0)
    c = lax.broadcasted_iota(I32, (tm, tm), 1)
    rank = jnp.dot(chosen.astype(BF16), (r < c).astype(BF16), preferred_element_type=F32)
    cnt = jnp.sum(chosen, axis=1, keepdims=True)
    cnt_al = jnp.floor((cnt + (SEG_ALIGN - 1)) / SEG_ALIGN) * SEG_ALIGN
    er = lax.broadcasted_iota(I32, (ne, ne), 0)
    ec = lax.broadcasted_iota(I32, (ne, ne), 1)
    seg_off = jnp.dot((ec < er).astype(BF16), jnp.broadcast_to(cnt_al, (ne, LANES)).astype(BF16),
                      preferred_element_type=F32)[:, 0:1]
    slot = rank + seg_off
    krow = lax.broadcasted_iota(I32, (TOP_K, tm), 0)
    lpt = jnp.zeros((TOP_K, tm), I32)
    gt = jnp.zeros((TOP_K, tm), F32)
    for k, (onehot, sc) in enumerate(picks):
        row = jnp.sum(jnp.where(onehot, slot, 0.0), axis=0, keepdims=True)
        lpt = jnp.where(krow == k, row.astype(I32), lpt)
        gt = jnp.where(krow == k, sc / gsum * ROUTED_SCALE, gt)
    lpt_ref[...] = lpt
    lp_ref[...] = jnp.concatenate([lpt, jnp.zeros((LANES - TOP_K, tm), I32)], axis=0).T
    gate_ref[...] = jnp.concatenate([gt, jnp.zeros((LANES - TOP_K, tm), F32)], axis=0).T
    cnt_ref[...] = cnt_al
    base_ref[...] = carry[...]
    carry[...] = carry[...] + cnt_al
    tot_ref[...] = carry[...]


def _router(xt, router_w, router_bias, tm):
    t, d = xt.shape
    ne = router_w.shape[1]
    n_tiles = t // tm
    wide = pl.BlockSpec((tm, LANES), lambda i: (i, 0))
    per_tile = pl.BlockSpec((None, ne, 1), lambda i: (i, 0, 0))
    return pl.pallas_call(
        _router_kernel,
        grid=(n_tiles,),
        in_specs=[pl.BlockSpec((tm, d), lambda i: (i, 0)), _full((ne, d)), _full((ne, 1))],
        out_specs=[wide, pl.BlockSpec((None, TOP_K, tm), lambda i: (i, 0, 0)), wide, per_tile, per_tile, _full((ne, 1))],
        out_shape=[jax.ShapeDtypeStruct((t, LANES), I32), jax.ShapeDtypeStruct((n_tiles, TOP_K, tm), I32),
                   jax.ShapeDtypeStruct((t, LANES), F32), jax.ShapeDtypeStruct((n_tiles, ne, 1), F32),
                   jax.ShapeDtypeStruct((n_tiles, ne, 1), F32), jax.ShapeDtypeStruct((ne, 1), F32)],
        scratch_shapes=[pltpu.VMEM((ne, 1), F32)],
        compiler_params=_cparams(("arbitrary",)),
        name="moe_router",
    )(xt, router_w.T, router_bias.reshape(ne, 1))


def _segment_loop(tbl, base, ne, make_copy, action):
    def body(e, off):
        n = pl.multiple_of(tbl[base + e], SEG_ALIGN)
        g = pl.multiple_of(tbl[base + ne + e], SEG_ALIGN)
        off = pl.multiple_of(off, SEG_ALIGN)

        @pl.when(n > 0)
        def _():
            action(make_copy(off, g, n))

        return off + n

    lax.fori_loop(0, ne, body, 0)


def _for_sorted_chunks(chunk, n_chunks, max_chunks):
    for ci in range(max_chunks - 1):
        chunk(ci)

    @pl.when(n_chunks == max_chunks)
    def _():
        chunk(max_chunks - 1)


def _start(cp):
    cp.start()


def _wait(cp):
    cp.wait()


def _dispatch_kernel(fs_ref, fn_ref, tbl, lpt_ref, x_ref, xs_hbm, sorted_buf, zbuf, sem, *, ne):
    i = pl.program_id(0)
    last = pl.num_programs(0) - 1
    slot = lax.rem(i, 2)
    tt = x_ref.shape[0]
    width = _table_width(ne)

    @pl.when(i == 0)
    def _():
        zbuf[...] = jnp.zeros(zbuf.shape, zbuf.dtype)

        def visit(action):
            def body(e, carry):
                n = pl.multiple_of(fn_ref[e], SEG_ALIGN)
                s = pl.multiple_of(fs_ref[e], SEG_ALIGN)

                @pl.when(n > 0)
                def _():
                    action(pltpu.make_async_copy(zbuf.at[pl.ds(0, n), :], xs_hbm.at[pl.ds(s, n), :], sem.at[0]))

                return carry
            return body

        lax.fori_loop(0, ne, visit(_start), 0)
        lax.fori_loop(0, ne, visit(_wait), 0)

    xb = x_ref[...].astype(BF16)
    lpt = lpt_ref[...]
    n_chunks = lax.shift_right_logical(tbl[i * width + 2 * ne] + (SORT_CHUNK - 1), SORT_CHUNK.bit_length() - 1)

    def chunk(ci):
        j0 = ci * SORT_CHUNK
        rows = j0 + lax.broadcasted_iota(I32, (SORT_CHUNK, tt), 0)
        hit = lpt[0:1, :] == rows
        for k in range(1, TOP_K):
            hit = jnp.logical_or(hit, lpt[k:k + 1, :] == rows)
        srt = jnp.dot(jnp.where(hit, 1.0, 0.0).astype(BF16), xb, preferred_element_type=F32)
        sorted_buf[slot, j0:j0 + SORT_CHUNK, :] = _pack_bf16_pairs(srt, holds_bf16=True)

    _for_sorted_chunks(chunk, n_chunks, sorted_buf.shape[1] // SORT_CHUNK)

    def writes(sl):
        return lambda off, g, n: pltpu.make_async_copy(
            sorted_buf.at[sl, pl.ds(off, n), :], xs_hbm.at[pl.ds(g, n), :], sem.at[sl])

    _segment_loop(tbl, i * width, ne, writes(slot), _start)

    def wait_all(step, sl):
        tot = pl.multiple_of(tbl[step * width + 2 * ne], SEG_ALIGN)
        pltpu.make_async_copy(sorted_buf.at[sl, pl.ds(0, tot), :], xs_hbm.at[pl.ds(0, tot), :], sem.at[sl]).wait()

    @pl.when(i > 0)
    def _():
        wait_all(i - 1, 1 - slot)

    @pl.when(i == last)
    def _():
        wait_all(i, slot)


def _sorted_rows(tt, ne):
    return -(-(tt * TOP_K + ne * SEG_ALIGN) // SORT_CHUNK) * SORT_CHUNK


def _table_width(ne):
    return 2 * ne + SEG_ALIGN


def _dispatch(xt, lpt, table, fill_start, fill_n, n_rows, tt, br):
    t, d = xt.shape
    ne = fill_start.shape[0]
    grid_spec = pltpu.PrefetchScalarGridSpec(
        num_scalar_prefetch=3,
        grid=(t // tt,),
        in_specs=[pl.BlockSpec((None, TOP_K, tt), lambda i, *_: (i, 0, 0)),
                  pl.BlockSpec((tt, d), lambda i, *_: (i, 0))],
        out_specs=pl.BlockSpec(memory_space=pl.ANY),
        scratch_shapes=[pltpu.VMEM((2, _sorted_rows(tt, ne), d // 2), U32),
                        pltpu.VMEM((br, d // 2), U32), pltpu.SemaphoreType.DMA((2,))],
    )
    return pl.pallas_call(
        functools.partial(_dispatch_kernel, ne=ne),
        grid_spec=grid_spec,
        out_shape=jax.ShapeDtypeStruct((n_rows, d // 2), U32),
        compiler_params=_cparams(("arbitrary",)),
        name="moe_dispatch",
    )(fill_start, fill_n, table, lpt, xt)


def _expert_kernel(be_ref, nu_ref, xs_ref, wg_ref, wu_ref, wd_ref, y_ref, wg_b, wu_b, wd_b):
    i = pl.program_id(0)
    live = i < nu_ref[0]

    @pl.when(jnp.logical_and(live, jnp.logical_or(i == 0, be_ref[i] != be_ref[jnp.maximum(i - 1, 0)])))
    def _():
        wg_b[...] = wg_ref[...].astype(BF16)
        wu_b[...] = wu_ref[...].astype(BF16)
        wd_b[...] = wd_ref[...].astype(BF16)

    @pl.when(live)
    def _():
        first, second = _unpack_bf16_pairs(xs_ref[...])
        xb = jnp.concatenate([first.astype(BF16), second.astype(BF16)], axis=1)
        g = jnp.dot(xb, wg_b[...], preferred_element_type=F32)
        u = jnp.dot(xb, wu_b[...], preferred_element_type=F32)
        h = g * jax.nn.sigmoid(g) * u
        y = jnp.dot(h.astype(BF16), wd_b[...], preferred_element_type=F32)
        y_ref[...] = _pack_bf16_pairs(y)


def _experts(xs, blk_exp, n_used, w_gate, w_up, w_down, layer, br):
    p, dh = xs.shape
    d, f = w_gate.shape[2:]
    n_blocks = p // br

    def row_map(i, be, nu):
        return (jnp.minimum(i, nu[0] - 1), 0)

    def w_map(i, be, nu):
        return (layer, be[jnp.minimum(i, nu[0] - 1)], 0, 0)

    grid_spec = pltpu.PrefetchScalarGridSpec(
        num_scalar_prefetch=2,
        grid=(n_blocks,),
        in_specs=[pl.BlockSpec((br, dh), row_map),
                  pl.BlockSpec((None, None, d, f), w_map),
                  pl.BlockSpec((None, None, d, f), w_map),
                  pl.BlockSpec((None, None, f, d), w_map)],
        out_specs=pl.BlockSpec((br, dh), row_map),
        scratch_shapes=[pltpu.VMEM((d, f), BF16), pltpu.VMEM((d, f), BF16), pltpu.VMEM((f, d), BF16)],
    )
    return pl.pallas_call(
        _expert_kernel,
        grid_spec=grid_spec,
        out_shape=jax.ShapeDtypeStruct((p, dh), U32),
        compiler_params=_cparams(("arbitrary",)),
        name="moe_experts",
    )(blk_exp, n_used, xs, w_gate, w_up, w_down)


def _combine_kernel(tbl, y_hbm, lp_ref, gate_ref, x_ref, sg_ref, su_ref, sd_ref, g_ref, b_ref, o_ref,
                    ysorted, acc, lpb, gb, sem, *, ne, alpha):
    i = pl.program_id(0)
    last = pl.num_programs(0) - 1
    slot = lax.rem(i, 2)
    tt = x_ref.shape[0]
    half = x_ref.shape[1] // 2
    width = _table_width(ne)

    def gathers(sl):
        return lambda off, g, n: pltpu.make_async_copy(
            y_hbm.at[pl.ds(g, n), :], ysorted.at[sl, pl.ds(off, n), :], sem.at[sl])

    @pl.when(i == 0)
    def _():
        _segment_loop(tbl, 0, ne, gathers(0), _start)

    @pl.when(i < last)
    def _():
        _segment_loop(tbl, (i + 1) * width, ne, gathers(1 - slot), _start)

    x = x_ref[...]
    xb = x.astype(BF16)
    g = jnp.dot(xb, sg_ref[...], preferred_element_type=F32)
    u = jnp.dot(xb, su_ref[...], preferred_element_type=F32)
    h = g * jax.nn.sigmoid(g) * u
    acc[...] = jnp.dot(h.astype(BF16), sd_ref[...], preferred_element_type=F32)
    lp = lp_ref[...]
    gate = gate_ref[...]
    for k in range(TOP_K):
        lpb[k] = jnp.broadcast_to(lp[:, k:k + 1], (tt, LANES))
        gb[k] = jnp.broadcast_to(gate[:, k:k + 1], (tt, LANES))
    total = pl.multiple_of(tbl[i * width + 2 * ne], SEG_ALIGN)
    pltpu.make_async_copy(y_hbm.at[pl.ds(0, total), :], ysorted.at[slot, pl.ds(0, total), :], sem.at[slot]).wait()
    n_chunks = lax.shift_right_logical(total + (SORT_CHUNK - 1), SORT_CHUNK.bit_length() - 1)
    lane = lax.broadcasted_iota(I32, (tt, LANES), 1)

    def chunk(ci):
        j0 = ci * SORT_CHUNK
        parts = []
        for c0 in range(0, SORT_CHUNK, LANES):
            cols = lane + (j0 + c0)
            gm = jnp.zeros((tt, LANES), F32)
            for k in range(TOP_K):
                gm = jnp.where(lpb[k] == cols, gb[k], gm)
            parts.append(gm)
        gmat = jnp.concatenate(parts, axis=1).astype(BF16)
        rows = j0 + lax.broadcasted_iota(I32, (SORT_CHUNK, half), 0)
        ys = jnp.where(rows < total, ysorted[slot, j0:j0 + SORT_CHUNK, :], jnp.uint32(0))
        first, second = _unpack_bf16_pairs(ys)
        first = first.astype(BF16)
        second = second.astype(BF16)
        acc[:, :half] += jnp.dot(gmat, first, preferred_element_type=F32)
        acc[:, half:] += jnp.dot(gmat, second, preferred_element_type=F32)

    _for_sorted_chunks(chunk, n_chunks, ysorted.shape[1] // SORT_CHUNK)
    o_ref[...] = _ln(alpha * x + acc[...], g_ref[...], b_ref[...])


def _combine(table, y, lp, gate, xt, sw_gate, sw_up, sw_down, g2, b2, alpha, tt, ne):
    t, d = xt.shape
    f = sw_gate.shape[1]
    row = pl.BlockSpec((tt, d), lambda i, *_: (i, 0))
    wide = pl.BlockSpec((tt, LANES), lambda i, *_: (i, 0))
    grid_spec = pltpu.PrefetchScalarGridSpec(
        num_scalar_prefetch=1,
        grid=(t // tt,),
        in_specs=[pl.BlockSpec(memory_space=pl.ANY), wide, wide, row,
                  _full((d, f)), _full((d, f)), _full((f, d)), _full((1, d)), _full((1, d))],
        out_specs=row,
        scratch_shapes=[pltpu.VMEM((2, _sorted_rows(tt, ne), d // 2), U32),
                        pltpu.VMEM((tt, d), F32), pltpu.VMEM((TOP_K, tt, LANES), I32), pltpu.VMEM((TOP_K, tt, LANES), F32),
                        pltpu.SemaphoreType.DMA((2,))],
    )
    return pl.pallas_call(
        functools.partial(_combine_kernel, ne=ne, alpha=alpha),
        grid_spec=grid_spec,
        out_shape=jax.ShapeDtypeStruct((t, d), F32),
        compiler_params=_cparams(("arbitrary",)),
        name="moe_combine_ln",
    )(table, y, lp, gate, xt, sw_gate, sw_up, sw_down, g2, b2)


def _moe_layer(xt, router_w, router_bias, w_gate, w_up, w_down, layer, sw_gate, sw_up, sw_down, g2, b2, alpha, cfg):
    t, d = xt.shape
    ne = router_w.shape[1]
    br = cfg["moe_rows"]
    tt = cfg["moe_tile"]
    n_tiles = t // tt
    lp, lpt, gate, cnt, base, tot = _router(xt, router_w, router_bias, tt)
    cnt = cnt.reshape(n_tiles, ne).astype(I32)
    tot = tot.reshape(ne).astype(I32)
    padded = (tot + br - 1) // br * br
    pends = jnp.cumsum(padded)
    pstarts = pends - padded
    seg_start = pstarts[None, :] + base.reshape(n_tiles, ne).astype(I32)
    tile_rows = jnp.sum(cnt, axis=1, keepdims=True)
    table = jnp.concatenate([cnt, seg_start, tile_rows, jnp.zeros((n_tiles, SEG_ALIGN - 1), I32)], axis=1).reshape(-1)
    n_blocks = -(-(t * TOP_K + ne * n_tiles * SEG_ALIGN) // br) + ne
    blk_start = jnp.arange(n_blocks, dtype=I32) * br
    blk_exp = jnp.minimum(jnp.sum(pends[None, :] <= blk_start[:, None], axis=1), ne - 1).astype(I32)
    n_used = (pends[-1:] // br).astype(I32)
    xs = _dispatch(xt, lpt, table, pstarts + tot, padded - tot, n_blocks * br, tt, br)
    y = _experts(xs, blk_exp, n_used, w_gate, w_up, w_down, layer, br)
    return _combine(table, y, lp, gate, xt, sw_gate.astype(BF16), sw_up.astype(BF16), sw_down.astype(BF16),
                    g2, b2, alpha, tt, ne)


def _config(b, s, d):
    return dict(row_tm=min(512, s), conv_tm=min(512, s), conv_cw=min(256, d), proj_rows=min(512, s), attn_out_tm=256,
                moe_rows=1024, moe_tile=min(256, b * s))


def kernel(x, positions, ln1_g, ln1_b, ln2_g, ln2_b, conv_w_in, conv_b_in, conv_w_dw, conv_b_dw, conv_ln_g, conv_ln_b, conv_w_out, conv_b_out, w_kv, attn_w_q, attn_w_o, router_w, router_bias, exp_w_gate, exp_w_up, exp_w_down, sh_w_gate, sh_w_up, sh_w_down):
    b, s, d = x.shape
    t = b * s
    depth = ln1_g.shape[0]
    n_conv = conv_w_in.shape[0]
    alpha = (2.0 * depth) ** 0.25
    cfg = _config(b, s, d)
    n_groups = len(WINDOW_DILATIONS)
    gw = attn_w_o.shape[1]
    vec = lambda a: a.reshape(1, -1)

    xt = x.reshape(t, d)
    kv = None
    tables = None
    for layer in range(depth):
        g1, b1 = vec(ln1_g[layer]), vec(ln1_b[layer])
        if layer < n_conv:
            h = _glu(xt, conv_w_in[layer].astype(BF16), vec(conv_b_in[layer]), cfg["row_tm"])
            c = _dwconv(h, conv_w_dw[layer], vec(conv_b_dw[layer]), b, s, cfg["conv_tm"], cfg["conv_cw"])
            xt = _conv_out(c, xt, vec(conv_ln_g[layer]), vec(conv_ln_b[layer]), conv_w_out[layer].astype(BF16),
                           vec(conv_b_out[layer]), g1, b1, alpha, cfg["row_tm"])
        else:
            j = layer - n_conv
            if tables is None:
                cos, sin = _rope_tables(positions, cfg["row_tm"])
                tables = (cos.reshape(b, s, LANES), sin.reshape(b, s, LANES))
            x3 = xt.reshape(b, s, d)
            wq = attn_w_q[j].astype(BF16)
            outs, lses = [], []
            new_kv = []
            for g, (win, dil) in enumerate(WINDOW_DILATIONS):
                assert win // dil == ATTN_BLOCK
                if kv is None:
                    wkv = w_kv.astype(BF16)
                    q, kg, vg = _project(x3, *tables, [(wq, g), (wkv, g), (wkv, n_groups + g)], (True, True, False),
                                         dil, cfg["proj_rows"], gw)
                    new_kv.append((kg, vg))
                else:
                    (q,) = _project(x3, *tables, [(wq, g)], (True,), dil, cfg["proj_rows"], gw)
                    kg, vg = kv[g]
                o, lse = _attention(q, kg, vg, dil, min(512, s // dil))
                outs.append(o)
                lses.append(lse)
            if kv is None:
                kv = new_kv
            xt = _attn_out(outs, lses, x3, attn_w_o[j].astype(BF16), g1, b1, alpha, cfg["attn_out_tm"]).reshape(t, d)
        xt = _moe_layer(xt, router_w[layer], router_bias[layer], exp_w_gate, exp_w_up, exp_w_down, layer,
                        sh_w_gate[layer], sh_w_up[layer], sh_w_down[layer],
                        vec(ln2_g[layer]), vec(ln2_b[layer]), alpha, cfg)
    return xt.reshape(b, s, d)
```

```python
import functools
import math

import jax
import jax.numpy as jnp
from jax import lax
from jax.experimental import pallas as pl
from jax.experimental.pallas import tpu as pltpu

F32 = jnp.float32
BF16 = jnp.bfloat16
I32 = jnp.int32
U32 = jnp.uint32

LANES = 128
HEAD_DIM = 128
ROT_DIM = HEAD_DIM // 4
ROPE_THETA = 500000.0
ATTN_BLOCK = 128
WINDOW_DILATIONS = ((128, 1), (512, 4), (2048, 16))
NEG_INF = -1e30
TOP_K = 8
ROUTED_SCALE = 2.5
LN_EPS = 1e-5
CONV_HALO = 32
CONV_ROWS = 64
VMEM_LIMIT = 56 * 1024 * 1024
SEG_ALIGN = 8
SORT_CHUNK = 256


def _cparams(sem):
    return pltpu.CompilerParams(dimension_semantics=sem, vmem_limit_bytes=VMEM_LIMIT)


def _ln(y, g, b):
    mu = jnp.mean(y, axis=-1, keepdims=True)
    d = y - mu
    var = jnp.mean(d * d, axis=-1, keepdims=True)
    return d * lax.rsqrt(var + LN_EPS) * g + b


def _full(shape):
    n = len(shape)
    return pl.BlockSpec(shape, lambda *_: (0,) * n)


def _glu_kernel(x_ref, w_ref, b_ref, o_ref):
    d = o_ref.shape[-1]
    h = jnp.dot(x_ref[...].astype(BF16), w_ref[...], preferred_element_type=F32) + b_ref[...]
    o_ref[...] = h[:, :d] * jax.nn.sigmoid(h[:, d:])


def _glu(xt, w_in, b_in, tm):
    t, d = xt.shape
    return pl.pallas_call(
        _glu_kernel,
        grid=(t // tm,),
        in_specs=[pl.BlockSpec((tm, d), lambda i: (i, 0)), _full((d, 2 * d)), _full((1, 2 * d))],
        out_specs=pl.BlockSpec((tm, d), lambda i: (i, 0)),
        out_shape=jax.ShapeDtypeStruct((t, d), F32),
        compiler_params=_cparams(("arbitrary",)),
        name="conv_glu",
    )(xt, w_in, b_in)


def _dwconv_kernel(h_ref, w_ref, b_ref, o_ref, buf, win, *, rows):
    s = pl.program_id(2)
    tm = h_ref.shape[0]
    width = w_ref.shape[0]
    off = CONV_HALO - (width - 1)

    @pl.when(s == 0)
    def _():
        buf[0:CONV_HALO, :] = jnp.zeros((CONV_HALO, buf.shape[1]), F32)

    @pl.when(s > 0)
    def _():
        buf[0:CONV_HALO, :] = buf[tm:tm + CONV_HALO, :]

    buf[CONV_HALO:CONV_HALO + tm, :] = h_ref[...]
    sub = 8
    for c0 in range(0, buf.shape[1], LANES):
        cs = slice(c0, c0 + LANES)
        for r0 in range(0, tm, rows):
            acc = None
            for phase in range(min(sub, width)):
                n_taps = (width - 1 - phase) // sub + 1
                start = r0 + off + phase
                span = rows + sub * (n_taps - 1)
                win[0:span, :] = buf[start:start + span, cs]
                for a in range(n_taps):
                    k = sub * a + phase
                    term = win[sub * a:sub * a + rows, :] * w_ref[k:k + 1, cs]
                    acc = term if acc is None else acc + term
            o_ref[r0:r0 + rows, cs] = acc + b_ref[:, cs]


def _dwconv(h, w_dw, b_dw, b, s, tm, cw):
    t, d = h.shape
    h3 = h.reshape(b, s, d)
    width = w_dw.shape[0]
    out = pl.pallas_call(
        functools.partial(_dwconv_kernel, rows=CONV_ROWS),
        grid=(b, d // cw, s // tm),
        in_specs=[pl.BlockSpec((None, tm, cw), lambda bi, c, si: (bi, si, c)),
                  pl.BlockSpec((width, cw), lambda bi, c, si: (0, c)),
                  pl.BlockSpec((1, cw), lambda bi, c, si: (0, c))],
        out_specs=pl.BlockSpec((None, tm, cw), lambda bi, c, si: (bi, si, c)),
        out_shape=jax.ShapeDtypeStruct((b, s, d), F32),
        scratch_shapes=[pltpu.VMEM((CONV_HALO + tm, cw), F32), pltpu.VMEM((CONV_ROWS + CONV_HALO, LANES), F32)],
        compiler_params=_cparams(("arbitrary", "arbitrary", "arbitrary")),
        name="conv_depthwise",
    )(h3, w_dw, b_dw)
    return out.reshape(t, d)


def _conv_out_kernel(c_ref, x_ref, cg_ref, cb_ref, w_ref, bo_ref, g_ref, b_ref, o_ref, *, alpha):
    u = _ln(c_ref[...], cg_ref[...], cb_ref[...])
    u = u * jax.nn.sigmoid(u)
    mix = jnp.dot(u.astype(BF16), w_ref[...], preferred_element_type=F32) + bo_ref[...]
    o_ref[...] = _ln(alpha * x_ref[...] + mix, g_ref[...], b_ref[...])


def _conv_out(c, xt, cg, cb, w_out, b_out, g1, b1, alpha, tm):
    t, d = xt.shape
    row = pl.BlockSpec((tm, d), lambda i: (i, 0))
    vec = _full((1, d))
    return pl.pallas_call(
        functools.partial(_conv_out_kernel, alpha=alpha),
        grid=(t // tm,),
        in_specs=[row, row, vec, vec, _full((d, d)), vec, vec, vec],
        out_specs=row,
        out_shape=jax.ShapeDtypeStruct((t, d), F32),
        compiler_params=_cparams(("arbitrary",)),
        name="conv_out_ln",
    )(c, xt, cg, cb, w_out, b_out, g1, b1)


def _rope_kernel(pos_ref, invf_ref, c_ref, s_ref):
    ang = pos_ref[...].astype(F32) * invf_ref[...]
    lane = lax.broadcasted_iota(I32, ang.shape, 1)
    half = ROT_DIM // 2
    c_ref[...] = jnp.where(lane < ROT_DIM, jnp.cos(ang), 1.0)
    sn = jnp.sin(ang)
    s_ref[...] = jnp.where(lane < half, -sn, jnp.where(lane < ROT_DIM, sn, 0.0))


def _rope_tables(positions, tm):
    t = positions.size
    half = ROT_DIM // 2
    inv_freq = ROPE_THETA ** (-jnp.arange(half, dtype=F32) * 2.0 / ROT_DIM)
    invf = jnp.zeros((1, LANES), F32).at[0, :half].set(inv_freq).at[0, half:ROT_DIM].set(inv_freq)
    out = jax.ShapeDtypeStruct((t, LANES), F32)
    return pl.pallas_call(
        _rope_kernel,
        grid=(t // tm,),
        in_specs=[pl.BlockSpec((tm, 1), lambda i: (i, 0)), _full((1, LANES))],
        out_specs=[pl.BlockSpec((tm, LANES), lambda i: (i, 0))] * 2,
        out_shape=[out, out],
        compiler_params=_cparams(("arbitrary",)),
        name="rope_tables",
    )(positions.reshape(t, 1), invf)


def _proj_kernel(x_ref, c_ref, s_ref, *rest, rot, heads, dil):
    n_out = len(rot)
    w_refs, o_refs = rest[:n_out], rest[n_out:2 * n_out]
    lhs, tabs, xc = rest[2 * n_out:]
    tm = o_refs[0].shape[1]
    n_lane_tiles = xc.shape[0]
    if dil > 1:
        for c in range(n_lane_tiles):
            xc[c] = x_ref[:, c * LANES:(c + 1) * LANES]
    for r in range(dil):
        rows = slice(r * tm, (r + 1) * tm)
        if dil > 1:
            src = pl.ds(r, tm, stride=dil)
            for c in range(n_lane_tiles):
                lhs[rows, c * LANES:(c + 1) * LANES] = xc[c, src, :].astype(BF16)
        else:
            src = slice(None)
            lhs[rows, :] = x_ref[...].astype(BF16)
        tabs[0, rows, :] = c_ref[src, :]
        tabs[1, rows, :] = s_ref[src, :]
    cos = tabs[0]
    sin = tabs[1]
    lane = lax.broadcasted_iota(I32, cos.shape, 1)
    first = lane < ROT_DIM // 2
    for j, o_ref in enumerate(o_refs):
        y = jnp.dot(lhs[...], w_refs[j][...], preferred_element_type=F32)
        for h in range(heads):
            t = y[:, h * HEAD_DIM:(h + 1) * HEAD_DIM]
            if rot[j]:
                partner = jnp.where(first, pltpu.roll(t, HEAD_DIM - ROT_DIM // 2, 1), pltpu.roll(t, ROT_DIM // 2, 1))
                t = t * cos + partner * sin
            t = t.astype(BF16)
            for r in range(dil):
                o_ref[r, :, h * HEAD_DIM:(h + 1) * HEAD_DIM] = t[r * tm:(r + 1) * tm]


def _project(x3, cos3, sin3, weights, rot, dil, rows, gw):
    b, s, d = x3.shape
    l = s // dil
    tm = rows // dil
    n_out = len(rot)
    out = jax.ShapeDtypeStruct((b, dil, l, gw), BF16)
    tab = pl.BlockSpec((None, rows, LANES), lambda bi, i: (bi, i, 0))
    w_specs = [pl.BlockSpec((d, gw), lambda bi, i, blk=blk: (0, blk)) for _, blk in weights]
    return pl.pallas_call(
        functools.partial(_proj_kernel, rot=rot, heads=gw // HEAD_DIM, dil=dil),
        grid=(b, s // rows),
        in_specs=[pl.BlockSpec((None, rows, d), lambda bi, i: (bi, i, 0)), tab, tab] + w_specs,
        out_specs=[pl.BlockSpec((None, dil, tm, gw), lambda bi, i: (bi, 0, i, 0))] * n_out,
        out_shape=[out] * n_out,
        scratch_shapes=[pltpu.VMEM((rows, d), BF16), pltpu.VMEM((2, rows, LANES), F32),
                        pltpu.VMEM((d // LANES, rows, LANES), F32)],
        compiler_params=_cparams(("arbitrary", "arbitrary")),
        name=f"proj_dil{dil}_n{n_out}",
    )(x3, cos3, sin3, *[w for w, _ in weights])


def _attn_kernel(q_ref, kc_ref, kh_ref, vc_ref, vh_ref, o_ref, lse_ref, kbuf, vbuf, *, heads):
    i = pl.program_id(2)
    tq = q_ref.shape[0]
    blk = ATTN_BLOCK
    kbuf[0:blk, :] = kh_ref[...]
    kbuf[blk:blk + tq, :] = kc_ref[...]
    vbuf[0:blk, :] = vh_ref[...]
    vbuf[blk:blk + tq, :] = vc_ref[...]
    scale = 1.0 / math.sqrt(HEAD_DIM)
    row = lax.broadcasted_iota(I32, (blk, 2 * blk), 0)
    col = lax.broadcasted_iota(I32, (blk, 2 * blk), 1)
    band = jnp.logical_and(col >= row, col <= row + blk)
    first_band = jnp.logical_and(band, jnp.logical_or(col >= blk, i > 0))
    lane = lax.broadcasted_iota(I32, (blk, LANES), 1)
    dims = (((1,), (1,)), ((), ()))
    for n in range(tq // blk):
        rs = slice(n * blk, (n + 1) * blk)
        ks = slice(n * blk, (n + 2) * blk)
        mask = first_band if n == 0 else band
        lse_tile = jnp.zeros((blk, LANES), F32)
        for h in range(heads):
            cs = slice(h * HEAD_DIM, (h + 1) * HEAD_DIM)
            s = lax.dot_general(q_ref[rs, cs], kbuf[ks, cs], dims, preferred_element_type=F32) * scale
            s = jnp.where(mask, s, NEG_INF)
            m = jnp.max(s, axis=1, keepdims=True)
            p = jnp.exp(s - m)
            den = jnp.sum(p, axis=1, keepdims=True)
            o = jnp.dot(p.astype(BF16), vbuf[ks, cs], preferred_element_type=F32)
            o_ref[rs, cs] = (o / den).astype(o_ref.dtype)
            lse_tile = jnp.where(lane == h, m + jnp.log(den), lse_tile)
        lse_ref[rs, :] = lse_tile


def _attention(q, k, v, dil, tq):
    b, _, l, gw = q.shape
    nb = tq // ATTN_BLOCK
    cur = pl.BlockSpec((None, None, tq, gw), lambda bi, r, i: (bi, r, i, 0))
    halo = pl.BlockSpec((None, None, ATTN_BLOCK, gw), lambda bi, r, i: (bi, r, jnp.maximum(i * nb - 1, 0), 0))
    o, lse = pl.pallas_call(
        functools.partial(_attn_kernel, heads=gw // HEAD_DIM),
        grid=(b, dil, l // tq),
        in_specs=[cur, cur, halo, cur, halo],
        out_specs=[pl.BlockSpec((None, None, tq, gw), lambda bi, r, i: (bi, r, i, 0)),
                   pl.BlockSpec((None, None, tq, LANES), lambda bi, r, i: (bi, r, i, 0))],
        out_shape=[jax.ShapeDtypeStruct((b, dil, l, gw), BF16),
                   jax.ShapeDtypeStruct((b, dil, l, LANES), F32)],
        scratch_shapes=[pltpu.VMEM((ATTN_BLOCK + tq, gw), BF16), pltpu.VMEM((ATTN_BLOCK + tq, gw), BF16)],
        compiler_params=_cparams(("arbitrary", "arbitrary", "arbitrary")),
        name=f"attn_dil{dil}",
    )(q, k, k, v, v)
    return o, lse


def _attn_out_kernel(o0_ref, o1_ref, o2_ref, l0_ref, l1_ref, l2_ref, x_ref, w_ref, g_ref, b_ref,
                     out_ref, onat, lnat, mixed, *, alpha, heads, dils):
    tm = x_ref.shape[0]
    for gi, (o_ref, l_ref) in enumerate(((o0_ref, l0_ref), (o1_ref, l1_ref), (o2_ref, l2_ref))):
        dil = dils[gi]
        for r in range(dil):
            dst = pl.ds(r, tm // dil, stride=dil) if dil > 1 else slice(None)
            for h in range(heads):
                onat[gi * heads + h, dst, :] = o_ref[r, :, h * HEAD_DIM:(h + 1) * HEAD_DIM].astype(F32)
            lnat[gi, dst, :] = l_ref[r]
    l0, l1, l2 = lnat[0], lnat[1], lnat[2]
    m = jnp.maximum(jnp.maximum(l0, l1), l2)
    e0, e1, e2 = jnp.exp(l0 - m), jnp.exp(l1 - m), jnp.exp(l2 - m)
    den = e0 + e1 + e2
    w0, w1, w2 = e0 / den, e1 / den, e2 / den
    for h in range(heads):
        cs = slice(h * HEAD_DIM, (h + 1) * HEAD_DIM)
        o = w0[:, h:h + 1] * onat[h] + w1[:, h:h + 1] * onat[heads + h] + w2[:, h:h + 1] * onat[2 * heads + h]
        mixed[:, cs] = o.astype(BF16)
    mix = jnp.dot(mixed[...], w_ref[...], preferred_element_type=F32)
    out_ref[...] = _ln(alpha * x_ref[...] + mix, g_ref[...], b_ref[...])


def _attn_out(outs, lses, x3, w_o, g1, b1, alpha, tm):
    b, s, d = x3.shape
    gw = w_o.shape[0]
    dils = tuple(o.shape[1] for o in outs)
    ospecs = [pl.BlockSpec((None, dil, tm // dil, gw), lambda bi, i: (bi, 0, i, 0)) for dil in dils]
    lspecs = [pl.BlockSpec((None, dil, tm // dil, LANES), lambda bi, i: (bi, 0, i, 0)) for dil in dils]
    xrow = pl.BlockSpec((None, tm, d), lambda bi, i: (bi, i, 0))
    n = len(dils)
    return pl.pallas_call(
        functools.partial(_attn_out_kernel, alpha=alpha, heads=gw // HEAD_DIM, dils=dils),
        grid=(b, s // tm),
        in_specs=ospecs + lspecs + [xrow, _full((gw, d)), _full((1, d)), _full((1, d))],
        out_specs=xrow,
        out_shape=jax.ShapeDtypeStruct((b, s, d), F32),
        scratch_shapes=[pltpu.VMEM((n * gw // HEAD_DIM, tm, HEAD_DIM), F32), pltpu.VMEM((n, tm, LANES), F32),
                        pltpu.VMEM((tm, gw), BF16)],
        compiler_params=_cparams(("arbitrary", "arbitrary")),
        name="attn_out_ln",
    )(*outs, *lses, x3, w_o, g1, b1)


def _pack_bf16_pairs(v, holds_bf16=False):
    half = v.shape[1] // 2
    first, second = v[:, :half], v[:, half:]
    if not holds_bf16:
        first, second = first.astype(BF16).astype(F32), second.astype(BF16).astype(F32)
    return lax.bitcast_convert_type(first, U32) | (lax.bitcast_convert_type(second, U32) >> 16)


def _unpack_bf16_pairs(p):
    first = lax.bitcast_convert_type(p & jnp.uint32(0xFFFF0000), F32)
    second = lax.bitcast_convert_type(p << 16, F32)
    return first, second


def _router_kernel(x_ref, rwt_ref, rb_ref, lp_ref, lpt_ref, gate_ref, cnt_ref, base_ref, tot_ref, carry):
    @pl.when(pl.program_id(0) == 0)
    def _():
        carry[...] = jnp.zeros(carry.shape, F32)

    tm = x_ref.shape[0]
    ne = rwt_ref.shape[0]
    logits = lax.dot_general(rwt_ref[...], x_ref[...], (((1,), (1,)), ((), ())),
                             precision=lax.Precision.HIGHEST, preferred_element_type=F32)
    scores = jax.nn.sigmoid(logits)
    sel = scores + rb_ref[...]
    eidx = lax.broadcasted_iota(I32, (ne, tm), 0)
    chosen = jnp.zeros((ne, tm), F32)
    gsum = jnp.zeros((1, tm), F32)
    picks = []
    for _ in range(TOP_K):
        best = jnp.max(sel, axis=0, keepdims=True)
        j = jnp.min(jnp.where(sel == best, eidx, ne), axis=0, keepdims=True)
        onehot = eidx == j
        sc = jnp.sum(jnp.where(onehot, scores, 0.0), axis=0, keepdims=True)
        sel = jnp.where(onehot, -jnp.inf, sel)
        chosen = chosen + onehot.astype(F32)
        gsum = gsum + sc
        picks.append((onehot, sc))
    r = lax.broadcasted_iota(I32, (tm, tm), 0)
    c = lax.broadcasted_iota(I32, (tm, tm), 1)
    rank = jnp.dot(chosen.astype(BF16), (r < c).astype(BF16), preferred_element_type=F32)
    cnt = jnp.sum(chosen, axis=1, keepdims=True)
    cnt_al = jnp.floor((cnt + (SEG_ALIGN - 1)) / SEG_ALIGN) * SEG_ALIGN
    er = lax.broadcasted_iota(I32, (ne, ne), 0)
    ec = lax.broadcasted_iota(I32, (ne, ne), 1)
    seg_off = jnp.dot((ec < er).astype(BF16), jnp.broadcast_to(cnt_al, (ne, LANES)).astype(BF16),
                      preferred_element_type=F32)[:, 0:1]
    slot = rank + seg_off
    krow = lax.broadcasted_iota(I32, (TOP_K, tm), 0)
    lpt = jnp.zeros((TOP_K, tm), I32)
    gt = jnp.zeros((TOP_K, tm), F32)
    for k, (onehot, sc) in enumerate(picks):
        row = jnp.sum(jnp.where(onehot, slot, 0.0), axis=0, keepdims=True)
        lpt = jnp.where(krow == k, row.astype(I32), lpt)
        gt = jnp.where(krow == k, sc / gsum * ROUTED_SCALE, gt)
    lpt_ref[...] = lpt
    lp_ref[...] = jnp.concatenate([lpt, jnp.zeros((LANES - TOP_K, tm), I32)], axis=0).T
    gate_ref[...] = jnp.concatenate([gt, jnp.zeros((LANES - TOP_K, tm), F32)], axis=0).T
    cnt_ref[...] = cnt_al
    base_ref[...] = carry[...]
    carry[...] = carry[...] + cnt_al
    tot_ref[...] = carry[...]


def _router(xt, router_w, router_bias, tm):
    t, d = xt.shape
    ne = router_w.shape[1]
    n_tiles = t // tm
    wide = pl.BlockSpec((tm, LANES), lambda i: (i, 0))
    per_tile = pl.BlockSpec((None, ne, 1), lambda i: (i, 0, 0))
    return pl.pallas_call(
        _router_kernel,
        grid=(n_tiles,),
        in_specs=[pl.BlockSpec((tm, d), lambda i: (i, 0)), _full((ne, d)), _full((ne, 1))],
        out_specs=[wide, pl.BlockSpec((None, TOP_K, tm), lambda i: (i, 0, 0)), wide, per_tile, per_tile, _full((ne, 1))],
        out_shape=[jax.ShapeDtypeStruct((t, LANES), I32), jax.ShapeDtypeStruct((n_tiles, TOP_K, tm), I32),
                   jax.ShapeDtypeStruct((t, LANES), F32), jax.ShapeDtypeStruct((n_tiles, ne, 1), F32),
                   jax.ShapeDtypeStruct((n_tiles, ne, 1), F32), jax.ShapeDtypeStruct((ne, 1), F32)],
        scratch_shapes=[pltpu.VMEM((ne, 1), F32)],
        compiler_params=_cparams(("arbitrary",)),
        name="moe_router",
    )(xt, router_w.T, router_bias.reshape(ne, 1))


def _segment_loop(tbl, base, ne, make_copy, action):
    def body(e, off):
        n = pl.multiple_of(tbl[base + e], SEG_ALIGN)
        g = pl.multiple_of(tbl[base + ne + e], SEG_ALIGN)
        off = pl.multiple_of(off, SEG_ALIGN)

        @pl.when(n > 0)
        def _():
            action(make_copy(off, g, n))

        return off + n

    lax.fori_loop(0, ne, body, 0)


def _for_sorted_chunks(chunk, n_chunks, max_chunks):
    for ci in range(max_chunks - 1):
        chunk(ci)

    @pl.when(n_chunks == max_chunks)
    def _():
        chunk(max_chunks - 1)


def _start(cp):
    cp.start()


def _wait(cp):
    cp.wait()


def _dispatch_kernel(fs_ref, fn_ref, tbl, lpt_ref, x_ref, xs_hbm, sorted_buf, zbuf, sem, *, ne):
    i = pl.program_id(0)
    last = pl.num_programs(0) - 1
    slot = lax.rem(i, 2)
    tt = x_ref.shape[0]
    width = _table_width(ne)

    @pl.when(i == 0)
    def _():
        zbuf[...] = jnp.zeros(zbuf.shape, zbuf.dtype)

        def visit(action):
            def body(e, carry):
                n = pl.multiple_of(fn_ref[e], SEG_ALIGN)
                s = pl.multiple_of(fs_ref[e], SEG_ALIGN)

                @pl.when(n > 0)
                def _():
                    action(pltpu.make_async_copy(zbuf.at[pl.ds(0, n), :], xs_hbm.at[pl.ds(s, n), :], sem.at[0]))

                return carry
            return body

        lax.fori_loop(0, ne, visit(_start), 0)
        lax.fori_loop(0, ne, visit(_wait), 0)

    xb = x_ref[...].astype(BF16)
    lpt = lpt_ref[...]
    n_chunks = lax.shift_right_logical(tbl[i * width + 2 * ne] + (SORT_CHUNK - 1), SORT_CHUNK.bit_length() - 1)

    def chunk(ci):
        j0 = ci * SORT_CHUNK
        rows = j0 + lax.broadcasted_iota(I32, (SORT_CHUNK, tt), 0)
        hit = lpt[0:1, :] == rows
        for k in range(1, TOP_K):
            hit = jnp.logical_or(hit, lpt[k:k + 1, :] == rows)
        srt = jnp.dot(jnp.where(hit, 1.0, 0.0).astype(BF16), xb, preferred_element_type=F32)
        sorted_buf[slot, j0:j0 + SORT_CHUNK, :] = _pack_bf16_pairs(srt, holds_bf16=True)

    _for_sorted_chunks(chunk, n_chunks, sorted_buf.shape[1] // SORT_CHUNK)

    def writes(sl):
        return lambda off, g, n: pltpu.make_async_copy(
            sorted_buf.at[sl, pl.ds(off, n), :], xs_hbm.at[pl.ds(g, n), :], sem.at[sl])

    _segment_loop(tbl, i * width, ne, writes(slot), _start)

    def wait_all(step, sl):
        tot = pl.multiple_of(tbl[step * width + 2 * ne], SEG_ALIGN)
        pltpu.make_async_copy(sorted_buf.at[sl, pl.ds(0, tot), :], xs_hbm.at[pl.ds(0, tot), :], sem.at[sl]).wait()

    @pl.when(i > 0)
    def _():
        wait_all(i - 1, 1 - slot)

    @pl.when(i == last)
    def _():
        wait_all(i, slot)


def _sorted_rows(tt, ne):
    rows = -(-(tt * TOP_K + ne * SEG_ALIGN) // SORT_CHUNK) * SORT_CHUNK
    assert rows < 2 ** 16
    return rows


def _table_width(ne):
    return 2 * ne + SEG_ALIGN


def _dispatch(xt, lpt, table, fill_start, fill_n, n_rows, tt, br):
    t, d = xt.shape
    ne = fill_start.shape[0]
    grid_spec = pltpu.PrefetchScalarGridSpec(
        num_scalar_prefetch=3,
        grid=(t // tt,),
        in_specs=[pl.BlockSpec((None, TOP_K, tt), lambda i, *_: (i, 0, 0)),
                  pl.BlockSpec((tt, d), lambda i, *_: (i, 0))],
        out_specs=pl.BlockSpec(memory_space=pl.ANY),
        scratch_shapes=[pltpu.VMEM((2, _sorted_rows(tt, ne), d // 2), U32),
                        pltpu.VMEM((br, d // 2), U32), pltpu.SemaphoreType.DMA((2,))],
    )
    return pl.pallas_call(
        functools.partial(_dispatch_kernel, ne=ne),
        grid_spec=grid_spec,
        out_shape=jax.ShapeDtypeStruct((n_rows, d // 2), U32),
        compiler_params=_cparams(("arbitrary",)),
        name="moe_dispatch",
    )(fill_start, fill_n, table, lpt, xt)


def _expert_kernel(be_ref, nu_ref, xs_ref, wg_ref, wu_ref, wd_ref, y_ref, wg_b, wu_b, wd_b):
    i = pl.program_id(0)
    live = i < nu_ref[0]

    @pl.when(jnp.logical_and(live, jnp.logical_or(i == 0, be_ref[i] != be_ref[jnp.maximum(i - 1, 0)])))
    def _():
        wg_b[...] = wg_ref[...].astype(BF16)
        wu_b[...] = wu_ref[...].astype(BF16)
        wd_b[...] = wd_ref[...].astype(BF16)

    @pl.when(live)
    def _():
        first, second = _unpack_bf16_pairs(xs_ref[...])
        xb = jnp.concatenate([first.astype(BF16), second.astype(BF16)], axis=1)
        g = jnp.dot(xb, wg_b[...], preferred_element_type=F32)
        u = jnp.dot(xb, wu_b[...], preferred_element_type=F32)
        h = g * jax.nn.sigmoid(g) * u
        y = jnp.dot(h.astype(BF16), wd_b[...], preferred_element_type=F32)
        y_ref[...] = _pack_bf16_pairs(y)


def _experts(xs, blk_exp, n_used, w_gate, w_up, w_down, layer, br):
    p, dh = xs.shape
    d, f = w_gate.shape[2:]
    n_blocks = p // br

    def row_map(i, be, nu):
        return (jnp.minimum(i, nu[0] - 1), 0)

    def w_map(i, be, nu):
        return (layer, be[jnp.minimum(i, nu[0] - 1)], 0, 0)

    grid_spec = pltpu.PrefetchScalarGridSpec(
        num_scalar_prefetch=2,
        grid=(n_blocks,),
        in_specs=[pl.BlockSpec((br, dh), row_map),
                  pl.BlockSpec((None, None, d, f), w_map),
                  pl.BlockSpec((None, None, d, f), w_map),
                  pl.BlockSpec((None, None, f, d), w_map)],
        out_specs=pl.BlockSpec((br, dh), row_map),
        scratch_shapes=[pltpu.VMEM((d, f), BF16), pltpu.VMEM((d, f), BF16), pltpu.VMEM((f, d), BF16)],
    )
    return pl.pallas_call(
        _expert_kernel,
        grid_spec=grid_spec,
        out_shape=jax.ShapeDtypeStruct((p, dh), U32),
        compiler_params=_cparams(("arbitrary",)),
        name="moe_experts",
    )(blk_exp, n_used, xs, w_gate, w_up, w_down)


def _combine_kernel(tbl, y_hbm, lp_ref, gate_ref, x_ref, sg_ref, su_ref, sd_ref, g_ref, b_ref, o_ref,
                    ysorted, acc, lpb, gb, sem, *, ne, alpha):
    i = pl.program_id(0)
    last = pl.num_programs(0) - 1
    slot = lax.rem(i, 2)
    tt = x_ref.shape[0]
    half = x_ref.shape[1] // 2
    width = _table_width(ne)

    def gathers(sl):
        return lambda off, g, n: pltpu.make_async_copy(
            y_hbm.at[pl.ds(g, n), :], ysorted.at[sl, pl.ds(off, n), :], sem.at[sl])

    @pl.when(i == 0)
    def _():
        _segment_loop(tbl, 0, ne, gathers(0), _start)

    @pl.when(i < last)
    def _():
        _segment_loop(tbl, (i + 1) * width, ne, gathers(1 - slot), _start)

    x = x_ref[...]
    xb = x.astype(BF16)
    g = jnp.dot(xb, sg_ref[...], preferred_element_type=F32)
    u = jnp.dot(xb, su_ref[...], preferred_element_type=F32)
    h = g * jax.nn.sigmoid(g) * u
    acc[...] = jnp.dot(h.astype(BF16), sd_ref[...], preferred_element_type=F32)
    gate_bits = lax.bitcast_convert_type(gate_ref[...].astype(BF16).astype(F32), U32)
    word = gate_bits | lax.bitcast_convert_type(lp_ref[...], U32)
    for k in range(TOP_K):
        wb = jnp.broadcast_to(word[:, k:k + 1], (tt, LANES))
        lpb[k] = lax.bitcast_convert_type(wb & jnp.uint32(0xFFFF), I32)
        gb[k] = lax.bitcast_convert_type(wb & jnp.uint32(0xFFFF0000), F32)
    total = pl.multiple_of(tbl[i * width + 2 * ne], SEG_ALIGN)
    pltpu.make_async_copy(y_hbm.at[pl.ds(0, total), :], ysorted.at[slot, pl.ds(0, total), :], sem.at[slot]).wait()
    n_chunks = lax.shift_right_logical(total + (SORT_CHUNK - 1), SORT_CHUNK.bit_length() - 1)
    lane = lax.broadcasted_iota(I32, (tt, LANES), 1)

    def chunk(ci):
        j0 = ci * SORT_CHUNK
        parts = []
        for c0 in range(0, SORT_CHUNK, LANES):
            cols = lane + (j0 + c0)
            gm = jnp.zeros((tt, LANES), F32)
            for k in range(TOP_K):
                gm = jnp.where(lpb[k] == cols, gb[k], gm)
            parts.append(gm)
        gmat = jnp.concatenate(parts, axis=1).astype(BF16)
        rows = j0 + lax.broadcasted_iota(I32, (SORT_CHUNK, half), 0)
        ys = jnp.where(rows < total, ysorted[slot, j0:j0 + SORT_CHUNK, :], jnp.uint32(0))
        first, second = _unpack_bf16_pairs(ys)
        first = first.astype(BF16)
        second = second.astype(BF16)
        acc[:, :half] += jnp.dot(gmat, first, preferred_element_type=F32)
        acc[:, half:] += jnp.dot(gmat, second, preferred_element_type=F32)

    _for_sorted_chunks(chunk, n_chunks, ysorted.shape[1] // SORT_CHUNK)
    o_ref[...] = _ln(alpha * x + acc[...], g_ref[...], b_ref[...])


def _combine(table, y, lp, gate, xt, sw_gate, sw_up, sw_down, g2, b2, alpha, tt, ne):
    t, d = xt.shape
    f = sw_gate.shape[1]
    row = pl.BlockSpec((tt, d), lambda i, *_: (i, 0))
    wide = pl.BlockSpec((tt, LANES), lambda i, *_: (i, 0))
    grid_spec = pltpu.PrefetchScalarGridSpec(
        num_scalar_prefetch=1,
        grid=(t // tt,),
        in_specs=[pl.BlockSpec(memory_space=pl.ANY), wide, wide, row,
                  _full((d, f)), _full((d, f)), _full((f, d)), _full((1, d)), _full((1, d))],
        out_specs=row,
        scratch_shapes=[pltpu.VMEM((2, _sorted_rows(tt, ne), d // 2), U32),
                        pltpu.VMEM((tt, d), F32), pltpu.VMEM((TOP_K, tt, LANES), I32), pltpu.VMEM((TOP_K, tt, LANES), F32),
                        pltpu.SemaphoreType.DMA((2,))],
    )
    return pl.pallas_call(
        functools.partial(_combine_kernel, ne=ne, alpha=alpha),
        grid_spec=grid_spec,
        out_shape=jax.ShapeDtypeStruct((t, d), F32),
        compiler_params=_cparams(("arbitrary",)),
        name="moe_combine_ln",
    )(table, y, lp, gate, xt, sw_gate, sw_up, sw_down, g2, b2)


def _moe_layer(xt, router_w, router_bias, w_gate, w_up, w_down, layer, sw_gate, sw_up, sw_down, g2, b2, alpha, cfg):
    t, d = xt.shape
    ne = router_w.shape[1]
    br = cfg["moe_rows"]
    tt = cfg["moe_tile"]
    n_tiles = t // tt
    lp, lpt, gate, cnt, base, tot = _router(xt, router_w, router_bias, tt)
    cnt = cnt.reshape(n_tiles, ne).astype(I32)
    tot = tot.reshape(ne).astype(I32)
    padded = (tot + br - 1) // br * br
    pends = jnp.cumsum(padded)
    pstarts = pends - padded
    seg_start = pstarts[None, :] + base.reshape(n_tiles, ne).astype(I32)
    tile_rows = jnp.sum(cnt, axis=1, keepdims=True)
    table = jnp.concatenate([cnt, seg_start, tile_rows, jnp.zeros((n_tiles, SEG_ALIGN - 1), I32)], axis=1).reshape(-1)
    n_blocks = -(-(t * TOP_K + ne * n_tiles * SEG_ALIGN) // br) + ne
    blk_start = jnp.arange(n_blocks, dtype=I32) * br
    blk_exp = jnp.minimum(jnp.sum(pends[None, :] <= blk_start[:, None], axis=1), ne - 1).astype(I32)
    n_used = (pends[-1:] // br).astype(I32)
    xs = _dispatch(xt, lpt, table, pstarts + tot, padded - tot, n_blocks * br, tt, br)
    y = _experts(xs, blk_exp, n_used, w_gate, w_up, w_down, layer, br)
    return _combine(table, y, lp, gate, xt, sw_gate.astype(BF16), sw_up.astype(BF16), sw_down.astype(BF16),
                    g2, b2, alpha, tt, ne)


def _config(b, s, d):
    return dict(row_tm=min(512, s), conv_tm=min(512, s), conv_cw=min(256, d), proj_rows=min(512, s), attn_out_tm=256,
                moe_rows=1024, moe_tile=min(256, b * s))


def kernel(x, positions, ln1_g, ln1_b, ln2_g, ln2_b, conv_w_in, conv_b_in, conv_w_dw, conv_b_dw, conv_ln_g, conv_ln_b, conv_w_out, conv_b_out, w_kv, attn_w_q, attn_w_o, router_w, router_bias, exp_w_gate, exp_w_up, exp_w_down, sh_w_gate, sh_w_up, sh_w_down):
    b, s, d = x.shape
    t = b * s
    depth = ln1_g.shape[0]
    n_conv = conv_w_in.shape[0]
    alpha = (2.0 * depth) ** 0.25
    cfg = _config(b, s, d)
    n_groups = len(WINDOW_DILATIONS)
    gw = attn_w_o.shape[1]
    vec = lambda a: a.reshape(1, -1)

    xt = x.reshape(t, d)
    kv = None
    tables = None
    for layer in range(depth):
        g1, b1 = vec(ln1_g[layer]), vec(ln1_b[layer])
        if layer < n_conv:
            h = _glu(xt, conv_w_in[layer].astype(BF16), vec(conv_b_in[layer]), cfg["row_tm"])
            c = _dwconv(h, conv_w_dw[layer], vec(conv_b_dw[layer]), b, s, cfg["conv_tm"], cfg["conv_cw"])
            xt = _conv_out(c, xt, vec(conv_ln_g[layer]), vec(conv_ln_b[layer]), conv_w_out[layer].astype(BF16),
                           vec(conv_b_out[layer]), g1, b1, alpha, cfg["row_tm"])
        else:
            j = layer - n_conv
            if tables is None:
                cos, sin = _rope_tables(positions, cfg["row_tm"])
                tables = (cos.reshape(b, s, LANES), sin.reshape(b, s, LANES))
            x3 = xt.reshape(b, s, d)
            wq = attn_w_q[j].astype(BF16)
            outs, lses = [], []
            new_kv = []
            for g, (win, dil) in enumerate(WINDOW_DILATIONS):
                assert win // dil == ATTN_BLOCK
                if kv is None:
                    wkv = w_kv.astype(BF16)
                    q, kg, vg = _project(x3, *tables, [(wq, g), (wkv, g), (wkv, n_groups + g)], (True, True, False),
                                         dil, cfg["proj_rows"], gw)
                    new_kv.append((kg, vg))
                else:
                    (q,) = _project(x3, *tables, [(wq, g)], (True,), dil, cfg["proj_rows"], gw)
                    kg, vg = kv[g]
                o, lse = _attention(q, kg, vg, dil, min(512, s // dil))
                outs.append(o)
                lses.append(lse)
            if kv is None:
                kv = new_kv
            xt = _attn_out(outs, lses, x3, attn_w_o[j].astype(BF16), g1, b1, alpha, cfg["attn_out_tm"]).reshape(t, d)
        xt = _moe_layer(xt, router_w[layer], router_bias[layer], exp_w_gate, exp_w_up, exp_w_down, layer,
                        sh_w_gate[layer], sh_w_up[layer], sh_w_down[layer],
                        vec(ln2_g[layer]), vec(ln2_b[layer]), alpha, cfg)
    return xt.reshape(b, s, d)
```

```python
import functools
import math

import jax
import jax.numpy as jnp
from jax import lax
from jax.experimental import pallas as pl
from jax.experimental.pallas import tpu as pltpu

F32 = jnp.float32
BF16 = jnp.bfloat16
I32 = jnp.int32
U32 = jnp.uint32

LANES = 128
HEAD_DIM = 128
ROT_DIM = HEAD_DIM // 4
ROPE_THETA = 500000.0
ATTN_BLOCK = 128
WINDOW_DILATIONS = ((128, 1), (512, 4), (2048, 16))
NEG_INF = -1e30
TOP_K = 8
ROUTED_SCALE = 2.5
LN_EPS = 1e-5
CONV_HALO = 32
CONV_ROWS = 64
VMEM_LIMIT = 56 * 1024 * 1024
SEG_ALIGN = 8
SORT_CHUNK = 256


def _cparams(sem):
    return pltpu.CompilerParams(dimension_semantics=sem, vmem_limit_bytes=VMEM_LIMIT)


def _ln(y, g, b):
    mu = jnp.mean(y, axis=-1, keepdims=True)
    d = y - mu
    var = jnp.mean(d * d, axis=-1, keepdims=True)
    return d * lax.rsqrt(var + LN_EPS) * g + b


def _full(shape):
    n = len(shape)
    return pl.BlockSpec(shape, lambda *_: (0,) * n)


def _glu_kernel(x_ref, w_ref, b_ref, o_ref):
    d = o_ref.shape[-1]
    h = jnp.dot(x_ref[...].astype(BF16), w_ref[...], preferred_element_type=F32) + b_ref[...]
    o_ref[...] = h[:, :d] * jax.nn.sigmoid(h[:, d:])


def _glu(xt, w_in, b_in, tm):
    t, d = xt.shape
    return pl.pallas_call(
        _glu_kernel,
        grid=(t // tm,),
        in_specs=[pl.BlockSpec((tm, d), lambda i: (i, 0)), _full((d, 2 * d)), _full((1, 2 * d))],
        out_specs=pl.BlockSpec((tm, d), lambda i: (i, 0)),
        out_shape=jax.ShapeDtypeStruct((t, d), F32),
        compiler_params=_cparams(("arbitrary",)),
        name="conv_glu",
    )(xt, w_in, b_in)


def _dwconv_kernel(h_ref, w_ref, b_ref, o_ref, buf, win, *, rows):
    s = pl.program_id(2)
    tm = h_ref.shape[0]
    width = w_ref.shape[0]
    off = CONV_HALO - (width - 1)

    @pl.when(s == 0)
    def _():
        buf[0:CONV_HALO, :] = jnp.zeros((CONV_HALO, buf.shape[1]), F32)

    @pl.when(s > 0)
    def _():
        buf[0:CONV_HALO, :] = buf[tm:tm + CONV_HALO, :]

    buf[CONV_HALO:CONV_HALO + tm, :] = h_ref[...]
    sub = 8
    for c0 in range(0, buf.shape[1], LANES):
        cs = slice(c0, c0 + LANES)
        for r0 in range(0, tm, rows):
            acc = None
            for phase in range(min(sub, width)):
                n_taps = (width - 1 - phase) // sub + 1
                start = r0 + off + phase
                span = rows + sub * (n_taps - 1)
                win[0:span, :] = buf[start:start + span, cs]
                for a in range(n_taps):
                    k = sub * a + phase
                    term = win[sub * a:sub * a + rows, :] * w_ref[k:k + 1, cs]
                    acc = term if acc is None else acc + term
            o_ref[r0:r0 + rows, cs] = acc + b_ref[:, cs]


def _dwconv(h, w_dw, b_dw, b, s, tm, cw):
    t, d = h.shape
    h3 = h.reshape(b, s, d)
    width = w_dw.shape[0]
    out = pl.pallas_call(
        functools.partial(_dwconv_kernel, rows=CONV_ROWS),
        grid=(b, d // cw, s // tm),
        in_specs=[pl.BlockSpec((None, tm, cw), lambda bi, c, si: (bi, si, c)),
                  pl.BlockSpec((width, cw), lambda bi, c, si: (0, c)),
                  pl.BlockSpec((1, cw), lambda bi, c, si: (0, c))],
        out_specs=pl.BlockSpec((None, tm, cw), lambda bi, c, si: (bi, si, c)),
        out_shape=jax.ShapeDtypeStruct((b, s, d), F32),
        scratch_shapes=[pltpu.VMEM((CONV_HALO + tm, cw), F32), pltpu.VMEM((CONV_ROWS + CONV_HALO, LANES), F32)],
        compiler_params=_cparams(("arbitrary", "arbitrary", "arbitrary")),
        name="conv_depthwise",
    )(h3, w_dw, b_dw)
    return out.reshape(t, d)


def _conv_out_kernel(c_ref, x_ref, cg_ref, cb_ref, w_ref, bo_ref, g_ref, b_ref, o_ref, *, alpha):
    u = _ln(c_ref[...], cg_ref[...], cb_ref[...])
    u = u * jax.nn.sigmoid(u)
    mix = jnp.dot(u.astype(BF16), w_ref[...], preferred_element_type=F32) + bo_ref[...]
    o_ref[...] = _ln(alpha * x_ref[...] + mix, g_ref[...], b_ref[...])


def _conv_out(c, xt, cg, cb, w_out, b_out, g1, b1, alpha, tm):
    t, d = xt.shape
    row = pl.BlockSpec((tm, d), lambda i: (i, 0))
    vec = _full((1, d))
    return pl.pallas_call(
        functools.partial(_conv_out_kernel, alpha=alpha),
        grid=(t // tm,),
        in_specs=[row, row, vec, vec, _full((d, d)), vec, vec, vec],
        out_specs=row,
        out_shape=jax.ShapeDtypeStruct((t, d), F32),
        compiler_params=_cparams(("arbitrary",)),
        name="conv_out_ln",
    )(c, xt, cg, cb, w_out, b_out, g1, b1)


def _rope_kernel(pos_ref, invf_ref, c_ref, s_ref):
    ang = pos_ref[...].astype(F32) * invf_ref[...]
    lane = lax.broadcasted_iota(I32, ang.shape, 1)
    half = ROT_DIM // 2
    c_ref[...] = jnp.where(lane < ROT_DIM, jnp.cos(ang), 1.0)
    sn = jnp.sin(ang)
    s_ref[...] = jnp.where(lane < half, -sn, jnp.where(lane < ROT_DIM, sn, 0.0))


def _rope_tables(positions, tm):
    t = positions.size
    half = ROT_DIM // 2
    inv_freq = ROPE_THETA ** (-jnp.arange(half, dtype=F32) * 2.0 / ROT_DIM)
    invf = jnp.zeros((1, LANES), F32).at[0, :half].set(inv_freq).at[0, half:ROT_DIM].set(inv_freq)
    out = jax.ShapeDtypeStruct((t, LANES), F32)
    return pl.pallas_call(
        _rope_kernel,
        grid=(t // tm,),
        in_specs=[pl.BlockSpec((tm, 1), lambda i: (i, 0)), _full((1, LANES))],
        out_specs=[pl.BlockSpec((tm, LANES), lambda i: (i, 0))] * 2,
        out_shape=[out, out],
        compiler_params=_cparams(("arbitrary",)),
        name="rope_tables",
    )(positions.reshape(t, 1), invf)


def _proj_kernel(x_ref, c_ref, s_ref, *rest, rot, heads, dil):
    n_out = len(rot)
    w_refs, o_refs = rest[:n_out], rest[n_out:2 * n_out]
    lhs, tabs, xc = rest[2 * n_out:]
    tm = o_refs[0].shape[1]
    n_lane_tiles = xc.shape[0]
    if dil > 1:
        for c in range(n_lane_tiles):
            xc[c] = x_ref[:, c * LANES:(c + 1) * LANES]
    for r in range(dil):
        rows = slice(r * tm, (r + 1) * tm)
        if dil > 1:
            src = pl.ds(r, tm, stride=dil)
            for c in range(n_lane_tiles):
                lhs[rows, c * LANES:(c + 1) * LANES] = xc[c, src, :].astype(BF16)
        else:
            src = slice(None)
            lhs[rows, :] = x_ref[...].astype(BF16)
        tabs[0, rows, :] = c_ref[src, :]
        tabs[1, rows, :] = s_ref[src, :]
    cos = tabs[0]
    sin = tabs[1]
    lane = lax.broadcasted_iota(I32, cos.shape, 1)
    first = lane < ROT_DIM // 2
    for j, o_ref in enumerate(o_refs):
        y = jnp.dot(lhs[...], w_refs[j][...], preferred_element_type=F32)
        for h in range(heads):
            t = y[:, h * HEAD_DIM:(h + 1) * HEAD_DIM]
            if rot[j]:
                partner = jnp.where(first, pltpu.roll(t, HEAD_DIM - ROT_DIM // 2, 1), pltpu.roll(t, ROT_DIM // 2, 1))
                t = t * cos + partner * sin
            t = t.astype(BF16)
            for r in range(dil):
                o_ref[r, :, h * HEAD_DIM:(h + 1) * HEAD_DIM] = t[r * tm:(r + 1) * tm]


def _project(x3, cos3, sin3, weights, rot, dil, rows, gw):
    b, s, d = x3.shape
    l = s // dil
    tm = rows // dil
    n_out = len(rot)
    out = jax.ShapeDtypeStruct((b, dil, l, gw), BF16)
    tab = pl.BlockSpec((None, rows, LANES), lambda bi, i: (bi, i, 0))
    w_specs = [pl.BlockSpec((d, gw), lambda bi, i, blk=blk: (0, blk)) for _, blk in weights]
    return pl.pallas_call(
        functools.partial(_proj_kernel, rot=rot, heads=gw // HEAD_DIM, dil=dil),
        grid=(b, s // rows),
        in_specs=[pl.BlockSpec((None, rows, d), lambda bi, i: (bi, i, 0)), tab, tab] + w_specs,
        out_specs=[pl.BlockSpec((None, dil, tm, gw), lambda bi, i: (bi, 0, i, 0))] * n_out,
        out_shape=[out] * n_out,
        scratch_shapes=[pltpu.VMEM((rows, d), BF16), pltpu.VMEM((2, rows, LANES), F32),
                        pltpu.VMEM((d // LANES, rows, LANES), F32)],
        compiler_params=_cparams(("arbitrary", "arbitrary")),
        name=f"proj_dil{dil}_n{n_out}",
    )(x3, cos3, sin3, *[w for w, _ in weights])


def _attn_kernel(q_ref, kc_ref, kh_ref, vc_ref, vh_ref, o_ref, lse_ref, kbuf, vbuf, *, heads):
    i = pl.program_id(2)
    tq = q_ref.shape[0]
    blk = ATTN_BLOCK
    kbuf[0:blk, :] = kh_ref[...]
    kbuf[blk:blk + tq, :] = kc_ref[...]
    vbuf[0:blk, :] = vh_ref[...]
    vbuf[blk:blk + tq, :] = vc_ref[...]
    scale = 1.0 / math.sqrt(HEAD_DIM)
    row = lax.broadcasted_iota(I32, (blk, 2 * blk), 0)
    col = lax.broadcasted_iota(I32, (blk, 2 * blk), 1)
    band = jnp.logical_and(col >= row, col <= row + blk)
    first_band = jnp.logical_and(band, jnp.logical_or(col >= blk, i > 0))
    lane = lax.broadcasted_iota(I32, (blk, LANES), 1)
    dims = (((1,), (1,)), ((), ()))
    for n in range(tq // blk):
        rs = slice(n * blk, (n + 1) * blk)
        ks = slice(n * blk, (n + 2) * blk)
        mask = first_band if n == 0 else band
        lse_tile = jnp.zeros((blk, LANES), F32)
        for h in range(heads):
            cs = slice(h * HEAD_DIM, (h + 1) * HEAD_DIM)
            s = lax.dot_general(q_ref[rs, cs], kbuf[ks, cs], dims, preferred_element_type=F32) * scale
            s = jnp.where(mask, s, NEG_INF)
            m = jnp.max(s, axis=1, keepdims=True)
            p = jnp.exp(s - m)
            den = jnp.sum(p, axis=1, keepdims=True)
            o = jnp.dot(p.astype(BF16), vbuf[ks, cs], preferred_element_type=F32)
            o_ref[rs, cs] = (o / den).astype(o_ref.dtype)
            lse_tile = jnp.where(lane == h, m + jnp.log(den), lse_tile)
        lse_ref[rs, :] = lse_tile


def _attention(q, k, v, dil, tq):
    b, _, l, gw = q.shape
    nb = tq // ATTN_BLOCK
    cur = pl.BlockSpec((None, None, tq, gw), lambda bi, r, i: (bi, r, i, 0))
    halo = pl.BlockSpec((None, None, ATTN_BLOCK, gw), lambda bi, r, i: (bi, r, jnp.maximum(i * nb - 1, 0), 0))
    o, lse = pl.pallas_call(
        functools.partial(_attn_kernel, heads=gw // HEAD_DIM),
        grid=(b, dil, l // tq),
        in_specs=[cur, cur, halo, cur, halo],
        out_specs=[pl.BlockSpec((None, None, tq, gw), lambda bi, r, i: (bi, r, i, 0)),
                   pl.BlockSpec((None, None, tq, LANES), lambda bi, r, i: (bi, r, i, 0))],
        out_shape=[jax.ShapeDtypeStruct((b, dil, l, gw), BF16),
                   jax.ShapeDtypeStruct((b, dil, l, LANES), F32)],
        scratch_shapes=[pltpu.VMEM((ATTN_BLOCK + tq, gw), BF16), pltpu.VMEM((ATTN_BLOCK + tq, gw), BF16)],
        compiler_params=_cparams(("arbitrary", "arbitrary", "arbitrary")),
        name=f"attn_dil{dil}",
    )(q, k, k, v, v)
    return o, lse


def _attn_out_kernel(o0_ref, o1_ref, o2_ref, l0_ref, l1_ref, l2_ref, x_ref, w_ref, g_ref, b_ref,
                     out_ref, onat, lnat, mixed, *, alpha, heads, dils):
    tm = x_ref.shape[0]
    for gi, (o_ref, l_ref) in enumerate(((o0_ref, l0_ref), (o1_ref, l1_ref), (o2_ref, l2_ref))):
        dil = dils[gi]
        for r in range(dil):
            dst = pl.ds(r, tm // dil, stride=dil) if dil > 1 else slice(None)
            for h in range(heads):
                onat[gi * heads + h, dst, :] = o_ref[r, :, h * HEAD_DIM:(h + 1) * HEAD_DIM].astype(F32)
            lnat[gi, dst, :] = l_ref[r]
    l0, l1, l2 = lnat[0], lnat[1], lnat[2]
    m = jnp.maximum(jnp.maximum(l0, l1), l2)
    e0, e1, e2 = jnp.exp(l0 - m), jnp.exp(l1 - m), jnp.exp(l2 - m)
    den = e0 + e1 + e2
    w0, w1, w2 = e0 / den, e1 / den, e2 / den
    for h in range(heads):
        cs = slice(h * HEAD_DIM, (h + 1) * HEAD_DIM)
        o = w0[:, h:h + 1] * onat[h] + w1[:, h:h + 1] * onat[heads + h] + w2[:, h:h + 1] * onat[2 * heads + h]
        mixed[:, cs] = o.astype(BF16)
    mix = jnp.dot(mixed[...], w_ref[...], preferred_element_type=F32)
    out_ref[...] = _ln(alpha * x_ref[...] + mix, g_ref[...], b_ref[...])


def _attn_out(outs, lses, x3, w_o, g1, b1, alpha, tm):
    b, s, d = x3.shape
    gw = w_o.shape[0]
    dils = tuple(o.shape[1] for o in outs)
    ospecs = [pl.BlockSpec((None, dil, tm // dil, gw), lambda bi, i: (bi, 0, i, 0)) for dil in dils]
    lspecs = [pl.BlockSpec((None, dil, tm // dil, LANES), lambda bi, i: (bi, 0, i, 0)) for dil in dils]
    xrow = pl.BlockSpec((None, tm, d), lambda bi, i: (bi, i, 0))
    n = len(dils)
    return pl.pallas_call(
        functools.partial(_attn_out_kernel, alpha=alpha, heads=gw // HEAD_DIM, dils=dils),
        grid=(b, s // tm),
        in_specs=ospecs + lspecs + [xrow, _full((gw, d)), _full((1, d)), _full((1, d))],
        out_specs=xrow,
        out_shape=jax.ShapeDtypeStruct((b, s, d), F32),
        scratch_shapes=[pltpu.VMEM((n * gw // HEAD_DIM, tm, HEAD_DIM), F32), pltpu.VMEM((n, tm, LANES), F32),
                        pltpu.VMEM((tm, gw), BF16)],
        compiler_params=_cparams(("arbitrary", "arbitrary")),
        name="attn_out_ln",
    )(*outs, *lses, x3, w_o, g1, b1)


def _pack_bf16_pairs(v, holds_bf16=False):
    half = v.shape[1] // 2
    first, second = v[:, :half], v[:, half:]
    if not holds_bf16:
        first, second = first.astype(BF16).astype(F32), second.astype(BF16).astype(F32)
    return lax.bitcast_convert_type(first, U32) | (lax.bitcast_convert_type(second, U32) >> 16)


def _unpack_bf16_pairs(p):
    first = lax.bitcast_convert_type(p & jnp.uint32(0xFFFF0000), F32)
    second = lax.bitcast_convert_type(p << 16, F32)
    return first, second


def _router_kernel(x_ref, rwt_ref, rb_ref, lp_ref, lpt_ref, gate_ref, cnt_ref, base_ref, tot_ref, carry):
    @pl.when(pl.program_id(0) == 0)
    def _():
        carry[...] = jnp.zeros(carry.shape, F32)

    tm = x_ref.shape[0]
    ne = rwt_ref.shape[0]
    logits = lax.dot_general(rwt_ref[...], x_ref[...], (((1,), (1,)), ((), ())),
                             precision=lax.Precision.HIGHEST, preferred_element_type=F32)
    scores = jax.nn.sigmoid(logits)
    sel = scores + rb_ref[...]
    eidx = lax.broadcasted_iota(I32, (ne, tm), 0)
    chosen = jnp.zeros((ne, tm), F32)
    gsum = jnp.zeros((1, tm), F32)
    picks = []
    for _ in range(TOP_K):
        best = jnp.max(sel, axis=0, keepdims=True)
        j = jnp.min(jnp.where(sel == best, eidx, ne), axis=0, keepdims=True)
        onehot = eidx == j
        sc = jnp.sum(jnp.where(onehot, scores, 0.0), axis=0, keepdims=True)
        sel = jnp.where(onehot, -jnp.inf, sel)
        chosen = chosen + onehot.astype(F32)
        gsum = gsum + sc
        picks.append((onehot, sc))
    r = lax.broadcasted_iota(I32, (tm, tm), 0)
    c = lax.broadcasted_iota(I32, (tm, tm), 1)
    rank = jnp.dot(chosen.astype(BF16), (r < c).astype(BF16), preferred_element_type=F32)
    cnt = jnp.sum(chosen, axis=1, keepdims=True)
    cnt_al = jnp.maximum(jnp.floor((cnt + (SEG_ALIGN - 1)) / SEG_ALIGN), 1.0) * SEG_ALIGN
    er = lax.broadcasted_iota(I32, (ne, ne), 0)
    ec = lax.broadcasted_iota(I32, (ne, ne), 1)
    seg_off = jnp.dot((ec < er).astype(BF16), jnp.broadcast_to(cnt_al, (ne, LANES)).astype(BF16),
                      preferred_element_type=F32)[:, 0:1]
    slot = rank + seg_off
    krow = lax.broadcasted_iota(I32, (TOP_K, tm), 0)
    lpt = jnp.zeros((TOP_K, tm), I32)
    gt = jnp.zeros((TOP_K, tm), F32)
    for k, (onehot, sc) in enumerate(picks):
        row = jnp.sum(jnp.where(onehot, slot, 0.0), axis=0, keepdims=True)
        lpt = jnp.where(krow == k, row.astype(I32), lpt)
        gt = jnp.where(krow == k, sc / gsum * ROUTED_SCALE, gt)
    lpt_ref[...] = lpt
    lp_ref[...] = jnp.concatenate([lpt, jnp.zeros((LANES - TOP_K, tm), I32)], axis=0).T
    gate_ref[...] = jnp.concatenate([gt, jnp.zeros((LANES - TOP_K, tm), F32)], axis=0).T
    cnt_ref[...] = cnt_al
    base_ref[...] = carry[...]
    carry[...] = carry[...] + cnt_al
    tot_ref[...] = carry[...]


def _router(xt, router_w, router_bias, tm):
    t, d = xt.shape
    ne = router_w.shape[1]
    n_tiles = t // tm
    wide = pl.BlockSpec((tm, LANES), lambda i: (i, 0))
    per_tile = pl.BlockSpec((None, ne, 1), lambda i: (i, 0, 0))
    return pl.pallas_call(
        _router_kernel,
        grid=(n_tiles,),
        in_specs=[pl.BlockSpec((tm, d), lambda i: (i, 0)), _full((ne, d)), _full((ne, 1))],
        out_specs=[wide, pl.BlockSpec((None, TOP_K, tm), lambda i: (i, 0, 0)), wide, per_tile, per_tile, _full((ne, 1))],
        out_shape=[jax.ShapeDtypeStruct((t, LANES), I32), jax.ShapeDtypeStruct((n_tiles, TOP_K, tm), I32),
                   jax.ShapeDtypeStruct((t, LANES), F32), jax.ShapeDtypeStruct((n_tiles, ne, 1), F32),
                   jax.ShapeDtypeStruct((n_tiles, ne, 1), F32), jax.ShapeDtypeStruct((ne, 1), F32)],
        scratch_shapes=[pltpu.VMEM((ne, 1), F32)],
        compiler_params=_cparams(("arbitrary",)),
        name="moe_router",
    )(xt, router_w.T, router_bias.reshape(ne, 1))


def _start_segments(tbl, base, ne, make_copy):
    off = 0
    for e in range(ne):
        n = pl.multiple_of(tbl[base + e], SEG_ALIGN)
        g = pl.multiple_of(tbl[base + ne + e], SEG_ALIGN)
        make_copy(off, g, n).start()
        off = pl.multiple_of(off + n, SEG_ALIGN)


def _for_sorted_chunks(chunk, n_chunks, max_chunks):
    for ci in range(max_chunks - 1):
        chunk(ci)

    @pl.when(n_chunks == max_chunks)
    def _():
        chunk(max_chunks - 1)


def _start(cp):
    cp.start()


def _wait(cp):
    cp.wait()


def _dispatch_kernel(fs_ref, fn_ref, tbl, lpt_ref, x_ref, xs_hbm, sorted_buf, zbuf, sem, *, ne):
    i = pl.program_id(0)
    last = pl.num_programs(0) - 1
    slot = lax.rem(i, 2)
    tt = x_ref.shape[0]
    width = _table_width(ne)

    @pl.when(i == 0)
    def _():
        zbuf[...] = jnp.zeros(zbuf.shape, zbuf.dtype)

        def visit(action):
            def body(e, carry):
                n = pl.multiple_of(fn_ref[e], SEG_ALIGN)
                s = pl.multiple_of(fs_ref[e], SEG_ALIGN)

                @pl.when(n > 0)
                def _():
                    action(pltpu.make_async_copy(zbuf.at[pl.ds(0, n), :], xs_hbm.at[pl.ds(s, n), :], sem.at[0]))

                return carry
            return body

        lax.fori_loop(0, ne, visit(_start), 0)
        lax.fori_loop(0, ne, visit(_wait), 0)

    xb = x_ref[...].astype(BF16)
    lpt = lpt_ref[...]
    n_chunks = lax.shift_right_logical(tbl[i * width + 2 * ne] + (SORT_CHUNK - 1), SORT_CHUNK.bit_length() - 1)

    def chunk(ci):
        j0 = ci * SORT_CHUNK
        rows = j0 + lax.broadcasted_iota(I32, (SORT_CHUNK, tt), 0)
        hit = lpt[0:1, :] == rows
        for k in range(1, TOP_K):
            hit = jnp.logical_or(hit, lpt[k:k + 1, :] == rows)
        srt = jnp.dot(jnp.where(hit, 1.0, 0.0).astype(BF16), xb, preferred_element_type=F32)
        sorted_buf[slot, j0:j0 + SORT_CHUNK, :] = _pack_bf16_pairs(srt, holds_bf16=True)

    _for_sorted_chunks(chunk, n_chunks, sorted_buf.shape[1] // SORT_CHUNK)

    def writes(sl):
        return lambda off, g, n: pltpu.make_async_copy(
            sorted_buf.at[sl, pl.ds(off, n), :], xs_hbm.at[pl.ds(g, n), :], sem.at[sl])

    _start_segments(tbl, i * width, ne, writes(slot))

    def wait_all(step, sl):
        tot = pl.multiple_of(tbl[step * width + 2 * ne], SEG_ALIGN)
        pltpu.make_async_copy(sorted_buf.at[sl, pl.ds(0, tot), :], xs_hbm.at[pl.ds(0, tot), :], sem.at[sl]).wait()

    @pl.when(i > 0)
    def _():
        wait_all(i - 1, 1 - slot)

    @pl.when(i == last)
    def _():
        wait_all(i, slot)


def _sorted_rows(tt, ne):
    rows = -(-(tt * TOP_K + ne * SEG_ALIGN) // SORT_CHUNK) * SORT_CHUNK
    assert rows < 2 ** 16
    return rows


def _table_width(ne):
    return 2 * ne + SEG_ALIGN


def _dispatch(xt, lpt, table, fill_start, fill_n, n_rows, tt, br):
    t, d = xt.shape
    ne = fill_start.shape[0]
    grid_spec = pltpu.PrefetchScalarGridSpec(
        num_scalar_prefetch=3,
        grid=(t // tt,),
        in_specs=[pl.BlockSpec((None, TOP_K, tt), lambda i, *_: (i, 0, 0)),
                  pl.BlockSpec((tt, d), lambda i, *_: (i, 0))],
        out_specs=pl.BlockSpec(memory_space=pl.ANY),
        scratch_shapes=[pltpu.VMEM((2, _sorted_rows(tt, ne), d // 2), U32),
                        pltpu.VMEM((br, d // 2), U32), pltpu.SemaphoreType.DMA((2,))],
    )
    return pl.pallas_call(
        functools.partial(_dispatch_kernel, ne=ne),
        grid_spec=grid_spec,
        out_shape=jax.ShapeDtypeStruct((n_rows, d // 2), U32),
        compiler_params=_cparams(("arbitrary",)),
        name="moe_dispatch",
    )(fill_start, fill_n, table, lpt, xt)


def _expert_kernel(be_ref, nu_ref, xs_ref, wg_ref, wu_ref, wd_ref, y_ref, wg_b, wu_b, wd_b):
    i = pl.program_id(0)
    live = i < nu_ref[0]

    @pl.when(jnp.logical_and(live, jnp.logical_or(i == 0, be_ref[i] != be_ref[jnp.maximum(i - 1, 0)])))
    def _():
        wg_b[...] = wg_ref[...].astype(BF16)
        wu_b[...] = wu_ref[...].astype(BF16)
        wd_b[...] = wd_ref[...].astype(BF16)

    @pl.when(live)
    def _():
        first, second = _unpack_bf16_pairs(xs_ref[...])
        xb = jnp.concatenate([first.astype(BF16), second.astype(BF16)], axis=1)
        g = jnp.dot(xb, wg_b[...], preferred_element_type=F32)
        u = jnp.dot(xb, wu_b[...], preferred_element_type=F32)
        h = g * jax.nn.sigmoid(g) * u
        y = jnp.dot(h.astype(BF16), wd_b[...], preferred_element_type=F32)
        y_ref[...] = _pack_bf16_pairs(y)


def _experts(xs, blk_exp, n_used, w_gate, w_up, w_down, layer, br):
    p, dh = xs.shape
    d, f = w_gate.shape[2:]
    n_blocks = p // br

    def row_map(i, be, nu):
        return (jnp.minimum(i, nu[0] - 1), 0)

    def w_map(i, be, nu):
        return (layer, be[jnp.minimum(i, nu[0] - 1)], 0, 0)

    grid_spec = pltpu.PrefetchScalarGridSpec(
        num_scalar_prefetch=2,
        grid=(n_blocks,),
        in_specs=[pl.BlockSpec((br, dh), row_map),
                  pl.BlockSpec((None, None, d, f), w_map),
                  pl.BlockSpec((None, None, d, f), w_map),
                  pl.BlockSpec((None, None, f, d), w_map)],
        out_specs=pl.BlockSpec((br, dh), row_map),
        scratch_shapes=[pltpu.VMEM((d, f), BF16), pltpu.VMEM((d, f), BF16), pltpu.VMEM((f, d), BF16)],
    )
    return pl.pallas_call(
        _expert_kernel,
        grid_spec=grid_spec,
        out_shape=jax.ShapeDtypeStruct((p, dh), U32),
        compiler_params=_cparams(("arbitrary",)),
        name="moe_experts",
    )(blk_exp, n_used, xs, w_gate, w_up, w_down)


def _combine_kernel(tbl, y_hbm, lp_ref, gate_ref, x_ref, sg_ref, su_ref, sd_ref, g_ref, b_ref, o_ref,
                    ysorted, acc, lpb, gb, sem, *, ne, alpha):
    i = pl.program_id(0)
    last = pl.num_programs(0) - 1
    slot = lax.rem(i, 2)
    tt = x_ref.shape[0]
    half = x_ref.shape[1] // 2
    width = _table_width(ne)

    def gathers(sl):
        return lambda off, g, n: pltpu.make_async_copy(
            y_hbm.at[pl.ds(g, n), :], ysorted.at[sl, pl.ds(off, n), :], sem.at[sl])

    @pl.when(i == 0)
    def _():
        _start_segments(tbl, 0, ne, gathers(0))

    @pl.when(i < last)
    def _():
        _start_segments(tbl, (i + 1) * width, ne, gathers(1 - slot))

    x = x_ref[...]
    xb = x.astype(BF16)
    g = jnp.dot(xb, sg_ref[...], preferred_element_type=F32)
    u = jnp.dot(xb, su_ref[...], preferred_element_type=F32)
    h = g * jax.nn.sigmoid(g) * u
    acc[...] = jnp.dot(h.astype(BF16), sd_ref[...], preferred_element_type=F32)
    gate_bits = lax.bitcast_convert_type(gate_ref[...].astype(BF16).astype(F32), U32)
    word = gate_bits | lax.bitcast_convert_type(lp_ref[...], U32)
    for k in range(TOP_K):
        wb = jnp.broadcast_to(word[:, k:k + 1], (tt, LANES))
        lpb[k] = lax.bitcast_convert_type(wb & jnp.uint32(0xFFFF), I32)
        gb[k] = lax.bitcast_convert_type(wb & jnp.uint32(0xFFFF0000), F32)
    total = pl.multiple_of(tbl[i * width + 2 * ne], SEG_ALIGN)
    pltpu.make_async_copy(y_hbm.at[pl.ds(0, total), :], ysorted.at[slot, pl.ds(0, total), :], sem.at[slot]).wait()
    n_chunks = lax.shift_right_logical(total + (SORT_CHUNK - 1), SORT_CHUNK.bit_length() - 1)
    lane = lax.broadcasted_iota(I32, (tt, LANES), 1)

    def chunk(ci):
        j0 = ci * SORT_CHUNK
        parts = []
        for c0 in range(0, SORT_CHUNK, LANES):
            cols = lane + (j0 + c0)
            gm = jnp.zeros((tt, LANES), F32)
            for k in range(TOP_K):
                gm = jnp.where(lpb[k] == cols, gb[k], gm)
            parts.append(gm)
        gmat = jnp.concatenate(parts, axis=1).astype(BF16)
        rows = j0 + lax.broadcasted_iota(I32, (SORT_CHUNK, half), 0)
        ys = jnp.where(rows < total, ysorted[slot, j0:j0 + SORT_CHUNK, :], jnp.uint32(0))
        first, second = _unpack_bf16_pairs(ys)
        first = first.astype(BF16)
        second = second.astype(BF16)
        acc[:, :half] += jnp.dot(gmat, first, preferred_element_type=F32)
        acc[:, half:] += jnp.dot(gmat, second, preferred_element_type=F32)

    _for_sorted_chunks(chunk, n_chunks, ysorted.shape[1] // SORT_CHUNK)
    o_ref[...] = _ln(alpha * x + acc[...], g_ref[...], b_ref[...])


def _combine(table, y, lp, gate, xt, sw_gate, sw_up, sw_down, g2, b2, alpha, tt, ne):
    t, d = xt.shape
    f = sw_gate.shape[1]
    row = pl.BlockSpec((tt, d), lambda i, *_: (i, 0))
    wide = pl.BlockSpec((tt, LANES), lambda i, *_: (i, 0))
    grid_spec = pltpu.PrefetchScalarGridSpec(
        num_scalar_prefetch=1,
        grid=(t // tt,),
        in_specs=[pl.BlockSpec(memory_space=pl.ANY), wide, wide, row,
                  _full((d, f)), _full((d, f)), _full((f, d)), _full((1, d)), _full((1, d))],
        out_specs=row,
        scratch_shapes=[pltpu.VMEM((2, _sorted_rows(tt, ne), d // 2), U32),
                        pltpu.VMEM((tt, d), F32), pltpu.VMEM((TOP_K, tt, LANES), I32), pltpu.VMEM((TOP_K, tt, LANES), F32),
                        pltpu.SemaphoreType.DMA((2,))],
    )
    return pl.pallas_call(
        functools.partial(_combine_kernel, ne=ne, alpha=alpha),
        grid_spec=grid_spec,
        out_shape=jax.ShapeDtypeStruct((t, d), F32),
        compiler_params=_cparams(("arbitrary",)),
        name="moe_combine_ln",
    )(table, y, lp, gate, xt, sw_gate, sw_up, sw_down, g2, b2)


def _moe_layer(xt, router_w, router_bias, w_gate, w_up, w_down, layer, sw_gate, sw_up, sw_down, g2, b2, alpha, cfg):
    t, d = xt.shape
    ne = router_w.shape[1]
    br = cfg["moe_rows"]
    tt = cfg["moe_tile"]
    n_tiles = t // tt
    lp, lpt, gate, cnt, base, tot = _router(xt, router_w, router_bias, tt)
    cnt = cnt.reshape(n_tiles, ne).astype(I32)
    tot = tot.reshape(ne).astype(I32)
    padded = (tot + br - 1) // br * br
    pends = jnp.cumsum(padded)
    pstarts = pends - padded
    seg_start = pstarts[None, :] + base.reshape(n_tiles, ne).astype(I32)
    tile_rows = jnp.sum(cnt, axis=1, keepdims=True)
    table = jnp.concatenate([cnt, seg_start, tile_rows, jnp.zeros((n_tiles, SEG_ALIGN - 1), I32)], axis=1).reshape(-1)
    n_blocks = -(-(t * TOP_K + ne * n_tiles * SEG_ALIGN) // br) + ne
    blk_start = jnp.arange(n_blocks, dtype=I32) * br
    blk_exp = jnp.minimum(jnp.sum(pends[None, :] <= blk_start[:, None], axis=1), ne - 1).astype(I32)
    n_used = (pends[-1:] // br).astype(I32)
    xs = _dispatch(xt, lpt, table, pstarts + tot, padded - tot, n_blocks * br, tt, br)
    y = _experts(xs, blk_exp, n_used, w_gate, w_up, w_down, layer, br)
    return _combine(table, y, lp, gate, xt, sw_gate.astype(BF16), sw_up.astype(BF16), sw_down.astype(BF16),
                    g2, b2, alpha, tt, ne)


def _config(b, s, d):
    return dict(row_tm=min(512, s), conv_tm=min(512, s), conv_cw=min(256, d), proj_rows=min(512, s), attn_out_tm=256,
                moe_rows=1024, moe_tile=min(256, b * s))


def kernel(x, positions, ln1_g, ln1_b, ln2_g, ln2_b, conv_w_in, conv_b_in, conv_w_dw, conv_b_dw, conv_ln_g, conv_ln_b, conv_w_out, conv_b_out, w_kv, attn_w_q, attn_w_o, router_w, router_bias, exp_w_gate, exp_w_up, exp_w_down, sh_w_gate, sh_w_up, sh_w_down):
    b, s, d = x.shape
    t = b * s
    depth = ln1_g.shape[0]
    n_conv = conv_w_in.shape[0]
    alpha = (2.0 * depth) ** 0.25
    cfg = _config(b, s, d)
    n_groups = len(WINDOW_DILATIONS)
    gw = attn_w_o.shape[1]
    vec = lambda a: a.reshape(1, -1)

    xt = x.reshape(t, d)
    kv = None
    tables = None
    for layer in range(depth):
        g1, b1 = vec(ln1_g[layer]), vec(ln1_b[layer])
        if layer < n_conv:
            h = _glu(xt, conv_w_in[layer].astype(BF16), vec(conv_b_in[layer]), cfg["row_tm"])
            c = _dwconv(h, conv_w_dw[layer], vec(conv_b_dw[layer]), b, s, cfg["conv_tm"], cfg["conv_cw"])
            xt = _conv_out(c, xt, vec(conv_ln_g[layer]), vec(conv_ln_b[layer]), conv_w_out[layer].astype(BF16),
                           vec(conv_b_out[layer]), g1, b1, alpha, cfg["row_tm"])
        else:
            j = layer - n_conv
            if tables is None:
                cos, sin = _rope_tables(positions, cfg["row_tm"])
                tables = (cos.reshape(b, s, LANES), sin.reshape(b, s, LANES))
            x3 = xt.reshape(b, s, d)
            wq = attn_w_q[j].astype(BF16)
            outs, lses = [], []
            new_kv = []
            for g, (win, dil) in enumerate(WINDOW_DILATIONS):
                assert win // dil == ATTN_BLOCK
                if kv is None:
                    wkv = w_kv.astype(BF16)
                    q, kg, vg = _project(x3, *tables, [(wq, g), (wkv, g), (wkv, n_groups + g)], (True, True, False),
                                         dil, cfg["proj_rows"], gw)
                    new_kv.append((kg, vg))
                else:
                    (q,) = _project(x3, *tables, [(wq, g)], (True,), dil, cfg["proj_rows"], gw)
                    kg, vg = kv[g]
                o, lse = _attention(q, kg, vg, dil, min(512, s // dil))
                outs.append(o)
                lses.append(lse)
            if kv is None:
                kv = new_kv
            xt = _attn_out(outs, lses, x3, attn_w_o[j].astype(BF16), g1, b1, alpha, cfg["attn_out_tm"]).reshape(t, d)
        xt = _moe_layer(xt, router_w[layer], router_bias[layer], exp_w_gate, exp_w_up, exp_w_down, layer,
                        sh_w_gate[layer], sh_w_up[layer], sh_w_down[layer],
                        vec(ln2_g[layer]), vec(ln2_b[layer]), alpha, cfg)
    return xt.reshape(b, s, d)
```

```python
import functools
import math

import jax
import jax.numpy as jnp
from jax import lax
from jax.experimental import pallas as pl
from jax.experimental.pallas import tpu as pltpu

F32 = jnp.float32
BF16 = jnp.bfloat16
I32 = jnp.int32
U32 = jnp.uint32

LANES = 128
HEAD_DIM = 128
ROT_DIM = HEAD_DIM // 4
ROPE_THETA = 500000.0
ATTN_BLOCK = 128
WINDOW_DILATIONS = ((128, 1), (512, 4), (2048, 16))
NEG_INF = -1e30
TOP_K = 8
ROUTED_SCALE = 2.5
LN_EPS = 1e-5
CONV_HALO = 32
CONV_ROWS = 64
VMEM_LIMIT = 56 * 1024 * 1024
SEG_ALIGN = 8
SORT_CHUNK = 256


def _cparams(sem):
    return pltpu.CompilerParams(dimension_semantics=sem, vmem_limit_bytes=VMEM_LIMIT)


def _ln(y, g, b):
    mu = jnp.mean(y, axis=-1, keepdims=True)
    d = y - mu
    var = jnp.mean(d * d, axis=-1, keepdims=True)
    return d * lax.rsqrt(var + LN_EPS) * g + b


def _full(shape):
    n = len(shape)
    return pl.BlockSpec(shape, lambda *_: (0,) * n)


def _glu_kernel(x_ref, w_ref, b_ref, o_ref):
    d = o_ref.shape[-1]
    h = jnp.dot(x_ref[...].astype(BF16), w_ref[...], preferred_element_type=F32) + b_ref[...]
    o_ref[...] = h[:, :d] * jax.nn.sigmoid(h[:, d:])


def _glu(xt, w_in, b_in, tm):
    t, d = xt.shape
    return pl.pallas_call(
        _glu_kernel,
        grid=(t // tm,),
        in_specs=[pl.BlockSpec((tm, d), lambda i: (i, 0)), _full((d, 2 * d)), _full((1, 2 * d))],
        out_specs=pl.BlockSpec((tm, d), lambda i: (i, 0)),
        out_shape=jax.ShapeDtypeStruct((t, d), F32),
        compiler_params=_cparams(("arbitrary",)),
        name="conv_glu",
    )(xt, w_in, b_in)


def _dwconv_kernel(h_ref, w_ref, b_ref, o_ref, buf, win, *, rows):
    s = pl.program_id(2)
    tm = h_ref.shape[0]
    width = w_ref.shape[0]
    off = CONV_HALO - (width - 1)

    @pl.when(s == 0)
    def _():
        buf[0:CONV_HALO, :] = jnp.zeros((CONV_HALO, buf.shape[1]), F32)

    @pl.when(s > 0)
    def _():
        buf[0:CONV_HALO, :] = buf[tm:tm + CONV_HALO, :]

    buf[CONV_HALO:CONV_HALO + tm, :] = h_ref[...]
    sub = 8
    for c0 in range(0, buf.shape[1], LANES):
        cs = slice(c0, c0 + LANES)
        for r0 in range(0, tm, rows):
            acc = None
            for phase in range(min(sub, width)):
                n_taps = (width - 1 - phase) // sub + 1
                start = r0 + off + phase
                span = rows + sub * (n_taps - 1)
                win[0:span, :] = buf[start:start + span, cs]
                for a in range(n_taps):
                    k = sub * a + phase
                    term = win[sub * a:sub * a + rows, :] * w_ref[k:k + 1, cs]
                    acc = term if acc is None else acc + term
            o_ref[r0:r0 + rows, cs] = acc + b_ref[:, cs]


def _dwconv(h, w_dw, b_dw, b, s, tm, cw):
    t, d = h.shape
    h3 = h.reshape(b, s, d)
    width = w_dw.shape[0]
    out = pl.pallas_call(
        functools.partial(_dwconv_kernel, rows=CONV_ROWS),
        grid=(b, d // cw, s // tm),
        in_specs=[pl.BlockSpec((None, tm, cw), lambda bi, c, si: (bi, si, c)),
                  pl.BlockSpec((width, cw), lambda bi, c, si: (0, c)),
                  pl.BlockSpec((1, cw), lambda bi, c, si: (0, c))],
        out_specs=pl.BlockSpec((None, tm, cw), lambda bi, c, si: (bi, si, c)),
        out_shape=jax.ShapeDtypeStruct((b, s, d), F32),
        scratch_shapes=[pltpu.VMEM((CONV_HALO + tm, cw), F32), pltpu.VMEM((CONV_ROWS + CONV_HALO, LANES), F32)],
        compiler_params=_cparams(("arbitrary", "arbitrary", "arbitrary")),
        name="conv_depthwise",
    )(h3, w_dw, b_dw)
    return out.reshape(t, d)


def _conv_out_kernel(c_ref, x_ref, cg_ref, cb_ref, w_ref, bo_ref, g_ref, b_ref, o_ref, *, alpha):
    u = _ln(c_ref[...], cg_ref[...], cb_ref[...])
    u = u * jax.nn.sigmoid(u)
    mix = jnp.dot(u.astype(BF16), w_ref[...], preferred_element_type=F32) + bo_ref[...]
    o_ref[...] = _ln(alpha * x_ref[...] + mix, g_ref[...], b_ref[...])


def _conv_out(c, xt, cg, cb, w_out, b_out, g1, b1, alpha, tm):
    t, d = xt.shape
    row = pl.BlockSpec((tm, d), lambda i: (i, 0))
    vec = _full((1, d))
    return pl.pallas_call(
        functools.partial(_conv_out_kernel, alpha=alpha),
        grid=(t // tm,),
        in_specs=[row, row, vec, vec, _full((d, d)), vec, vec, vec],
        out_specs=row,
        out_shape=jax.ShapeDtypeStruct((t, d), F32),
        compiler_params=_cparams(("arbitrary",)),
        name="conv_out_ln",
    )(c, xt, cg, cb, w_out, b_out, g1, b1)


def _rope_kernel(pos_ref, invf_ref, c_ref, s_ref):
    ang = pos_ref[...].astype(F32) * invf_ref[...]
    lane = lax.broadcasted_iota(I32, ang.shape, 1)
    half = ROT_DIM // 2
    c_ref[...] = jnp.where(lane < ROT_DIM, jnp.cos(ang), 1.0)
    sn = jnp.sin(ang)
    s_ref[...] = jnp.where(lane < half, -sn, jnp.where(lane < ROT_DIM, sn, 0.0))


def _rope_tables(positions, tm):
    t = positions.size
    half = ROT_DIM // 2
    inv_freq = ROPE_THETA ** (-jnp.arange(half, dtype=F32) * 2.0 / ROT_DIM)
    invf = jnp.zeros((1, LANES), F32).at[0, :half].set(inv_freq).at[0, half:ROT_DIM].set(inv_freq)
    out = jax.ShapeDtypeStruct((t, LANES), F32)
    return pl.pallas_call(
        _rope_kernel,
        grid=(t // tm,),
        in_specs=[pl.BlockSpec((tm, 1), lambda i: (i, 0)), _full((1, LANES))],
        out_specs=[pl.BlockSpec((tm, LANES), lambda i: (i, 0))] * 2,
        out_shape=[out, out],
        compiler_params=_cparams(("arbitrary",)),
        name="rope_tables",
    )(positions.reshape(t, 1), invf)


def _proj_kernel(x_ref, c_ref, s_ref, *rest, rot, heads, dil):
    n_out = len(rot)
    w_refs, o_refs = rest[:n_out], rest[n_out:2 * n_out]
    lhs, tabs, xc = rest[2 * n_out:]
    tm = o_refs[0].shape[1]
    n_lane_tiles = xc.shape[0]
    if dil > 1:
        for c in range(n_lane_tiles):
            xc[c] = x_ref[:, c * LANES:(c + 1) * LANES]
    for r in range(dil):
        rows = slice(r * tm, (r + 1) * tm)
        if dil > 1:
            src = pl.ds(r, tm, stride=dil)
            for c in range(n_lane_tiles):
                lhs[rows, c * LANES:(c + 1) * LANES] = xc[c, src, :].astype(BF16)
        else:
            src = slice(None)
            lhs[rows, :] = x_ref[...].astype(BF16)
        tabs[0, rows, :] = c_ref[src, :]
        tabs[1, rows, :] = s_ref[src, :]
    cos = tabs[0]
    sin = tabs[1]
    lane = lax.broadcasted_iota(I32, cos.shape, 1)
    first = lane < ROT_DIM // 2
    for j, o_ref in enumerate(o_refs):
        y = jnp.dot(lhs[...], w_refs[j][...], preferred_element_type=F32)
        for h in range(heads):
            t = y[:, h * HEAD_DIM:(h + 1) * HEAD_DIM]
            if rot[j]:
                partner = jnp.where(first, pltpu.roll(t, HEAD_DIM - ROT_DIM // 2, 1), pltpu.roll(t, ROT_DIM // 2, 1))
                t = t * cos + partner * sin
            t = t.astype(BF16)
            for r in range(dil):
                o_ref[r, :, h * HEAD_DIM:(h + 1) * HEAD_DIM] = t[r * tm:(r + 1) * tm]


def _project(x3, cos3, sin3, weights, rot, dil, rows, gw):
    b, s, d = x3.shape
    l = s // dil
    tm = rows // dil
    n_out = len(rot)
    out = jax.ShapeDtypeStruct((b, dil, l, gw), BF16)
    tab = pl.BlockSpec((None, rows, LANES), lambda bi, i: (bi, i, 0))
    w_specs = [pl.BlockSpec((d, gw), lambda bi, i, blk=blk: (0, blk)) for _, blk in weights]
    return pl.pallas_call(
        functools.partial(_proj_kernel, rot=rot, heads=gw // HEAD_DIM, dil=dil),
        grid=(b, s // rows),
        in_specs=[pl.BlockSpec((None, rows, d), lambda bi, i: (bi, i, 0)), tab, tab] + w_specs,
        out_specs=[pl.BlockSpec((None, dil, tm, gw), lambda bi, i: (bi, 0, i, 0))] * n_out,
        out_shape=[out] * n_out,
        scratch_shapes=[pltpu.VMEM((rows, d), BF16), pltpu.VMEM((2, rows, LANES), F32),
                        pltpu.VMEM((d // LANES, rows, LANES), F32)],
        compiler_params=_cparams(("arbitrary", "arbitrary")),
        name=f"proj_dil{dil}_n{n_out}",
    )(x3, cos3, sin3, *[w for w, _ in weights])


def _attn_kernel(q_ref, kc_ref, kh_ref, vc_ref, vh_ref, o_ref, lse_ref, kbuf, vbuf, *, heads):
    i = pl.program_id(2)
    tq = q_ref.shape[0]
    blk = ATTN_BLOCK
    kbuf[0:blk, :] = kh_ref[...]
    kbuf[blk:blk + tq, :] = kc_ref[...]
    vbuf[0:blk, :] = vh_ref[...]
    vbuf[blk:blk + tq, :] = vc_ref[...]
    scale = 1.0 / math.sqrt(HEAD_DIM)
    row = lax.broadcasted_iota(I32, (blk, 2 * blk), 0)
    col = lax.broadcasted_iota(I32, (blk, 2 * blk), 1)
    band = jnp.logical_and(col >= row, col <= row + blk)
    first_band = jnp.logical_and(band, jnp.logical_or(col >= blk, i > 0))
    lane = lax.broadcasted_iota(I32, (blk, LANES), 1)
    dims = (((1,), (1,)), ((), ()))
    for n in range(tq // blk):
        rs = slice(n * blk, (n + 1) * blk)
        ks = slice(n * blk, (n + 2) * blk)
        mask = first_band if n == 0 else band
        lse_tile = jnp.zeros((blk, LANES), F32)
        for h in range(heads):
            cs = slice(h * HEAD_DIM, (h + 1) * HEAD_DIM)
            s = lax.dot_general(q_ref[rs, cs], kbuf[ks, cs], dims, preferred_element_type=F32) * scale
            s = jnp.where(mask, s, NEG_INF)
            m = jnp.max(s, axis=1, keepdims=True)
            p = jnp.exp(s - m)
            den = jnp.sum(p, axis=1, keepdims=True)
            o = jnp.dot(p.astype(BF16), vbuf[ks, cs], preferred_element_type=F32)
            o_ref[rs, cs] = (o / den).astype(o_ref.dtype)
            lse_tile = jnp.where(lane == h, m + jnp.log(den), lse_tile)
        lse_ref[rs, :] = lse_tile


def _attention(q, k, v, dil, tq):
    b, _, l, gw = q.shape
    nb = tq // ATTN_BLOCK
    cur = pl.BlockSpec((None, None, tq, gw), lambda bi, r, i: (bi, r, i, 0))
    halo = pl.BlockSpec((None, None, ATTN_BLOCK, gw), lambda bi, r, i: (bi, r, jnp.maximum(i * nb - 1, 0), 0))
    o, lse = pl.pallas_call(
        functools.partial(_attn_kernel, heads=gw // HEAD_DIM),
        grid=(b, dil, l // tq),
        in_specs=[cur, cur, halo, cur, halo],
        out_specs=[pl.BlockSpec((None, None, tq, gw), lambda bi, r, i: (bi, r, i, 0)),
                   pl.BlockSpec((None, None, tq, LANES), lambda bi, r, i: (bi, r, i, 0))],
        out_shape=[jax.ShapeDtypeStruct((b, dil, l, gw), BF16),
                   jax.ShapeDtypeStruct((b, dil, l, LANES), F32)],
        scratch_shapes=[pltpu.VMEM((ATTN_BLOCK + tq, gw), BF16), pltpu.VMEM((ATTN_BLOCK + tq, gw), BF16)],
        compiler_params=_cparams(("arbitrary", "arbitrary", "arbitrary")),
        name=f"attn_dil{dil}",
    )(q, k, k, v, v)
    return o, lse


def _attn_out_kernel(o0_ref, o1_ref, o2_ref, l0_ref, l1_ref, l2_ref, x_ref, w_ref, g_ref, b_ref,
                     out_ref, onat, lnat, mixed, *, alpha, heads, dils):
    tm = x_ref.shape[0]
    for gi, (o_ref, l_ref) in enumerate(((o0_ref, l0_ref), (o1_ref, l1_ref), (o2_ref, l2_ref))):
        dil = dils[gi]
        for r in range(dil):
            dst = pl.ds(r, tm // dil, stride=dil) if dil > 1 else slice(None)
            for h in range(heads):
                onat[gi * heads + h, dst, :] = o_ref[r, :, h * HEAD_DIM:(h + 1) * HEAD_DIM].astype(F32)
            lnat[gi, dst, :] = l_ref[r]
    l0, l1, l2 = lnat[0], lnat[1], lnat[2]
    m = jnp.maximum(jnp.maximum(l0, l1), l2)
    e0, e1, e2 = jnp.exp(l0 - m), jnp.exp(l1 - m), jnp.exp(l2 - m)
    den = e0 + e1 + e2
    w0, w1, w2 = e0 / den, e1 / den, e2 / den
    for h in range(heads):
        cs = slice(h * HEAD_DIM, (h + 1) * HEAD_DIM)
        o = w0[:, h:h + 1] * onat[h] + w1[:, h:h + 1] * onat[heads + h] + w2[:, h:h + 1] * onat[2 * heads + h]
        mixed[:, cs] = o.astype(BF16)
    mix = jnp.dot(mixed[...], w_ref[...], preferred_element_type=F32)
    out_ref[...] = _ln(alpha * x_ref[...] + mix, g_ref[...], b_ref[...])


def _attn_out(outs, lses, x3, w_o, g1, b1, alpha, tm):
    b, s, d = x3.shape
    gw = w_o.shape[0]
    dils = tuple(o.shape[1] for o in outs)
    ospecs = [pl.BlockSpec((None, dil, tm // dil, gw), lambda bi, i: (bi, 0, i, 0)) for dil in dils]
    lspecs = [pl.BlockSpec((None, dil, tm // dil, LANES), lambda bi, i: (bi, 0, i, 0)) for dil in dils]
    xrow = pl.BlockSpec((None, tm, d), lambda bi, i: (bi, i, 0))
    n = len(dils)
    return pl.pallas_call(
        functools.partial(_attn_out_kernel, alpha=alpha, heads=gw // HEAD_DIM, dils=dils),
        grid=(b, s // tm),
        in_specs=ospecs + lspecs + [xrow, _full((gw, d)), _full((1, d)), _full((1, d))],
        out_specs=xrow,
        out_shape=jax.ShapeDtypeStruct((b, s, d), F32),
        scratch_shapes=[pltpu.VMEM((n * gw // HEAD_DIM, tm, HEAD_DIM), F32), pltpu.VMEM((n, tm, LANES), F32),
                        pltpu.VMEM((tm, gw), BF16)],
        compiler_params=_cparams(("arbitrary", "arbitrary")),
        name="attn_out_ln",
    )(*outs, *lses, x3, w_o, g1, b1)


def _pack_bf16_pairs(v, holds_bf16=False):
    half = v.shape[1] // 2
    first, second = v[:, :half], v[:, half:]
    if not holds_bf16:
        first, second = first.astype(BF16).astype(F32), second.astype(BF16).astype(F32)
    return lax.bitcast_convert_type(first, U32) | (lax.bitcast_convert_type(second, U32) >> 16)


def _unpack_bf16_pairs(p):
    first = lax.bitcast_convert_type(p & jnp.uint32(0xFFFF0000), F32)
    second = lax.bitcast_convert_type(p << 16, F32)
    return first, second


def _router_kernel(x_ref, rwt_ref, rb_ref, lp_ref, lpt_ref, gate_ref, cnt_ref, base_ref, tot_ref, carry):
    @pl.when(pl.program_id(0) == 0)
    def _():
        carry[...] = jnp.zeros(carry.shape, F32)

    tm = x_ref.shape[0]
    ne = rwt_ref.shape[0]
    dims = (((1,), (1,)), ((), ()))
    w = rwt_ref[...]
    x = x_ref[...]
    w_hi = w.astype(BF16)
    w_lo = (w - w_hi.astype(F32)).astype(BF16)
    x_hi = x.astype(BF16)
    x_lo = (x - x_hi.astype(F32)).astype(BF16)
    logits = (lax.dot_general(w_hi, x_hi, dims, preferred_element_type=F32)
              + (lax.dot_general(w_lo, x_hi, dims, preferred_element_type=F32)
                 + lax.dot_general(w_hi, x_lo, dims, preferred_element_type=F32)))
    scores = jax.nn.sigmoid(logits)
    sel = scores + rb_ref[...]
    eidx = lax.broadcasted_iota(I32, (ne, tm), 0)
    chosen = jnp.zeros((ne, tm), F32)
    gsum = jnp.zeros((1, tm), F32)
    picks = []
    for _ in range(TOP_K):
        best = jnp.max(sel, axis=0, keepdims=True)
        j = jnp.min(jnp.where(sel == best, eidx, ne), axis=0, keepdims=True)
        onehot = eidx == j
        sc = jnp.sum(jnp.where(onehot, scores, 0.0), axis=0, keepdims=True)
        sel = jnp.where(onehot, -jnp.inf, sel)
        chosen = chosen + onehot.astype(F32)
        gsum = gsum + sc
        picks.append((onehot, sc))
    r = lax.broadcasted_iota(I32, (tm, tm), 0)
    c = lax.broadcasted_iota(I32, (tm, tm), 1)
    rank = jnp.dot(chosen.astype(BF16), (r < c).astype(BF16), preferred_element_type=F32)
    cnt = jnp.sum(chosen, axis=1, keepdims=True)
    cnt_al = jnp.maximum(jnp.floor((cnt + (SEG_ALIGN - 1)) / SEG_ALIGN), 1.0) * SEG_ALIGN
    er = lax.broadcasted_iota(I32, (ne, ne), 0)
    ec = lax.broadcasted_iota(I32, (ne, ne), 1)
    seg_off = jnp.dot((ec < er).astype(BF16), jnp.broadcast_to(cnt_al, (ne, LANES)).astype(BF16),
                      preferred_element_type=F32)[:, 0:1]
    slot = rank + seg_off
    krow = lax.broadcasted_iota(I32, (TOP_K, tm), 0)
    lpt = jnp.zeros((TOP_K, tm), I32)
    gt = jnp.zeros((TOP_K, tm), F32)
    for k, (onehot, sc) in enumerate(picks):
        row = jnp.sum(jnp.where(onehot, slot, 0.0), axis=0, keepdims=True)
        lpt = jnp.where(krow == k, row.astype(I32), lpt)
        gt = jnp.where(krow == k, sc / gsum * ROUTED_SCALE, gt)
    lpt_ref[...] = lpt
    lp_ref[...] = jnp.concatenate([lpt, jnp.zeros((LANES - TOP_K, tm), I32)], axis=0).T
    gate_ref[...] = jnp.concatenate([gt, jnp.zeros((LANES - TOP_K, tm), F32)], axis=0).T
    cnt_ref[...] = cnt_al
    base_ref[...] = carry[...]
    carry[...] = carry[...] + cnt_al
    tot_ref[...] = carry[...]


def _router(xt, router_w, router_bias, tm):
    t, d = xt.shape
    ne = router_w.shape[1]
    n_tiles = t // tm
    wide = pl.BlockSpec((tm, LANES), lambda i: (i, 0))
    per_tile = pl.BlockSpec((None, ne, 1), lambda i: (i, 0, 0))
    return pl.pallas_call(
        _router_kernel,
        grid=(n_tiles,),
        in_specs=[pl.BlockSpec((tm, d), lambda i: (i, 0)), _full((ne, d)), _full((ne, 1))],
        out_specs=[wide, pl.BlockSpec((None, TOP_K, tm), lambda i: (i, 0, 0)), wide, per_tile, per_tile, _full((ne, 1))],
        out_shape=[jax.ShapeDtypeStruct((t, LANES), I32), jax.ShapeDtypeStruct((n_tiles, TOP_K, tm), I32),
                   jax.ShapeDtypeStruct((t, LANES), F32), jax.ShapeDtypeStruct((n_tiles, ne, 1), F32),
                   jax.ShapeDtypeStruct((n_tiles, ne, 1), F32), jax.ShapeDtypeStruct((ne, 1), F32)],
        scratch_shapes=[pltpu.VMEM((ne, 1), F32)],
        compiler_params=_cparams(("arbitrary",)),
        name="moe_router",
    )(xt, router_w.T, router_bias.reshape(ne, 1))


def _start_segments(tbl, base, ne, make_copy):
    off = 0
    for e in range(ne):
        n = pl.multiple_of(tbl[base + e], SEG_ALIGN)
        g = pl.multiple_of(tbl[base + ne + e], SEG_ALIGN)
        make_copy(off, g, n).start()
        off = pl.multiple_of(off + n, SEG_ALIGN)


def _for_sorted_chunks(chunk, n_chunks, max_chunks):
    for ci in range(max_chunks - 1):
        chunk(ci)

    @pl.when(n_chunks == max_chunks)
    def _():
        chunk(max_chunks - 1)


def _start(cp):
    cp.start()


def _wait(cp):
    cp.wait()


def _dispatch_kernel(fs_ref, fn_ref, tbl, lpt_ref, x_ref, xs_hbm, sorted_buf, zbuf, sem, *, ne):
    i = pl.program_id(0)
    last = pl.num_programs(0) - 1
    slot = lax.rem(i, 2)
    tt = x_ref.shape[0]
    width = _table_width(ne)

    @pl.when(i == 0)
    def _():
        zbuf[...] = jnp.zeros(zbuf.shape, zbuf.dtype)

        def visit(action):
            def body(e, carry):
                n = pl.multiple_of(fn_ref[e], SEG_ALIGN)
                s = pl.multiple_of(fs_ref[e], SEG_ALIGN)

                @pl.when(n > 0)
                def _():
                    action(pltpu.make_async_copy(zbuf.at[pl.ds(0, n), :], xs_hbm.at[pl.ds(s, n), :], sem.at[0]))

                return carry
            return body

        lax.fori_loop(0, ne, visit(_start), 0)
        lax.fori_loop(0, ne, visit(_wait), 0)

    xb = x_ref[...].astype(BF16)
    lpt = lpt_ref[...]
    n_chunks = lax.shift_right_logical(tbl[i * width + 2 * ne] + (SORT_CHUNK - 1), SORT_CHUNK.bit_length() - 1)

    def chunk(ci):
        j0 = ci * SORT_CHUNK
        rows = j0 + lax.broadcasted_iota(I32, (SORT_CHUNK, tt), 0)
        hit = lpt[0:1, :] == rows
        for k in range(1, TOP_K):
            hit = jnp.logical_or(hit, lpt[k:k + 1, :] == rows)
        srt = jnp.dot(jnp.where(hit, 1.0, 0.0).astype(BF16), xb, preferred_element_type=F32)
        sorted_buf[slot, j0:j0 + SORT_CHUNK, :] = _pack_bf16_pairs(srt, holds_bf16=True)

    _for_sorted_chunks(chunk, n_chunks, sorted_buf.shape[1] // SORT_CHUNK)

    def writes(sl):
        return lambda off, g, n: pltpu.make_async_copy(
            sorted_buf.at[sl, pl.ds(off, n), :], xs_hbm.at[pl.ds(g, n), :], sem.at[sl])

    _start_segments(tbl, i * width, ne, writes(slot))

    def wait_all(step, sl):
        tot = pl.multiple_of(tbl[step * width + 2 * ne], SEG_ALIGN)
        pltpu.make_async_copy(sorted_buf.at[sl, pl.ds(0, tot), :], xs_hbm.at[pl.ds(0, tot), :], sem.at[sl]).wait()

    @pl.when(i > 0)
    def _():
        wait_all(i - 1, 1 - slot)

    @pl.when(i == last)
    def _():
        wait_all(i, slot)


def _sorted_rows(tt, ne):
    rows = -(-(tt * TOP_K + ne * SEG_ALIGN) // SORT_CHUNK) * SORT_CHUNK
    assert rows < 2 ** 16
    return rows


def _table_width(ne):
    return 2 * ne + SEG_ALIGN


def _dispatch(xt, lpt, table, fill_start, fill_n, n_rows, tt, br):
    t, d = xt.shape
    ne = fill_start.shape[0]
    grid_spec = pltpu.PrefetchScalarGridSpec(
        num_scalar_prefetch=3,
        grid=(t // tt,),
        in_specs=[pl.BlockSpec((None, TOP_K, tt), lambda i, *_: (i, 0, 0)),
                  pl.BlockSpec((tt, d), lambda i, *_: (i, 0))],
        out_specs=pl.BlockSpec(memory_space=pl.ANY),
        scratch_shapes=[pltpu.VMEM((2, _sorted_rows(tt, ne), d // 2), U32),
                        pltpu.VMEM((br, d // 2), U32), pltpu.SemaphoreType.DMA((2,))],
    )
    return pl.pallas_call(
        functools.partial(_dispatch_kernel, ne=ne),
        grid_spec=grid_spec,
        out_shape=jax.ShapeDtypeStruct((n_rows, d // 2), U32),
        compiler_params=_cparams(("arbitrary",)),
        name="moe_dispatch",
    )(fill_start, fill_n, table, lpt, xt)


def _expert_kernel(be_ref, nu_ref, xs_ref, wg_ref, wu_ref, wd_ref, y_ref, wg_b, wu_b, wd_b):
    i = pl.program_id(0)
    live = i < nu_ref[0]

    @pl.when(jnp.logical_and(live, jnp.logical_or(i == 0, be_ref[i] != be_ref[jnp.maximum(i - 1, 0)])))
    def _():
        wg_b[...] = wg_ref[...].astype(BF16)
        wu_b[...] = wu_ref[...].astype(BF16)
        wd_b[...] = wd_ref[...].astype(BF16)

    @pl.when(live)
    def _():
        first, second = _unpack_bf16_pairs(xs_ref[...])
        xb = jnp.concatenate([first.astype(BF16), second.astype(BF16)], axis=1)
        g = jnp.dot(xb, wg_b[...], preferred_element_type=F32)
        u = jnp.dot(xb, wu_b[...], preferred_element_type=F32)
        h = g * jax.nn.sigmoid(g) * u
        y = jnp.dot(h.astype(BF16), wd_b[...], preferred_element_type=F32)
        y_ref[...] = _pack_bf16_pairs(y)


def _experts(xs, blk_exp, n_used, w_gate, w_up, w_down, layer, br):
    p, dh = xs.shape
    d, f = w_gate.shape[2:]
    n_blocks = p // br

    def row_map(i, be, nu):
        return (jnp.minimum(i, nu[0] - 1), 0)

    def w_map(i, be, nu):
        return (layer, be[jnp.minimum(i, nu[0] - 1)], 0, 0)

    grid_spec = pltpu.PrefetchScalarGridSpec(
        num_scalar_prefetch=2,
        grid=(n_blocks,),
        in_specs=[pl.BlockSpec((br, dh), row_map),
                  pl.BlockSpec((None, None, d, f), w_map),
                  pl.BlockSpec((None, None, d, f), w_map),
                  pl.BlockSpec((None, None, f, d), w_map)],
        out_specs=pl.BlockSpec((br, dh), row_map),
        scratch_shapes=[pltpu.VMEM((d, f), BF16), pltpu.VMEM((d, f), BF16), pltpu.VMEM((f, d), BF16)],
    )
    return pl.pallas_call(
        _expert_kernel,
        grid_spec=grid_spec,
        out_shape=jax.ShapeDtypeStruct((p, dh), U32),
        compiler_params=_cparams(("arbitrary",)),
        name="moe_experts",
    )(blk_exp, n_used, xs, w_gate, w_up, w_down)


def _combine_kernel(tbl, y_hbm, lp_ref, gate_ref, x_ref, sg_ref, su_ref, sd_ref, g_ref, b_ref, o_ref,
                    ysorted, acc, lpb, gb, sem, *, ne, alpha):
    i = pl.program_id(0)
    last = pl.num_programs(0) - 1
    slot = lax.rem(i, 2)
    tt = x_ref.shape[0]
    half = x_ref.shape[1] // 2
    width = _table_width(ne)

    def gathers(sl):
        return lambda off, g, n: pltpu.make_async_copy(
            y_hbm.at[pl.ds(g, n), :], ysorted.at[sl, pl.ds(off, n), :], sem.at[sl])

    @pl.when(i == 0)
    def _():
        _start_segments(tbl, 0, ne, gathers(0))

    @pl.when(i < last)
    def _():
        _start_segments(tbl, (i + 1) * width, ne, gathers(1 - slot))

    x = x_ref[...]
    xb = x.astype(BF16)
    g = jnp.dot(xb, sg_ref[...], preferred_element_type=F32)
    u = jnp.dot(xb, su_ref[...], preferred_element_type=F32)
    h = g * jax.nn.sigmoid(g) * u
    acc[...] = jnp.dot(h.astype(BF16), sd_ref[...], preferred_element_type=F32)
    gate_bits = lax.bitcast_convert_type(gate_ref[...].astype(BF16).astype(F32), U32)
    word = gate_bits | lax.bitcast_convert_type(lp_ref[...], U32)
    for k in range(TOP_K):
        wb = jnp.broadcast_to(word[:, k:k + 1], (tt, LANES))
        lpb[k] = lax.bitcast_convert_type(wb & jnp.uint32(0xFFFF), I32)
        gb[k] = lax.bitcast_convert_type(wb & jnp.uint32(0xFFFF0000), F32)
    total = pl.multiple_of(tbl[i * width + 2 * ne], SEG_ALIGN)
    pltpu.make_async_copy(y_hbm.at[pl.ds(0, total), :], ysorted.at[slot, pl.ds(0, total), :], sem.at[slot]).wait()
    n_chunks = lax.shift_right_logical(total + (SORT_CHUNK - 1), SORT_CHUNK.bit_length() - 1)
    lane = lax.broadcasted_iota(I32, (tt, LANES), 1)

    def chunk(ci):
        j0 = ci * SORT_CHUNK
        parts = []
        for c0 in range(0, SORT_CHUNK, LANES):
            cols = lane + (j0 + c0)
            gm = jnp.zeros((tt, LANES), F32)
            for k in range(TOP_K):
                gm = jnp.where(lpb[k] == cols, gb[k], gm)
            parts.append(gm)
        gmat = jnp.concatenate(parts, axis=1).astype(BF16)
        rows = j0 + lax.broadcasted_iota(I32, (SORT_CHUNK, half), 0)
        ys = jnp.where(rows < total, ysorted[slot, j0:j0 + SORT_CHUNK, :], jnp.uint32(0))
        first, second = _unpack_bf16_pairs(ys)
        first = first.astype(BF16)
        second = second.astype(BF16)
        acc[:, :half] += jnp.dot(gmat, first, preferred_element_type=F32)
        acc[:, half:] += jnp.dot(gmat, second, preferred_element_type=F32)

    _for_sorted_chunks(chunk, n_chunks, ysorted.shape[1] // SORT_CHUNK)
    o_ref[...] = _ln(alpha * x + acc[...], g_ref[...], b_ref[...])


def _combine(table, y, lp, gate, xt, sw_gate, sw_up, sw_down, g2, b2, alpha, tt, ne):
    t, d = xt.shape
    f = sw_gate.shape[1]
    row = pl.BlockSpec((tt, d), lambda i, *_: (i, 0))
    wide = pl.BlockSpec((tt, LANES), lambda i, *_: (i, 0))
    grid_spec = pltpu.PrefetchScalarGridSpec(
        num_scalar_prefetch=1,
        grid=(t // tt,),
        in_specs=[pl.BlockSpec(memory_space=pl.ANY), wide, wide, row,
                  _full((d, f)), _full((d, f)), _full((f, d)), _full((1, d)), _full((1, d))],
        out_specs=row,
        scratch_shapes=[pltpu.VMEM((2, _sorted_rows(tt, ne), d // 2), U32),
                        pltpu.VMEM((tt, d), F32), pltpu.VMEM((TOP_K, tt, LANES), I32), pltpu.VMEM((TOP_K, tt, LANES), F32),
                        pltpu.SemaphoreType.DMA((2,))],
    )
    return pl.pallas_call(
        functools.partial(_combine_kernel, ne=ne, alpha=alpha),
        grid_spec=grid_spec,
        out_shape=jax.ShapeDtypeStruct((t, d), F32),
        compiler_params=_cparams(("arbitrary",)),
        name="moe_combine_ln",
    )(table, y, lp, gate, xt, sw_gate, sw_up, sw_down, g2, b2)


def _moe_layer(xt, router_w, router_bias, w_gate, w_up, w_down, layer, sw_gate, sw_up, sw_down, g2, b2, alpha, cfg):
    t, d = xt.shape
    ne = router_w.shape[1]
    br = cfg["moe_rows"]
    tt = cfg["moe_tile"]
    n_tiles = t // tt
    lp, lpt, gate, cnt, base, tot = _router(xt, router_w, router_bias, tt)
    cnt = cnt.reshape(n_tiles, ne).astype(I32)
    tot = tot.reshape(ne).astype(I32)
    padded = (tot + br - 1) // br * br
    pends = jnp.cumsum(padded)
    pstarts = pends - padded
    seg_start = pstarts[None, :] + base.reshape(n_tiles, ne).astype(I32)
    tile_rows = jnp.sum(cnt, axis=1, keepdims=True)
    table = jnp.concatenate([cnt, seg_start, tile_rows, jnp.zeros((n_tiles, SEG_ALIGN - 1), I32)], axis=1).reshape(-1)
    n_blocks = -(-(t * TOP_K + ne * n_tiles * SEG_ALIGN) // br) + ne
    blk_start = jnp.arange(n_blocks, dtype=I32) * br
    blk_exp = jnp.minimum(jnp.sum(pends[None, :] <= blk_start[:, None], axis=1), ne - 1).astype(I32)
    n_used = (pends[-1:] // br).astype(I32)
    xs = _dispatch(xt, lpt, table, pstarts + tot, padded - tot, n_blocks * br, tt, br)
    y = _experts(xs, blk_exp, n_used, w_gate, w_up, w_down, layer, br)
    return _combine(table, y, lp, gate, xt, sw_gate.astype(BF16), sw_up.astype(BF16), sw_down.astype(BF16),
                    g2, b2, alpha, tt, ne)


def _config(b, s, d):
    return dict(row_tm=min(512, s), conv_tm=min(1024, s), conv_cw=min(256, d), proj_rows=min(512, s), attn_out_tm=512,
                moe_rows=1024, moe_tile=min(256, b * s))


def kernel(x, positions, ln1_g, ln1_b, ln2_g, ln2_b, conv_w_in, conv_b_in, conv_w_dw, conv_b_dw, conv_ln_g, conv_ln_b, conv_w_out, conv_b_out, w_kv, attn_w_q, attn_w_o, router_w, router_bias, exp_w_gate, exp_w_up, exp_w_down, sh_w_gate, sh_w_up, sh_w_down):
    b, s, d = x.shape
    t = b * s
    depth = ln1_g.shape[0]
    n_conv = conv_w_in.shape[0]
    alpha = (2.0 * depth) ** 0.25
    cfg = _config(b, s, d)
    n_groups = len(WINDOW_DILATIONS)
    gw = attn_w_o.shape[1]
    vec = lambda a: a.reshape(1, -1)

    xt = x.reshape(t, d)
    kv = None
    tables = None
    for layer in range(depth):
        g1, b1 = vec(ln1_g[layer]), vec(ln1_b[layer])
        if layer < n_conv:
            h = _glu(xt, conv_w_in[layer].astype(BF16), vec(conv_b_in[layer]), cfg["row_tm"])
            c = _dwconv(h, conv_w_dw[layer], vec(conv_b_dw[layer]), b, s, cfg["conv_tm"], cfg["conv_cw"])
            xt = _conv_out(c, xt, vec(conv_ln_g[layer]), vec(conv_ln_b[layer]), conv_w_out[layer].astype(BF16),
                           vec(conv_b_out[layer]), g1, b1, alpha, cfg["row_tm"])
        else:
            j = layer - n_conv
            if tables is None:
                cos, sin = _rope_tables(positions, cfg["row_tm"])
                tables = (cos.reshape(b, s, LANES), sin.reshape(b, s, LANES))
            x3 = xt.reshape(b, s, d)
            wq = attn_w_q[j].astype(BF16)
            outs, lses = [], []
            new_kv = []
            for g, (win, dil) in enumerate(WINDOW_DILATIONS):
                assert win // dil == ATTN_BLOCK
                if kv is None:
                    wkv = w_kv.astype(BF16)
                    q, kg, vg = _project(x3, *tables, [(wq, g), (wkv, g), (wkv, n_groups + g)], (True, True, False),
                                         dil, cfg["proj_rows"], gw)
                    new_kv.append((kg, vg))
                else:
                    (q,) = _project(x3, *tables, [(wq, g)], (True,), dil, cfg["proj_rows"], gw)
                    kg, vg = kv[g]
                o, lse = _attention(q, kg, vg, dil, min(512, s // dil))
                outs.append(o)
                lses.append(lse)
            if kv is None:
                kv = new_kv
            xt = _attn_out(outs, lses, x3, attn_w_o[j].astype(BF16), g1, b1, alpha, cfg["attn_out_tm"]).reshape(t, d)
        xt = _moe_layer(xt, router_w[layer], router_bias[layer], exp_w_gate, exp_w_up, exp_w_down, layer,
                        sh_w_gate[layer], sh_w_up[layer], sh_w_down[layer],
                        vec(ln2_g[layer]), vec(ln2_b[layer]), alpha, cfg)
    return xt.reshape(b, s, d)
```

```python
import functools
import math

import jax
import jax.numpy as jnp
from jax import lax
from jax.experimental import pallas as pl
from jax.experimental.pallas import tpu as pltpu

F32 = jnp.float32
BF16 = jnp.bfloat16
I32 = jnp.int32
U32 = jnp.uint32

LANES = 128
HEAD_DIM = 128
ROT_DIM = HEAD_DIM // 4
ROPE_THETA = 500000.0
ATTN_BLOCK = 128
WINDOW_DILATIONS = ((128, 1), (512, 4), (2048, 16))
NEG_INF = -1e30
TOP_K = 8
ROUTED_SCALE = 2.5
LN_EPS = 1e-5
CONV_HALO = 32
CONV_ROWS = 64
VMEM_LIMIT = 56 * 1024 * 1024
SEG_ALIGN = 8
SORT_CHUNK = 256


def _cparams(sem):
    return pltpu.CompilerParams(dimension_semantics=sem, vmem_limit_bytes=VMEM_LIMIT)


def _ln(y, g, b):
    mu = jnp.mean(y, axis=-1, keepdims=True)
    d = y - mu
    var = jnp.mean(d * d, axis=-1, keepdims=True)
    return d * lax.rsqrt(var + LN_EPS) * g + b


def _full(shape):
    n = len(shape)
    return pl.BlockSpec(shape, lambda *_: (0,) * n)


def _glu_kernel(x_ref, w_ref, b_ref, o_ref):
    d = o_ref.shape[-1]
    h = jnp.dot(x_ref[...].astype(BF16), w_ref[...], preferred_element_type=F32) + b_ref[...]
    o_ref[...] = h[:, :d] * jax.nn.sigmoid(h[:, d:])


def _glu(xt, w_in, b_in, tm):
    t, d = xt.shape
    return pl.pallas_call(
        _glu_kernel,
        grid=(t // tm,),
        in_specs=[pl.BlockSpec((tm, d), lambda i: (i, 0)), _full((d, 2 * d)), _full((1, 2 * d))],
        out_specs=pl.BlockSpec((tm, d), lambda i: (i, 0)),
        out_shape=jax.ShapeDtypeStruct((t, d), F32),
        compiler_params=_cparams(("arbitrary",)),
        name="conv_glu",
    )(xt, w_in, b_in)


def _dwconv_kernel(h_ref, w_ref, b_ref, o_ref, buf, win, *, rows):
    s = pl.program_id(2)
    tm = h_ref.shape[0]
    width = w_ref.shape[0]
    off = CONV_HALO - (width - 1)

    @pl.when(s == 0)
    def _():
        buf[0:CONV_HALO, :] = jnp.zeros((CONV_HALO, buf.shape[1]), F32)

    @pl.when(s > 0)
    def _():
        buf[0:CONV_HALO, :] = buf[tm:tm + CONV_HALO, :]

    buf[CONV_HALO:CONV_HALO + tm, :] = h_ref[...]
    sub = 8
    for c0 in range(0, buf.shape[1], LANES):
        cs = slice(c0, c0 + LANES)
        for r0 in range(0, tm, rows):
            acc = None
            for phase in range(min(sub, width)):
                n_taps = (width - 1 - phase) // sub + 1
                start = r0 + off + phase
                span = rows + sub * (n_taps - 1)
                win[0:span, :] = buf[start:start + span, cs]
                for a in range(n_taps):
                    k = sub * a + phase
                    term = win[sub * a:sub * a + rows, :] * w_ref[k:k + 1, cs]
                    acc = term if acc is None else acc + term
            o_ref[r0:r0 + rows, cs] = acc + b_ref[:, cs]


def _dwconv(h, w_dw, b_dw, b, s, tm, cw):
    t, d = h.shape
    h3 = h.reshape(b, s, d)
    width = w_dw.shape[0]
    out = pl.pallas_call(
        functools.partial(_dwconv_kernel, rows=CONV_ROWS),
        grid=(b, d // cw, s // tm),
        in_specs=[pl.BlockSpec((None, tm, cw), lambda bi, c, si: (bi, si, c)),
                  pl.BlockSpec((width, cw), lambda bi, c, si: (0, c)),
                  pl.BlockSpec((1, cw), lambda bi, c, si: (0, c))],
        out_specs=pl.BlockSpec((None, tm, cw), lambda bi, c, si: (bi, si, c)),
        out_shape=jax.ShapeDtypeStruct((b, s, d), F32),
        scratch_shapes=[pltpu.VMEM((CONV_HALO + tm, cw), F32), pltpu.VMEM((CONV_ROWS + CONV_HALO, LANES), F32)],
        compiler_params=_cparams(("arbitrary", "arbitrary", "arbitrary")),
        name="conv_depthwise",
    )(h3, w_dw, b_dw)
    return out.reshape(t, d)


def _conv_out_kernel(c_ref, x_ref, cg_ref, cb_ref, w_ref, bo_ref, g_ref, b_ref, o_ref, *, alpha):
    u = _ln(c_ref[...], cg_ref[...], cb_ref[...])
    u = u * jax.nn.sigmoid(u)
    mix = jnp.dot(u.astype(BF16), w_ref[...], preferred_element_type=F32) + bo_ref[...]
    o_ref[...] = _ln(alpha * x_ref[...] + mix, g_ref[...], b_ref[...])


def _conv_out(c, xt, cg, cb, w_out, b_out, g1, b1, alpha, tm):
    t, d = xt.shape
    row = pl.BlockSpec((tm, d), lambda i: (i, 0))
    vec = _full((1, d))
    return pl.pallas_call(
        functools.partial(_conv_out_kernel, alpha=alpha),
        grid=(t // tm,),
        in_specs=[row, row, vec, vec, _full((d, d)), vec, vec, vec],
        out_specs=row,
        out_shape=jax.ShapeDtypeStruct((t, d), F32),
        compiler_params=_cparams(("arbitrary",)),
        name="conv_out_ln",
    )(c, xt, cg, cb, w_out, b_out, g1, b1)


def _rope_kernel(pos_ref, invf_ref, c_ref, s_ref):
    ang = pos_ref[...].astype(F32) * invf_ref[...]
    lane = lax.broadcasted_iota(I32, ang.shape, 1)
    half = ROT_DIM // 2
    c_ref[...] = jnp.where(lane < ROT_DIM, jnp.cos(ang), 1.0)
    sn = jnp.sin(ang)
    s_ref[...] = jnp.where(lane < half, -sn, jnp.where(lane < ROT_DIM, sn, 0.0))


def _rope_tables(positions, tm):
    t = positions.size
    half = ROT_DIM // 2
    inv_freq = ROPE_THETA ** (-jnp.arange(half, dtype=F32) * 2.0 / ROT_DIM)
    invf = jnp.zeros((1, LANES), F32).at[0, :half].set(inv_freq).at[0, half:ROT_DIM].set(inv_freq)
    out = jax.ShapeDtypeStruct((t, LANES), F32)
    return pl.pallas_call(
        _rope_kernel,
        grid=(t // tm,),
        in_specs=[pl.BlockSpec((tm, 1), lambda i: (i, 0)), _full((1, LANES))],
        out_specs=[pl.BlockSpec((tm, LANES), lambda i: (i, 0))] * 2,
        out_shape=[out, out],
        compiler_params=_cparams(("arbitrary",)),
        name="rope_tables",
    )(positions.reshape(t, 1), invf)


def _proj_kernel(x_ref, c_ref, s_ref, *rest, rot, heads, dil):
    n_out = len(rot)
    w_refs, o_refs = rest[:n_out], rest[n_out:2 * n_out]
    lhs, tabs, xc = rest[2 * n_out:]
    tm = o_refs[0].shape[1]
    n_lane_tiles = xc.shape[0]
    if dil > 1:
        for c in range(n_lane_tiles):
            xc[c] = x_ref[:, c * LANES:(c + 1) * LANES]
    for r in range(dil):
        rows = slice(r * tm, (r + 1) * tm)
        if dil > 1:
            src = pl.ds(r, tm, stride=dil)
            for c in range(n_lane_tiles):
                lhs[rows, c * LANES:(c + 1) * LANES] = xc[c, src, :].astype(BF16)
        else:
            src = slice(None)
            lhs[rows, :] = x_ref[...].astype(BF16)
        tabs[0, rows, :] = c_ref[src, :]
        tabs[1, rows, :] = s_ref[src, :]
    cos = tabs[0]
    sin = tabs[1]
    lane = lax.broadcasted_iota(I32, cos.shape, 1)
    first = lane < ROT_DIM // 2
    for j, o_ref in enumerate(o_refs):
        y = jnp.dot(lhs[...], w_refs[j][...], preferred_element_type=F32)
        for h in range(heads):
            t = y[:, h * HEAD_DIM:(h + 1) * HEAD_DIM]
            if rot[j]:
                partner = jnp.where(first, pltpu.roll(t, HEAD_DIM - ROT_DIM // 2, 1), pltpu.roll(t, ROT_DIM // 2, 1))
                t = t * cos + partner * sin
            t = t.astype(BF16)
            for r in range(dil):
                o_ref[r, :, h * HEAD_DIM:(h + 1) * HEAD_DIM] = t[r * tm:(r + 1) * tm]


def _project(x3, cos3, sin3, weights, rot, dil, rows, gw):
    b, s, d = x3.shape
    l = s // dil
    tm = rows // dil
    n_out = len(rot)
    out = jax.ShapeDtypeStruct((b, dil, l, gw), BF16)
    tab = pl.BlockSpec((None, rows, LANES), lambda bi, i: (bi, i, 0))
    w_specs = [pl.BlockSpec((d, gw), lambda bi, i, blk=blk: (0, blk)) for _, blk in weights]
    return pl.pallas_call(
        functools.partial(_proj_kernel, rot=rot, heads=gw // HEAD_DIM, dil=dil),
        grid=(b, s // rows),
        in_specs=[pl.BlockSpec((None, rows, d), lambda bi, i: (bi, i, 0)), tab, tab] + w_specs,
        out_specs=[pl.BlockSpec((None, dil, tm, gw), lambda bi, i: (bi, 0, i, 0))] * n_out,
        out_shape=[out] * n_out,
        scratch_shapes=[pltpu.VMEM((rows, d), BF16), pltpu.VMEM((2, rows, LANES), F32),
                        pltpu.VMEM((d // LANES, rows, LANES), F32)],
        compiler_params=_cparams(("arbitrary", "arbitrary")),
        name=f"proj_dil{dil}_n{n_out}",
    )(x3, cos3, sin3, *[w for w, _ in weights])


def _attn_kernel(q_ref, kc_ref, kh_ref, vc_ref, vh_ref, o_ref, lse_ref, kbuf, vbuf, *, heads):
    i = pl.program_id(2)
    tq = q_ref.shape[0]
    blk = ATTN_BLOCK
    kbuf[0:blk, :] = kh_ref[...]
    kbuf[blk:blk + tq, :] = kc_ref[...]
    vbuf[0:blk, :] = vh_ref[...]
    vbuf[blk:blk + tq, :] = vc_ref[...]
    scale = 1.0 / math.sqrt(HEAD_DIM)
    row = lax.broadcasted_iota(I32, (blk, 2 * blk), 0)
    col = lax.broadcasted_iota(I32, (blk, 2 * blk), 1)
    band = jnp.logical_and(col >= row, col <= row + blk)
    first_band = jnp.logical_and(band, jnp.logical_or(col >= blk, i > 0))
    lane = lax.broadcasted_iota(I32, (blk, LANES), 1)
    dims = (((1,), (1,)), ((), ()))
    for n in range(tq // blk):
        rs = slice(n * blk, (n + 1) * blk)
        ks = slice(n * blk, (n + 2) * blk)
        mask = first_band if n == 0 else band
        lse_tile = jnp.zeros((blk, LANES), F32)
        for h in range(heads):
            cs = slice(h * HEAD_DIM, (h + 1) * HEAD_DIM)
            s = lax.dot_general(q_ref[rs, cs], kbuf[ks, cs], dims, preferred_element_type=F32) * scale
            s = jnp.where(mask, s, NEG_INF)
            m = jnp.max(s, axis=1, keepdims=True)
            p = jnp.exp(s - m)
            den = jnp.sum(p, axis=1, keepdims=True)
            o = jnp.dot(p.astype(BF16), vbuf[ks, cs], preferred_element_type=F32)
            o_ref[rs, cs] = (o / den).astype(o_ref.dtype)
            lse_tile = jnp.where(lane == h, m + jnp.log(den), lse_tile)
        lse_ref[rs, :] = lse_tile


def _attention(q, k, v, dil, tq):
    b, _, l, gw = q.shape
    nb = tq // ATTN_BLOCK
    cur = pl.BlockSpec((None, None, tq, gw), lambda bi, r, i: (bi, r, i, 0))
    halo = pl.BlockSpec((None, None, ATTN_BLOCK, gw), lambda bi, r, i: (bi, r, jnp.maximum(i * nb - 1, 0), 0))
    o, lse = pl.pallas_call(
        functools.partial(_attn_kernel, heads=gw // HEAD_DIM),
        grid=(b, dil, l // tq),
        in_specs=[cur, cur, halo, cur, halo],
        out_specs=[pl.BlockSpec((None, None, tq, gw), lambda bi, r, i: (bi, r, i, 0)),
                   pl.BlockSpec((None, None, tq, LANES), lambda bi, r, i: (bi, r, i, 0))],
        out_shape=[jax.ShapeDtypeStruct((b, dil, l, gw), BF16),
                   jax.ShapeDtypeStruct((b, dil, l, LANES), F32)],
        scratch_shapes=[pltpu.VMEM((ATTN_BLOCK + tq, gw), BF16), pltpu.VMEM((ATTN_BLOCK + tq, gw), BF16)],
        compiler_params=_cparams(("arbitrary", "arbitrary", "arbitrary")),
        name=f"attn_dil{dil}",
    )(q, k, k, v, v)
    return o, lse


def _attn_out_kernel(o0_ref, o1_ref, o2_ref, l0_ref, l1_ref, l2_ref, x_ref, w_ref, g_ref, b_ref,
                     out_ref, onat, lnat, mixed, *, alpha, heads, dils):
    tm = x_ref.shape[0]
    for gi, (o_ref, l_ref) in enumerate(((o0_ref, l0_ref), (o1_ref, l1_ref), (o2_ref, l2_ref))):
        dil = dils[gi]
        for r in range(dil):
            dst = pl.ds(r, tm // dil, stride=dil) if dil > 1 else slice(None)
            for h in range(heads):
                onat[gi * heads + h, dst, :] = o_ref[r, :, h * HEAD_DIM:(h + 1) * HEAD_DIM].astype(F32)
            lnat[gi, dst, :] = l_ref[r]
    l0, l1, l2 = lnat[0], lnat[1], lnat[2]
    m = jnp.maximum(jnp.maximum(l0, l1), l2)
    e0, e1, e2 = jnp.exp(l0 - m), jnp.exp(l1 - m), jnp.exp(l2 - m)
    den = e0 + e1 + e2
    w0, w1, w2 = e0 / den, e1 / den, e2 / den
    for h in range(heads):
        cs = slice(h * HEAD_DIM, (h + 1) * HEAD_DIM)
        o = w0[:, h:h + 1] * onat[h] + w1[:, h:h + 1] * onat[heads + h] + w2[:, h:h + 1] * onat[2 * heads + h]
        mixed[:, cs] = o.astype(BF16)
    mix = jnp.dot(mixed[...], w_ref[...], preferred_element_type=F32)
    out_ref[...] = _ln(alpha * x_ref[...] + mix, g_ref[...], b_ref[...])


def _attn_out(outs, lses, x3, w_o, g1, b1, alpha, tm):
    b, s, d = x3.shape
    gw = w_o.shape[0]
    dils = tuple(o.shape[1] for o in outs)
    ospecs = [pl.BlockSpec((None, dil, tm // dil, gw), lambda bi, i: (bi, 0, i, 0)) for dil in dils]
    lspecs = [pl.BlockSpec((None, dil, tm // dil, LANES), lambda bi, i: (bi, 0, i, 0)) for dil in dils]
    xrow = pl.BlockSpec((None, tm, d), lambda bi, i: (bi, i, 0))
    n = len(dils)
    return pl.pallas_call(
        functools.partial(_attn_out_kernel, alpha=alpha, heads=gw // HEAD_DIM, dils=dils),
        grid=(b, s // tm),
        in_specs=ospecs + lspecs + [xrow, _full((gw, d)), _full((1, d)), _full((1, d))],
        out_specs=xrow,
        out_shape=jax.ShapeDtypeStruct((b, s, d), F32),
        scratch_shapes=[pltpu.VMEM((n * gw // HEAD_DIM, tm, HEAD_DIM), F32), pltpu.VMEM((n, tm, LANES), F32),
                        pltpu.VMEM((tm, gw), BF16)],
        compiler_params=_cparams(("arbitrary", "arbitrary")),
        name="attn_out_ln",
    )(*outs, *lses, x3, w_o, g1, b1)


def _pack_bf16_pairs(v, holds_bf16=False):
    half = v.shape[1] // 2
    first, second = v[:, :half], v[:, half:]
    if not holds_bf16:
        first, second = first.astype(BF16).astype(F32), second.astype(BF16).astype(F32)
    return lax.bitcast_convert_type(first, U32) | (lax.bitcast_convert_type(second, U32) >> 16)


def _unpack_bf16_pairs(p):
    first = lax.bitcast_convert_type(p & jnp.uint32(0xFFFF0000), F32)
    second = lax.bitcast_convert_type(p << 16, F32)
    return first, second


def _router_kernel(x_ref, rwt_ref, rb_ref, lp_ref, lpt_ref, gate_ref, cnt_ref, base_ref, tot_ref, carry):
    @pl.when(pl.program_id(0) == 0)
    def _():
        carry[...] = jnp.zeros(carry.shape, F32)

    tm = x_ref.shape[0]
    ne = rwt_ref.shape[0]
    dims = (((1,), (1,)), ((), ()))
    w = rwt_ref[...]
    x = x_ref[...]
    w_hi = w.astype(BF16)
    w_lo = (w - w_hi.astype(F32)).astype(BF16)
    x_hi = x.astype(BF16)
    x_lo = (x - x_hi.astype(F32)).astype(BF16)
    logits = (lax.dot_general(w_hi, x_hi, dims, preferred_element_type=F32)
              + (lax.dot_general(w_lo, x_hi, dims, preferred_element_type=F32)
                 + lax.dot_general(w_hi, x_lo, dims, preferred_element_type=F32)))
    scores = jax.nn.sigmoid(logits)
    sel = scores + rb_ref[...]
    eidx = lax.broadcasted_iota(I32, (ne, tm), 0)
    chosen = jnp.zeros((ne, tm), F32)
    gsum = jnp.zeros((1, tm), F32)
    picks = []
    for _ in range(TOP_K):
        best = jnp.max(sel, axis=0, keepdims=True)
        j = jnp.min(jnp.where(sel == best, eidx, ne), axis=0, keepdims=True)
        onehot = eidx == j
        sc = jnp.sum(jnp.where(onehot, scores, 0.0), axis=0, keepdims=True)
        sel = jnp.where(onehot, -jnp.inf, sel)
        chosen = chosen + onehot.astype(F32)
        gsum = gsum + sc
        picks.append((onehot, sc))
    r = lax.broadcasted_iota(I32, (tm, tm), 0)
    c = lax.broadcasted_iota(I32, (tm, tm), 1)
    rank = jnp.dot(chosen.astype(BF16), (r < c).astype(BF16), preferred_element_type=F32)
    cnt = jnp.sum(chosen, axis=1, keepdims=True)
    cnt_al = jnp.maximum(jnp.floor((cnt + (SEG_ALIGN - 1)) / SEG_ALIGN), 1.0) * SEG_ALIGN
    er = lax.broadcasted_iota(I32, (ne, ne), 0)
    ec = lax.broadcasted_iota(I32, (ne, ne), 1)
    seg_off = jnp.dot((ec < er).astype(BF16), jnp.broadcast_to(cnt_al, (ne, LANES)).astype(BF16),
                      preferred_element_type=F32)[:, 0:1]
    slot = rank + seg_off
    krow = lax.broadcasted_iota(I32, (TOP_K, tm), 0)
    lpt = jnp.zeros((TOP_K, tm), I32)
    gt = jnp.zeros((TOP_K, tm), F32)
    for k, (onehot, sc) in enumerate(picks):
        row = jnp.sum(jnp.where(onehot, slot, 0.0), axis=0, keepdims=True)
        lpt = jnp.where(krow == k, row.astype(I32), lpt)
        gt = jnp.where(krow == k, sc / gsum * ROUTED_SCALE, gt)
    lpt_ref[...] = lpt
    lp_ref[...] = jnp.concatenate([lpt, jnp.zeros((LANES - TOP_K, tm), I32)], axis=0).T
    gate_ref[...] = jnp.concatenate([gt, jnp.zeros((LANES - TOP_K, tm), F32)], axis=0).T
    cnt_ref[...] = cnt_al
    base_ref[...] = carry[...]
    carry[...] = carry[...] + cnt_al
    tot_ref[...] = carry[...]


def _router(xt, router_w, router_bias, tm):
    t, d = xt.shape
    ne = router_w.shape[1]
    n_tiles = t // tm
    wide = pl.BlockSpec((tm, LANES), lambda i: (i, 0))
    per_tile = pl.BlockSpec((None, ne, 1), lambda i: (i, 0, 0))
    return pl.pallas_call(
        _router_kernel,
        grid=(n_tiles,),
        in_specs=[pl.BlockSpec((tm, d), lambda i: (i, 0)), _full((ne, d)), _full((ne, 1))],
        out_specs=[wide, pl.BlockSpec((None, TOP_K, tm), lambda i: (i, 0, 0)), wide, per_tile, per_tile, _full((ne, 1))],
        out_shape=[jax.ShapeDtypeStruct((t, LANES), I32), jax.ShapeDtypeStruct((n_tiles, TOP_K, tm), I32),
                   jax.ShapeDtypeStruct((t, LANES), F32), jax.ShapeDtypeStruct((n_tiles, ne, 1), F32),
                   jax.ShapeDtypeStruct((n_tiles, ne, 1), F32), jax.ShapeDtypeStruct((ne, 1), F32)],
        scratch_shapes=[pltpu.VMEM((ne, 1), F32)],
        compiler_params=_cparams(("arbitrary",)),
        name="moe_router",
    )(xt, router_w.T, router_bias.reshape(ne, 1))


def _start_segments(tbl, base, ne, make_copy):
    off = 0
    for e in range(ne):
        n = pl.multiple_of(tbl[base + e], SEG_ALIGN)
        g = pl.multiple_of(tbl[base + ne + e], SEG_ALIGN)
        make_copy(off, g, n).start()
        off = pl.multiple_of(off + n, SEG_ALIGN)


def _for_sorted_chunks(chunk, n_chunks, max_chunks):
    for ci in range(max_chunks - 1):
        chunk(ci)

    @pl.when(n_chunks == max_chunks)
    def _():
        chunk(max_chunks - 1)


def _start(cp):
    cp.start()


def _wait(cp):
    cp.wait()


def _dispatch_kernel(fs_ref, fn_ref, tbl, lpt_ref, x_ref, xs_hbm, sorted_buf, zbuf, sem, *, ne):
    i = pl.program_id(0)
    last = pl.num_programs(0) - 1
    slot = lax.rem(i, 2)
    tt = x_ref.shape[0]
    width = _table_width(ne)

    @pl.when(i == 0)
    def _():
        zbuf[...] = jnp.zeros(zbuf.shape, zbuf.dtype)

        def visit(action):
            def body(e, carry):
                n = pl.multiple_of(fn_ref[e], SEG_ALIGN)
                s = pl.multiple_of(fs_ref[e], SEG_ALIGN)

                @pl.when(n > 0)
                def _():
                    action(pltpu.make_async_copy(zbuf.at[pl.ds(0, n), :], xs_hbm.at[pl.ds(s, n), :], sem.at[0]))

                return carry
            return body

        lax.fori_loop(0, ne, visit(_start), 0)
        lax.fori_loop(0, ne, visit(_wait), 0)

    xb = x_ref[...].astype(BF16)
    lpt = lpt_ref[...]
    n_chunks = lax.shift_right_logical(tbl[i * width + 2 * ne] + (SORT_CHUNK - 1), SORT_CHUNK.bit_length() - 1)

    def chunk(ci):
        j0 = ci * SORT_CHUNK
        rows = j0 + lax.broadcasted_iota(I32, (SORT_CHUNK, tt), 0)
        hit = lpt[0:1, :] == rows
        for k in range(1, TOP_K):
            hit = jnp.logical_or(hit, lpt[k:k + 1, :] == rows)
        srt = jnp.dot(jnp.where(hit, 1.0, 0.0).astype(BF16), xb, preferred_element_type=F32)
        sorted_buf[slot, j0:j0 + SORT_CHUNK, :] = _pack_bf16_pairs(srt, holds_bf16=True)

    _for_sorted_chunks(chunk, n_chunks, sorted_buf.shape[1] // SORT_CHUNK)

    def writes(sl):
        return lambda off, g, n: pltpu.make_async_copy(
            sorted_buf.at[sl, pl.ds(off, n), :], xs_hbm.at[pl.ds(g, n), :], sem.at[sl])

    _start_segments(tbl, i * width, ne, writes(slot))

    def wait_all(step, sl):
        tot = pl.multiple_of(tbl[step * width + 2 * ne], SEG_ALIGN)
        pltpu.make_async_copy(sorted_buf.at[sl, pl.ds(0, tot), :], xs_hbm.at[pl.ds(0, tot), :], sem.at[sl]).wait()

    @pl.when(i > 0)
    def _():
        wait_all(i - 1, 1 - slot)

    @pl.when(i == last)
    def _():
        wait_all(i, slot)


def _sorted_rows(tt, ne):
    rows = -(-(tt * TOP_K + ne * SEG_ALIGN) // SORT_CHUNK) * SORT_CHUNK
    assert rows < 2 ** 16
    return rows


def _table_width(ne):
    return 2 * ne + SEG_ALIGN


def _dispatch(xt, lpt, table, fill_start, fill_n, n_rows, tt, br):
    t, d = xt.shape
    ne = fill_start.shape[0]
    grid_spec = pltpu.PrefetchScalarGridSpec(
        num_scalar_prefetch=3,
        grid=(t // tt,),
        in_specs=[pl.BlockSpec((None, TOP_K, tt), lambda i, *_: (i, 0, 0)),
                  pl.BlockSpec((tt, d), lambda i, *_: (i, 0))],
        out_specs=pl.BlockSpec(memory_space=pl.ANY),
        scratch_shapes=[pltpu.VMEM((2, _sorted_rows(tt, ne), d // 2), U32),
                        pltpu.VMEM((br, d // 2), U32), pltpu.SemaphoreType.DMA((2,))],
    )
    return pl.pallas_call(
        functools.partial(_dispatch_kernel, ne=ne),
        grid_spec=grid_spec,
        out_shape=jax.ShapeDtypeStruct((n_rows, d // 2), U32),
        compiler_params=_cparams(("arbitrary",)),
        name="moe_dispatch",
    )(fill_start, fill_n, table, lpt, xt)


def _expert_kernel(be_ref, nu_ref, xs_ref, wg_ref, wu_ref, wd_ref, y_ref, wg_b, wu_b, wd_b):
    i = pl.program_id(0)
    live = i < nu_ref[0]

    @pl.when(jnp.logical_and(live, jnp.logical_or(i == 0, be_ref[i] != be_ref[jnp.maximum(i - 1, 0)])))
    def _():
        wg_b[...] = wg_ref[...].astype(BF16)
        wu_b[...] = wu_ref[...].astype(BF16)
        wd_b[...] = wd_ref[...].astype(BF16)

    @pl.when(live)
    def _():
        first, second = _unpack_bf16_pairs(xs_ref[...])
        xb = jnp.concatenate([first.astype(BF16), second.astype(BF16)], axis=1)
        g = jnp.dot(xb, wg_b[...], preferred_element_type=F32)
        u = jnp.dot(xb, wu_b[...], preferred_element_type=F32)
        h = g * jax.nn.sigmoid(g) * u
        y = jnp.dot(h.astype(BF16), wd_b[...], preferred_element_type=F32)
        y_ref[...] = _pack_bf16_pairs(y)


def _experts(xs, blk_exp, n_used, w_gate, w_up, w_down, layer, br):
    p, dh = xs.shape
    d, f = w_gate.shape[2:]
    n_blocks = p // br

    def row_map(i, be, nu):
        return (jnp.minimum(i, nu[0] - 1), 0)

    def w_map(i, be, nu):
        return (layer, be[jnp.minimum(i, nu[0] - 1)], 0, 0)

    grid_spec = pltpu.PrefetchScalarGridSpec(
        num_scalar_prefetch=2,
        grid=(n_blocks,),
        in_specs=[pl.BlockSpec((br, dh), row_map),
                  pl.BlockSpec((None, None, d, f), w_map),
                  pl.BlockSpec((None, None, d, f), w_map),
                  pl.BlockSpec((None, None, f, d), w_map)],
        out_specs=pl.BlockSpec((br, dh), row_map),
        scratch_shapes=[pltpu.VMEM((d, f), BF16), pltpu.VMEM((d, f), BF16), pltpu.VMEM((f, d), BF16)],
    )
    return pl.pallas_call(
        _expert_kernel,
        grid_spec=grid_spec,
        out_shape=jax.ShapeDtypeStruct((p, dh), U32),
        compiler_params=_cparams(("arbitrary",)),
        name="moe_experts",
    )(blk_exp, n_used, xs, w_gate, w_up, w_down)


def _combine_kernel(tbl, y_hbm, lp_ref, gate_ref, x_ref, sg_ref, su_ref, sd_ref, g_ref, b_ref, o_ref,
                    ysorted, acc, lpb, gb, sem, *, ne, alpha):
    i = pl.program_id(0)
    last = pl.num_programs(0) - 1
    slot = lax.rem(i, 2)
    tt = x_ref.shape[0]
    half = x_ref.shape[1] // 2
    width = _table_width(ne)

    def gathers(sl):
        return lambda off, g, n: pltpu.make_async_copy(
            y_hbm.at[pl.ds(g, n), :], ysorted.at[sl, pl.ds(off, n), :], sem.at[sl])

    @pl.when(i == 0)
    def _():
        _start_segments(tbl, 0, ne, gathers(0))

    nxt = jnp.minimum(i + 1, last)
    _start_segments(tbl, nxt * width, ne, gathers(1 - slot))

    x = x_ref[...]
    xb = x.astype(BF16)
    g = jnp.dot(xb, sg_ref[...], preferred_element_type=F32)
    u = jnp.dot(xb, su_ref[...], preferred_element_type=F32)
    h = g * jax.nn.sigmoid(g) * u
    acc[...] = jnp.dot(h.astype(BF16), sd_ref[...], preferred_element_type=F32)
    gate_bits = lax.bitcast_convert_type(gate_ref[...].astype(BF16).astype(F32), U32)
    word = gate_bits | lax.bitcast_convert_type(lp_ref[...], U32)
    for k in range(TOP_K):
        wb = jnp.broadcast_to(word[:, k:k + 1], (tt, LANES))
        lpb[k] = lax.bitcast_convert_type(wb & jnp.uint32(0xFFFF), I32)
        gb[k] = lax.bitcast_convert_type(wb & jnp.uint32(0xFFFF0000), F32)
    total = pl.multiple_of(tbl[i * width + 2 * ne], SEG_ALIGN)
    pltpu.make_async_copy(y_hbm.at[pl.ds(0, total), :], ysorted.at[slot, pl.ds(0, total), :], sem.at[slot]).wait()
    n_chunks = lax.shift_right_logical(total + (SORT_CHUNK - 1), SORT_CHUNK.bit_length() - 1)
    lane = lax.broadcasted_iota(I32, (tt, LANES), 1)

    def chunk(ci):
        j0 = ci * SORT_CHUNK
        parts = []
        for c0 in range(0, SORT_CHUNK, LANES):
            cols = lane + (j0 + c0)
            gm = jnp.zeros((tt, LANES), F32)
            for k in range(TOP_K):
                gm = jnp.where(lpb[k] == cols, gb[k], gm)
            parts.append(gm)
        gmat = jnp.concatenate(parts, axis=1).astype(BF16)
        rows = j0 + lax.broadcasted_iota(I32, (SORT_CHUNK, half), 0)
        ys = jnp.where(rows < total, ysorted[slot, j0:j0 + SORT_CHUNK, :], jnp.uint32(0))
        first, second = _unpack_bf16_pairs(ys)
        first = first.astype(BF16)
        second = second.astype(BF16)
        acc[:, :half] += jnp.dot(gmat, first, preferred_element_type=F32)
        acc[:, half:] += jnp.dot(gmat, second, preferred_element_type=F32)

    _for_sorted_chunks(chunk, n_chunks, ysorted.shape[1] // SORT_CHUNK)
    o_ref[...] = _ln(alpha * x + acc[...], g_ref[...], b_ref[...])

    @pl.when(i == last)
    def _():
        pltpu.make_async_copy(y_hbm.at[pl.ds(0, total), :], ysorted.at[1 - slot, pl.ds(0, total), :],
                              sem.at[1 - slot]).wait()


def _combine(table, y, lp, gate, xt, sw_gate, sw_up, sw_down, g2, b2, alpha, tt, ne):
    t, d = xt.shape
    f = sw_gate.shape[1]
    row = pl.BlockSpec((tt, d), lambda i, *_: (i, 0))
    wide = pl.BlockSpec((tt, LANES), lambda i, *_: (i, 0))
    grid_spec = pltpu.PrefetchScalarGridSpec(
        num_scalar_prefetch=1,
        grid=(t // tt,),
        in_specs=[pl.BlockSpec(memory_space=pl.ANY), wide, wide, row,
                  _full((d, f)), _full((d, f)), _full((f, d)), _full((1, d)), _full((1, d))],
        out_specs=row,
        scratch_shapes=[pltpu.VMEM((2, _sorted_rows(tt, ne), d // 2), U32),
                        pltpu.VMEM((tt, d), F32), pltpu.VMEM((TOP_K, tt, LANES), I32), pltpu.VMEM((TOP_K, tt, LANES), F32),
                        pltpu.SemaphoreType.DMA((2,))],
    )
    return pl.pallas_call(
        functools.partial(_combine_kernel, ne=ne, alpha=alpha),
        grid_spec=grid_spec,
        out_shape=jax.ShapeDtypeStruct((t, d), F32),
        compiler_params=_cparams(("arbitrary",)),
        name="moe_combine_ln",
    )(table, y, lp, gate, xt, sw_gate, sw_up, sw_down, g2, b2)


def _moe_layer(xt, router_w, router_bias, w_gate, w_up, w_down, layer, sw_gate, sw_up, sw_down, g2, b2, alpha, cfg):
    t, d = xt.shape
    ne = router_w.shape[1]
    br = cfg["moe_rows"]
    tt = cfg["moe_tile"]
    n_tiles = t // tt
    lp, lpt, gate, cnt, base, tot = _router(xt, router_w, router_bias, tt)
    cnt = cnt.reshape(n_tiles, ne).astype(I32)
    tot = tot.reshape(ne).astype(I32)
    padded = (tot + br - 1) // br * br
    pends = jnp.cumsum(padded)
    pstarts = pends - padded
    seg_start = pstarts[None, :] + base.reshape(n_tiles, ne).astype(I32)
    tile_rows = jnp.sum(cnt, axis=1, keepdims=True)
    table = jnp.concatenate([cnt, seg_start, tile_rows, jnp.zeros((n_tiles, SEG_ALIGN - 1), I32)], axis=1).reshape(-1)
    n_blocks = -(-(t * TOP_K + ne * n_tiles * SEG_ALIGN) // br) + ne
    blk_start = jnp.arange(n_blocks, dtype=I32) * br
    blk_exp = jnp.minimum(jnp.sum(pends[None, :] <= blk_start[:, None], axis=1), ne - 1).astype(I32)
    n_used = (pends[-1:] // br).astype(I32)
    xs = _dispatch(xt, lpt, table, pstarts + tot, padded - tot, n_blocks * br, tt, br)
    y = _experts(xs, blk_exp, n_used, w_gate, w_up, w_down, layer, br)
    return _combine(table, y, lp, gate, xt, sw_gate.astype(BF16), sw_up.astype(BF16), sw_down.astype(BF16),
                    g2, b2, alpha, tt, ne)


def _config(b, s, d):
    return dict(row_tm=min(512, s), conv_tm=min(1024, s), conv_cw=min(512, d), proj_rows=min(512, s), attn_out_tm=512,
                moe_rows=1024, moe_tile=min(256, b * s))


def kernel(x, positions, ln1_g, ln1_b, ln2_g, ln2_b, conv_w_in, conv_b_in, conv_w_dw, conv_b_dw, conv_ln_g, conv_ln_b, conv_w_out, conv_b_out, w_kv, attn_w_q, attn_w_o, router_w, router_bias, exp_w_gate, exp_w_up, exp_w_down, sh_w_gate, sh_w_up, sh_w_down):
    b, s, d = x.shape
    t = b * s
    depth = ln1_g.shape[0]
    n_conv = conv_w_in.shape[0]
    alpha = (2.0 * depth) ** 0.25
    cfg = _config(b, s, d)
    n_groups = len(WINDOW_DILATIONS)
    gw = attn_w_o.shape[1]
    vec = lambda a: a.reshape(1, -1)

    xt = x.reshape(t, d)
    kv = None
    tables = None
    for layer in range(depth):
        g1, b1 = vec(ln1_g[layer]), vec(ln1_b[layer])
        if layer < n_conv:
            h = _glu(xt, conv_w_in[layer].astype(BF16), vec(conv_b_in[layer]), cfg["row_tm"])
            c = _dwconv(h, conv_w_dw[layer], vec(conv_b_dw[layer]), b, s, cfg["conv_tm"], cfg["conv_cw"])
            xt = _conv_out(c, xt, vec(conv_ln_g[layer]), vec(conv_ln_b[layer]), conv_w_out[layer].astype(BF16),
                           vec(conv_b_out[layer]), g1, b1, alpha, cfg["row_tm"])
        else:
            j = layer - n_conv
            if tables is None:
                cos, sin = _rope_tables(positions, cfg["row_tm"])
                tables = (cos.reshape(b, s, LANES), sin.reshape(b, s, LANES))
            x3 = xt.reshape(b, s, d)
            wq = attn_w_q[j].astype(BF16)
            outs, lses = [], []
            new_kv = []
            for g, (win, dil) in enumerate(WINDOW_DILATIONS):
                assert win // dil == ATTN_BLOCK
                if kv is None:
                    wkv = w_kv.astype(BF16)
                    q, kg, vg = _project(x3, *tables, [(wq, g), (wkv, g), (wkv, n_groups + g)], (True, True, False),
                                         dil, cfg["proj_rows"], gw)
                    new_kv.append((kg, vg))
                else:
                    (q,) = _project(x3, *tables, [(wq, g)], (True,), dil, cfg["proj_rows"], gw)
                    kg, vg = kv[g]
                o, lse = _attention(q, kg, vg, dil, min(1024, s // dil))
                outs.append(o)
                lses.append(lse)
            if kv is None:
                kv = new_kv
            xt = _attn_out(outs, lses, x3, attn_w_o[j].astype(BF16), g1, b1, alpha, cfg["attn_out_tm"]).reshape(t, d)
        xt = _moe_layer(xt, router_w[layer], router_bias[layer], exp_w_gate, exp_w_up, exp_w_down, layer,
                        sh_w_gate[layer], sh_w_up[layer], sh_w_down[layer],
                        vec(ln2_g[layer]), vec(ln2_b[layer]), alpha, cfg)
    return xt.reshape(b, s, d)
```

```python
import functools
import math

import jax
import jax.numpy as jnp
from jax import lax
from jax.experimental import pallas as pl
from jax.experimental.pallas import tpu as pltpu

F32 = jnp.float32
BF16 = jnp.bfloat16
I32 = jnp.int32
U32 = jnp.uint32

LANES = 128
HEAD_DIM = 128
ROT_DIM = HEAD_DIM // 4
ROPE_THETA = 500000.0
ATTN_BLOCK = 128
WINDOW_DILATIONS = ((128, 1), (512, 4), (2048, 16))
NEG_INF = -1e30
TOP_K = 8
ROUTED_SCALE = 2.5
LN_EPS = 1e-5
CONV_HALO = 32
CONV_ROWS = 64
VMEM_LIMIT = 56 * 1024 * 1024
SEG_ALIGN = 8
SORT_CHUNK = 256
EXPERT_PART_ROWS = 256


def _cparams(sem):
    return pltpu.CompilerParams(dimension_semantics=sem, vmem_limit_bytes=VMEM_LIMIT)


def _ln(y, g, b):
    mu = jnp.mean(y, axis=-1, keepdims=True)
    d = y - mu
    var = jnp.mean(d * d, axis=-1, keepdims=True)
    return d * lax.rsqrt(var + LN_EPS) * g + b


def _full(shape):
    n = len(shape)
    return pl.BlockSpec(shape, lambda *_: (0,) * n)


def _glu_kernel(x_ref, w_ref, b_ref, o_ref):
    d = o_ref.shape[-1]
    h = jnp.dot(x_ref[...].astype(BF16), w_ref[...], preferred_element_type=F32) + b_ref[...]
    o_ref[...] = h[:, :d] * jax.nn.sigmoid(h[:, d:])


def _glu(xt, w_in, b_in, tm):
    t, d = xt.shape
    return pl.pallas_call(
        _glu_kernel,
        grid=(t // tm,),
        in_specs=[pl.BlockSpec((tm, d), lambda i: (i, 0)), _full((d, 2 * d)), _full((1, 2 * d))],
        out_specs=pl.BlockSpec((tm, d), lambda i: (i, 0)),
        out_shape=jax.ShapeDtypeStruct((t, d), F32),
        compiler_params=_cparams(("arbitrary",)),
        name="conv_glu",
    )(xt, w_in, b_in)


def _dwconv_kernel(h_ref, w_ref, b_ref, o_ref, buf, win, *, rows):
    s = pl.program_id(2)
    tm = h_ref.shape[0]
    width = w_ref.shape[0]
    off = CONV_HALO - (width - 1)

    @pl.when(s == 0)
    def _():
        buf[0:CONV_HALO, :] = jnp.zeros((CONV_HALO, buf.shape[1]), F32)

    @pl.when(s > 0)
    def _():
        buf[0:CONV_HALO, :] = buf[tm:tm + CONV_HALO, :]

    buf[CONV_HALO:CONV_HALO + tm, :] = h_ref[...]
    sub = 8
    for c0 in range(0, buf.shape[1], LANES):
        cs = slice(c0, c0 + LANES)
        for r0 in range(0, tm, rows):
            acc = None
            for phase in range(min(sub, width)):
                n_taps = (width - 1 - phase) // sub + 1
                start = r0 + off + phase
                span = rows + sub * (n_taps - 1)
                win[0:span, :] = buf[start:start + span, cs]
                for a in range(n_taps):
                    k = sub * a + phase
                    term = win[sub * a:sub * a + rows, :] * w_ref[k:k + 1, cs]
                    acc = term if acc is None else acc + term
            o_ref[r0:r0 + rows, cs] = acc + b_ref[:, cs]


def _dwconv(h, w_dw, b_dw, b, s, tm, cw):
    t, d = h.shape
    h3 = h.reshape(b, s, d)
    width = w_dw.shape[0]
    out = pl.pallas_call(
        functools.partial(_dwconv_kernel, rows=CONV_ROWS),
        grid=(b, d // cw, s // tm),
        in_specs=[pl.BlockSpec((None, tm, cw), lambda bi, c, si: (bi, si, c)),
                  pl.BlockSpec((width, cw), lambda bi, c, si: (0, c)),
                  pl.BlockSpec((1, cw), lambda bi, c, si: (0, c))],
        out_specs=pl.BlockSpec((None, tm, cw), lambda bi, c, si: (bi, si, c)),
        out_shape=jax.ShapeDtypeStruct((b, s, d), F32),
        scratch_shapes=[pltpu.VMEM((CONV_HALO + tm, cw), F32), pltpu.VMEM((CONV_ROWS + CONV_HALO, LANES), F32)],
        compiler_params=_cparams(("arbitrary", "arbitrary", "arbitrary")),
        name="conv_depthwise",
    )(h3, w_dw, b_dw)
    return out.reshape(t, d)


def _conv_out_kernel(c_ref, x_ref, cg_ref, cb_ref, w_ref, bo_ref, g_ref, b_ref, o_ref, *, alpha):
    u = _ln(c_ref[...], cg_ref[...], cb_ref[...])
    u = u * jax.nn.sigmoid(u)
    mix = jnp.dot(u.astype(BF16), w_ref[...], preferred_element_type=F32) + bo_ref[...]
    o_ref[...] = _ln(alpha * x_ref[...] + mix, g_ref[...], b_ref[...])


def _conv_out(c, xt, cg, cb, w_out, b_out, g1, b1, alpha, tm):
    t, d = xt.shape
    row = pl.BlockSpec((tm, d), lambda i: (i, 0))
    vec = _full((1, d))
    return pl.pallas_call(
        functools.partial(_conv_out_kernel, alpha=alpha),
        grid=(t // tm,),
        in_specs=[row, row, vec, vec, _full((d, d)), vec, vec, vec],
        out_specs=row,
        out_shape=jax.ShapeDtypeStruct((t, d), F32),
        compiler_params=_cparams(("arbitrary",)),
        name="conv_out_ln",
    )(c, xt, cg, cb, w_out, b_out, g1, b1)


def _rope_kernel(pos_ref, invf_ref, c_ref, s_ref):
    ang = pos_ref[...].astype(F32) * invf_ref[...]
    lane = lax.broadcasted_iota(I32, ang.shape, 1)
    half = ROT_DIM // 2
    c_ref[...] = jnp.where(lane < ROT_DIM, jnp.cos(ang), 1.0)
    sn = jnp.sin(ang)
    s_ref[...] = jnp.where(lane < half, -sn, jnp.where(lane < ROT_DIM, sn, 0.0))


def _rope_tables(positions, tm):
    t = positions.size
    half = ROT_DIM // 2
    inv_freq = ROPE_THETA ** (-jnp.arange(half, dtype=F32) * 2.0 / ROT_DIM)
    invf = jnp.zeros((1, LANES), F32).at[0, :half].set(inv_freq).at[0, half:ROT_DIM].set(inv_freq)
    out = jax.ShapeDtypeStruct((t, LANES), F32)
    return pl.pallas_call(
        _rope_kernel,
        grid=(t // tm,),
        in_specs=[pl.BlockSpec((tm, 1), lambda i: (i, 0)), _full((1, LANES))],
        out_specs=[pl.BlockSpec((tm, LANES), lambda i: (i, 0))] * 2,
        out_shape=[out, out],
        compiler_params=_cparams(("arbitrary",)),
        name="rope_tables",
    )(positions.reshape(t, 1), invf)


def _proj_kernel(x_ref, c_ref, s_ref, *rest, rot, heads, dil):
    n_out = len(rot)
    w_refs, o_refs = rest[:n_out], rest[n_out:2 * n_out]
    lhs, tabs, xc = rest[2 * n_out:]
    tm = o_refs[0].shape[1]
    n_lane_tiles = xc.shape[0]
    if dil > 1:
        for c in range(n_lane_tiles):
            xc[c] = x_ref[:, c * LANES:(c + 1) * LANES]
    for r in range(dil):
        rows = slice(r * tm, (r + 1) * tm)
        if dil > 1:
            src = pl.ds(r, tm, stride=dil)
            for c in range(n_lane_tiles):
                lhs[rows, c * LANES:(c + 1) * LANES] = xc[c, src, :].astype(BF16)
        else:
            src = slice(None)
            lhs[rows, :] = x_ref[...].astype(BF16)
        tabs[0, rows, :] = c_ref[src, :]
        tabs[1, rows, :] = s_ref[src, :]
    cos = tabs[0]
    sin = tabs[1]
    lane = lax.broadcasted_iota(I32, cos.shape, 1)
    first = lane < ROT_DIM // 2
    for j, o_ref in enumerate(o_refs):
        y = jnp.dot(lhs[...], w_refs[j][...], preferred_element_type=F32)
        for h in range(heads):
            t = y[:, h * HEAD_DIM:(h + 1) * HEAD_DIM]
            if rot[j]:
                partner = jnp.where(first, pltpu.roll(t, HEAD_DIM - ROT_DIM // 2, 1), pltpu.roll(t, ROT_DIM // 2, 1))
                t = t * cos + partner * sin
            t = t.astype(BF16)
            for r in range(dil):
                o_ref[r, :, h * HEAD_DIM:(h + 1) * HEAD_DIM] = t[r * tm:(r + 1) * tm]


def _project(x3, cos3, sin3, weights, rot, dil, rows, gw):
    b, s, d = x3.shape
    l = s // dil
    tm = rows // dil
    n_out = len(rot)
    out = jax.ShapeDtypeStruct((b, dil, l, gw), BF16)
    tab = pl.BlockSpec((None, rows, LANES), lambda bi, i: (bi, i, 0))
    w_specs = [pl.BlockSpec((d, gw), lambda bi, i, blk=blk: (0, blk)) for _, blk in weights]
    return pl.pallas_call(
        functools.partial(_proj_kernel, rot=rot, heads=gw // HEAD_DIM, dil=dil),
        grid=(b, s // rows),
        in_specs=[pl.BlockSpec((None, rows, d), lambda bi, i: (bi, i, 0)), tab, tab] + w_specs,
        out_specs=[pl.BlockSpec((None, dil, tm, gw), lambda bi, i: (bi, 0, i, 0))] * n_out,
        out_shape=[out] * n_out,
        scratch_shapes=[pltpu.VMEM((rows, d), BF16), pltpu.VMEM((2, rows, LANES), F32),
                        pltpu.VMEM((d // LANES, rows, LANES), F32)],
        compiler_params=_cparams(("arbitrary", "arbitrary")),
        name=f"proj_dil{dil}_n{n_out}",
    )(x3, cos3, sin3, *[w for w, _ in weights])


def _attn_kernel(q_ref, kc_ref, kh_ref, vc_ref, vh_ref, o_ref, lse_ref, kbuf, vbuf, *, heads):
    i = pl.program_id(2)
    tq = q_ref.shape[0]
    blk = ATTN_BLOCK
    kbuf[0:blk, :] = kh_ref[...]
    kbuf[blk:blk + tq, :] = kc_ref[...]
    vbuf[0:blk, :] = vh_ref[...]
    vbuf[blk:blk + tq, :] = vc_ref[...]
    scale = 1.0 / math.sqrt(HEAD_DIM)
    row = lax.broadcasted_iota(I32, (blk, 2 * blk), 0)
    col = lax.broadcasted_iota(I32, (blk, 2 * blk), 1)
    band = jnp.logical_and(col >= row, col <= row + blk)
    first_band = jnp.logical_and(band, jnp.logical_or(col >= blk, i > 0))
    lane = lax.broadcasted_iota(I32, (blk, LANES), 1)
    dims = (((1,), (1,)), ((), ()))
    for n in range(tq // blk):
        rs = slice(n * blk, (n + 1) * blk)
        ks = slice(n * blk, (n + 2) * blk)
        mask = first_band if n == 0 else band
        lse_tile = jnp.zeros((blk, LANES), F32)
        for h in range(heads):
            cs = slice(h * HEAD_DIM, (h + 1) * HEAD_DIM)
            s = lax.dot_general(q_ref[rs, cs], kbuf[ks, cs], dims, preferred_element_type=F32) * scale
            s = jnp.where(mask, s, NEG_INF)
            m = jnp.max(s, axis=1, keepdims=True)
            p = jnp.exp(s - m)
            den = jnp.sum(p, axis=1, keepdims=True)
            o = jnp.dot(p.astype(BF16), vbuf[ks, cs], preferred_element_type=F32)
            o_ref[rs, cs] = (o / den).astype(o_ref.dtype)
            lse_tile = jnp.where(lane == h, m + jnp.log(den), lse_tile)
        lse_ref[rs, :] = lse_tile


def _attention(q, k, v, dil, tq):
    b, _, l, gw = q.shape
    nb = tq // ATTN_BLOCK
    cur = pl.BlockSpec((None, None, tq, gw), lambda bi, r, i: (bi, r, i, 0))
    halo = pl.BlockSpec((None, None, ATTN_BLOCK, gw), lambda bi, r, i: (bi, r, jnp.maximum(i * nb - 1, 0), 0))
    o, lse = pl.pallas_call(
        functools.partial(_attn_kernel, heads=gw // HEAD_DIM),
        grid=(b, dil, l // tq),
        in_specs=[cur, cur, halo, cur, halo],
        out_specs=[pl.BlockSpec((None, None, tq, gw), lambda bi, r, i: (bi, r, i, 0)),
                   pl.BlockSpec((None, None, tq, LANES), lambda bi, r, i: (bi, r, i, 0))],
        out_shape=[jax.ShapeDtypeStruct((b, dil, l, gw), BF16),
                   jax.ShapeDtypeStruct((b, dil, l, LANES), F32)],
        scratch_shapes=[pltpu.VMEM((ATTN_BLOCK + tq, gw), BF16), pltpu.VMEM((ATTN_BLOCK + tq, gw), BF16)],
        compiler_params=_cparams(("arbitrary", "arbitrary", "arbitrary")),
        name=f"attn_dil{dil}",
    )(q, k, k, v, v)
    return o, lse


def _attn_out_kernel(o0_ref, o1_ref, o2_ref, l0_ref, l1_ref, l2_ref, x_ref, w_ref, g_ref, b_ref,
                     out_ref, onat, lnat, mixed, *, alpha, heads, dils):
    tm = x_ref.shape[0]
    for gi, (o_ref, l_ref) in enumerate(((o0_ref, l0_ref), (o1_ref, l1_ref), (o2_ref, l2_ref))):
        dil = dils[gi]
        for r in range(dil):
            dst = pl.ds(r, tm // dil, stride=dil) if dil > 1 else slice(None)
            for h in range(heads):
                onat[gi * heads + h, dst, :] = o_ref[r, :, h * HEAD_DIM:(h + 1) * HEAD_DIM].astype(F32)
            lnat[gi, dst, :] = l_ref[r]
    l0, l1, l2 = lnat[0], lnat[1], lnat[2]
    m = jnp.maximum(jnp.maximum(l0, l1), l2)
    e0, e1, e2 = jnp.exp(l0 - m), jnp.exp(l1 - m), jnp.exp(l2 - m)
    den = e0 + e1 + e2
    w0, w1, w2 = e0 / den, e1 / den, e2 / den
    for h in range(heads):
        cs = slice(h * HEAD_DIM, (h + 1) * HEAD_DIM)
        o = w0[:, h:h + 1] * onat[h] + w1[:, h:h + 1] * onat[heads + h] + w2[:, h:h + 1] * onat[2 * heads + h]
        mixed[:, cs] = o.astype(BF16)
    mix = jnp.dot(mixed[...], w_ref[...], preferred_element_type=F32)
    out_ref[...] = _ln(alpha * x_ref[...] + mix, g_ref[...], b_ref[...])


def _attn_out(outs, lses, x3, w_o, g1, b1, alpha, tm):
    b, s, d = x3.shape
    gw = w_o.shape[0]
    dils = tuple(o.shape[1] for o in outs)
    ospecs = [pl.BlockSpec((None, dil, tm // dil, gw), lambda bi, i: (bi, 0, i, 0)) for dil in dils]
    lspecs = [pl.BlockSpec((None, dil, tm // dil, LANES), lambda bi, i: (bi, 0, i, 0)) for dil in dils]
    xrow = pl.BlockSpec((None, tm, d), lambda bi, i: (bi, i, 0))
    n = len(dils)
    return pl.pallas_call(
        functools.partial(_attn_out_kernel, alpha=alpha, heads=gw // HEAD_DIM, dils=dils),
        grid=(b, s // tm),
        in_specs=ospecs + lspecs + [xrow, _full((gw, d)), _full((1, d)), _full((1, d))],
        out_specs=xrow,
        out_shape=jax.ShapeDtypeStruct((b, s, d), F32),
        scratch_shapes=[pltpu.VMEM((n * gw // HEAD_DIM, tm, HEAD_DIM), F32), pltpu.VMEM((n, tm, LANES), F32),
                        pltpu.VMEM((tm, gw), BF16)],
        compiler_params=_cparams(("arbitrary", "arbitrary")),
        name="attn_out_ln",
    )(*outs, *lses, x3, w_o, g1, b1)


def _pack_bf16_pairs(v, holds_bf16=False):
    half = v.shape[1] // 2
    first, second = v[:, :half], v[:, half:]
    if not holds_bf16:
        first, second = first.astype(BF16).astype(F32), second.astype(BF16).astype(F32)
    return lax.bitcast_convert_type(first, U32) | (lax.bitcast_convert_type(second, U32) >> 16)


def _unpack_bf16_pairs(p):
    first = lax.bitcast_convert_type(p & jnp.uint32(0xFFFF0000), F32)
    second = lax.bitcast_convert_type(p << 16, F32)
    return first, second


def _router_kernel(x_ref, rwt_ref, rb_ref, lp_ref, lpt_ref, gate_ref, cnt_ref, base_ref, tot_ref, carry):
    @pl.when(pl.program_id(0) == 0)
    def _():
        carry[...] = jnp.zeros(carry.shape, F32)

    tm = x_ref.shape[0]
    ne = rwt_ref.shape[0]
    dims = (((1,), (1,)), ((), ()))
    w = rwt_ref[...]
    x = x_ref[...]
    w_hi = w.astype(BF16)
    w_lo = (w - w_hi.astype(F32)).astype(BF16)
    x_hi = x.astype(BF16)
    x_lo = (x - x_hi.astype(F32)).astype(BF16)
    logits = (lax.dot_general(w_hi, x_hi, dims, preferred_element_type=F32)
              + (lax.dot_general(w_lo, x_hi, dims, preferred_element_type=F32)
                 + lax.dot_general(w_hi, x_lo, dims, preferred_element_type=F32)))
    scores = jax.nn.sigmoid(logits)
    sel = scores + rb_ref[...]
    eidx = lax.broadcasted_iota(I32, (ne, tm), 0)
    chosen = jnp.zeros((ne, tm), F32)
    gsum = jnp.zeros((1, tm), F32)
    picks = []
    for _ in range(TOP_K):
        best = jnp.max(sel, axis=0, keepdims=True)
        j = jnp.min(jnp.where(sel == best, eidx, ne), axis=0, keepdims=True)
        onehot = eidx == j
        sc = jnp.sum(jnp.where(onehot, scores, 0.0), axis=0, keepdims=True)
        sel = jnp.where(onehot, -jnp.inf, sel)
        chosen = chosen + onehot.astype(F32)
        gsum = gsum + sc
        picks.append((onehot, sc))
    r = lax.broadcasted_iota(I32, (tm, tm), 0)
    c = lax.broadcasted_iota(I32, (tm, tm), 1)
    rank = jnp.dot(chosen.astype(BF16), (r < c).astype(BF16), preferred_element_type=F32)
    cnt = jnp.sum(chosen, axis=1, keepdims=True)
    cnt_al = jnp.maximum(jnp.floor((cnt + (SEG_ALIGN - 1)) / SEG_ALIGN), 1.0) * SEG_ALIGN
    er = lax.broadcasted_iota(I32, (ne, ne), 0)
    ec = lax.broadcasted_iota(I32, (ne, ne), 1)
    seg_off = jnp.dot((ec < er).astype(BF16), jnp.broadcast_to(cnt_al, (ne, LANES)).astype(BF16),
                      preferred_element_type=F32)[:, 0:1]
    slot = rank + seg_off
    krow = lax.broadcasted_iota(I32, (TOP_K, tm), 0)
    lpt = jnp.zeros((TOP_K, tm), I32)
    gt = jnp.zeros((TOP_K, tm), F32)
    for k, (onehot, sc) in enumerate(picks):
        row = jnp.sum(jnp.where(onehot, slot, 0.0), axis=0, keepdims=True)
        lpt = jnp.where(krow == k, row.astype(I32), lpt)
        gt = jnp.where(krow == k, sc / gsum * ROUTED_SCALE, gt)
    lpt_ref[...] = lpt
    lp_ref[...] = jnp.concatenate([lpt, jnp.zeros((LANES - TOP_K, tm), I32)], axis=0).T
    gate_ref[...] = jnp.concatenate([gt, jnp.zeros((LANES - TOP_K, tm), F32)], axis=0).T
    cnt_ref[...] = cnt_al
    base_ref[...] = carry[...]
    carry[...] = carry[...] + cnt_al
    tot_ref[...] = carry[...]


def _router(xt, router_w, router_bias, tm):
    t, d = xt.shape
    ne = router_w.shape[1]
    n_tiles = t // tm
    wide = pl.BlockSpec((tm, LANES), lambda i: (i, 0))
    per_tile = pl.BlockSpec((None, ne, 1), lambda i: (i, 0, 0))
    return pl.pallas_call(
        _router_kernel,
        grid=(n_tiles,),
        in_specs=[pl.BlockSpec((tm, d), lambda i: (i, 0)), _full((ne, d)), _full((ne, 1))],
        out_specs=[wide, pl.BlockSpec((None, TOP_K, tm), lambda i: (i, 0, 0)), wide, per_tile, per_tile, _full((ne, 1))],
        out_shape=[jax.ShapeDtypeStruct((t, LANES), I32), jax.ShapeDtypeStruct((n_tiles, TOP_K, tm), I32),
                   jax.ShapeDtypeStruct((t, LANES), F32), jax.ShapeDtypeStruct((n_tiles, ne, 1), F32),
                   jax.ShapeDtypeStruct((n_tiles, ne, 1), F32), jax.ShapeDtypeStruct((ne, 1), F32)],
        scratch_shapes=[pltpu.VMEM((ne, 1), F32)],
        compiler_params=_cparams(("arbitrary",)),
        name="moe_router",
    )(xt, router_w.T, router_bias.reshape(ne, 1))


def _start_segments(tbl, base, ne, make_copy):
    off = 0
    for e in range(ne):
        n = pl.multiple_of(tbl[base + e], SEG_ALIGN)
        g = pl.multiple_of(tbl[base + ne + e], SEG_ALIGN)
        make_copy(off, g, n).start()
        off = pl.multiple_of(off + n, SEG_ALIGN)


def _for_sorted_chunks(chunk, n_chunks, max_chunks):
    for ci in range(max_chunks - 1):
        chunk(ci)

    @pl.when(n_chunks == max_chunks)
    def _():
        chunk(max_chunks - 1)


def _start(cp):
    cp.start()


def _wait(cp):
    cp.wait()


def _dispatch_kernel(fs_ref, fn_ref, tbl, lpt_ref, x_ref, xs_hbm, sorted_buf, zbuf, sem, *, ne):
    i = pl.program_id(0)
    last = pl.num_programs(0) - 1
    slot = lax.rem(i, 2)
    tt = x_ref.shape[0]
    width = _table_width(ne)

    @pl.when(i == 0)
    def _():
        zbuf[...] = jnp.zeros(zbuf.shape, zbuf.dtype)

        def visit(action):
            def body(e, carry):
                n = pl.multiple_of(fn_ref[e], SEG_ALIGN)
                s = pl.multiple_of(fs_ref[e], SEG_ALIGN)

                @pl.when(n > 0)
                def _():
                    action(pltpu.make_async_copy(zbuf.at[pl.ds(0, n), :], xs_hbm.at[pl.ds(s, n), :], sem.at[0]))

                return carry
            return body

        lax.fori_loop(0, ne, visit(_start), 0)
        lax.fori_loop(0, ne, visit(_wait), 0)

    xb = x_ref[...].astype(BF16)
    lpt = lpt_ref[...]
    n_chunks = lax.shift_right_logical(tbl[i * width + 2 * ne] + (SORT_CHUNK - 1), SORT_CHUNK.bit_length() - 1)

    def chunk(ci):
        j0 = ci * SORT_CHUNK
        rows = j0 + lax.broadcasted_iota(I32, (SORT_CHUNK, tt), 0)
        hit = lpt[0:1, :] == rows
        for k in range(1, TOP_K):
            hit = jnp.logical_or(hit, lpt[k:k + 1, :] == rows)
        srt = jnp.dot(jnp.where(hit, 1.0, 0.0).astype(BF16), xb, preferred_element_type=F32)
        sorted_buf[slot, j0:j0 + SORT_CHUNK, :] = _pack_bf16_pairs(srt, holds_bf16=True)

    _for_sorted_chunks(chunk, n_chunks, sorted_buf.shape[1] // SORT_CHUNK)

    def writes(sl):
        return lambda off, g, n: pltpu.make_async_copy(
            sorted_buf.at[sl, pl.ds(off, n), :], xs_hbm.at[pl.ds(g, n), :], sem.at[sl])

    _start_segments(tbl, i * width, ne, writes(slot))

    def wait_all(step, sl):
        tot = pl.multiple_of(tbl[step * width + 2 * ne], SEG_ALIGN)
        pltpu.make_async_copy(sorted_buf.at[sl, pl.ds(0, tot), :], xs_hbm.at[pl.ds(0, tot), :], sem.at[sl]).wait()

    @pl.when(i > 0)
    def _():
        wait_all(i - 1, 1 - slot)

    @pl.when(i == last)
    def _():
        wait_all(i, slot)


def _sorted_rows(tt, ne):
    rows = -(-(tt * TOP_K + ne * SEG_ALIGN) // SORT_CHUNK) * SORT_CHUNK
    assert rows < 2 ** 16
    return rows


def _table_width(ne):
    return 2 * ne + SEG_ALIGN


def _dispatch(xt, lpt, table, fill_start, fill_n, n_rows, tt, br):
    t, d = xt.shape
    ne = fill_start.shape[0]
    grid_spec = pltpu.PrefetchScalarGridSpec(
        num_scalar_prefetch=3,
        grid=(t // tt,),
        in_specs=[pl.BlockSpec((None, TOP_K, tt), lambda i, *_: (i, 0, 0)),
                  pl.BlockSpec((tt, d), lambda i, *_: (i, 0))],
        out_specs=pl.BlockSpec(memory_space=pl.ANY),
        scratch_shapes=[pltpu.VMEM((2, _sorted_rows(tt, ne), d // 2), U32),
                        pltpu.VMEM((br, d // 2), U32), pltpu.SemaphoreType.DMA((2,))],
    )
    return pl.pallas_call(
        functools.partial(_dispatch_kernel, ne=ne),
        grid_spec=grid_spec,
        out_shape=jax.ShapeDtypeStruct((n_rows, d // 2), U32),
        compiler_params=_cparams(("arbitrary",)),
        name="moe_dispatch",
    )(fill_start, fill_n, table, lpt, xt)


def _expert_kernel(be_ref, nu_ref, bv_ref, xs_ref, wg_ref, wu_ref, wd_ref, y_ref, wg_b, wu_b, wd_b):
    i = pl.program_id(0)
    live = i < nu_ref[0]
    valid = bv_ref[i]
    br = xs_ref.shape[0]
    part = min(EXPERT_PART_ROWS, br)

    @pl.when(jnp.logical_and(live, jnp.logical_or(i == 0, be_ref[i] != be_ref[jnp.maximum(i - 1, 0)])))
    def _():
        wg_b[...] = wg_ref[...].astype(BF16)
        wu_b[...] = wu_ref[...].astype(BF16)
        wd_b[...] = wd_ref[...].astype(BF16)

    def ffn(rows):
        first, second = _unpack_bf16_pairs(xs_ref[rows, :])
        xb = jnp.concatenate([first.astype(BF16), second.astype(BF16)], axis=1)
        g = jnp.dot(xb, wg_b[...], preferred_element_type=F32)
        u = jnp.dot(xb, wu_b[...], preferred_element_type=F32)
        h = g * jax.nn.sigmoid(g) * u
        y = jnp.dot(h.astype(BF16), wd_b[...], preferred_element_type=F32)
        y_ref[rows, :] = _pack_bf16_pairs(y)

    whole = valid > br - part

    @pl.when(jnp.logical_and(live, whole))
    def _():
        ffn(slice(None))

    @pl.when(jnp.logical_and(live, jnp.logical_not(whole)))
    def _():
        for r0 in range(0, br - part, part):
            @pl.when(r0 < valid)
            def _():
                ffn(slice(r0, r0 + part))


def _experts(xs, blk_exp, n_used, blk_valid, w_gate, w_up, w_down, layer, br):
    p, dh = xs.shape
    d, f = w_gate.shape[2:]
    n_blocks = p // br

    def row_map(i, be, nu, bv):
        return (jnp.minimum(i, nu[0] - 1), 0)

    def w_map(i, be, nu, bv):
        return (layer, be[jnp.minimum(i, nu[0] - 1)], 0, 0)

    grid_spec = pltpu.PrefetchScalarGridSpec(
        num_scalar_prefetch=3,
        grid=(n_blocks,),
        in_specs=[pl.BlockSpec((br, dh), row_map),
                  pl.BlockSpec((None, None, d, f), w_map),
                  pl.BlockSpec((None, None, d, f), w_map),
                  pl.BlockSpec((None, None, f, d), w_map)],
        out_specs=pl.BlockSpec((br, dh), row_map),
        scratch_shapes=[pltpu.VMEM((d, f), BF16), pltpu.VMEM((d, f), BF16), pltpu.VMEM((f, d), BF16)],
    )
    return pl.pallas_call(
        _expert_kernel,
        grid_spec=grid_spec,
        out_shape=jax.ShapeDtypeStruct((p, dh), U32),
        compiler_params=_cparams(("arbitrary",)),
        name="moe_experts",
    )(blk_exp, n_used, blk_valid, xs, w_gate, w_up, w_down)


def _combine_kernel(tbl, y_hbm, lp_ref, gate_ref, x_ref, sg_ref, su_ref, sd_ref, g_ref, b_ref, o_ref,
                    ysorted, acc, lpb, gb, sem, *, ne, alpha):
    i = pl.program_id(0)
    last = pl.num_programs(0) - 1
    slot = lax.rem(i, 2)
    tt = x_ref.shape[0]
    half = x_ref.shape[1] // 2
    width = _table_width(ne)

    def gathers(sl):
        return lambda off, g, n: pltpu.make_async_copy(
            y_hbm.at[pl.ds(g, n), :], ysorted.at[sl, pl.ds(off, n), :], sem.at[sl])

    @pl.when(i == 0)
    def _():
        _start_segments(tbl, 0, ne, gathers(0))

    nxt = jnp.minimum(i + 1, last)
    _start_segments(tbl, nxt * width, ne, gathers(1 - slot))

    x = x_ref[...]
    xb = x.astype(BF16)
    g = jnp.dot(xb, sg_ref[...], preferred_element_type=F32)
    u = jnp.dot(xb, su_ref[...], preferred_element_type=F32)
    h = g * jax.nn.sigmoid(g) * u
    acc[...] = jnp.dot(h.astype(BF16), sd_ref[...], preferred_element_type=F32)
    gate_bits = lax.bitcast_convert_type(gate_ref[...].astype(BF16).astype(F32), U32)
    word = gate_bits | lax.bitcast_convert_type(lp_ref[...], U32)
    for k in range(TOP_K):
        wb = jnp.broadcast_to(word[:, k:k + 1], (tt, LANES))
        lpb[k] = lax.bitcast_convert_type(wb & jnp.uint32(0xFFFF), I32)
        gb[k] = lax.bitcast_convert_type(wb & jnp.uint32(0xFFFF0000), F32)
    total = pl.multiple_of(tbl[i * width + 2 * ne], SEG_ALIGN)
    pltpu.make_async_copy(y_hbm.at[pl.ds(0, total), :], ysorted.at[slot, pl.ds(0, total), :], sem.at[slot]).wait()
    n_chunks = lax.shift_right_logical(total + (SORT_CHUNK - 1), SORT_CHUNK.bit_length() - 1)
    lane = lax.broadcasted_iota(I32, (tt, LANES), 1)

    def chunk(ci):
        j0 = ci * SORT_CHUNK
        parts = []
        for c0 in range(0, SORT_CHUNK, LANES):
            cols = lane + (j0 + c0)
            gm = jnp.zeros((tt, LANES), F32)
            for k in range(TOP_K):
                gm = jnp.where(lpb[k] == cols, gb[k], gm)
            parts.append(gm)
        gmat = jnp.concatenate(parts, axis=1).astype(BF16)
        rows = j0 + lax.broadcasted_iota(I32, (SORT_CHUNK, half), 0)
        ys = jnp.where(rows < total, ysorted[slot, j0:j0 + SORT_CHUNK, :], jnp.uint32(0))
        first, second = _unpack_bf16_pairs(ys)
        first = first.astype(BF16)
        second = second.astype(BF16)
        acc[:, :half] += jnp.dot(gmat, first, preferred_element_type=F32)
        acc[:, half:] += jnp.dot(gmat, second, preferred_element_type=F32)

    _for_sorted_chunks(chunk, n_chunks, ysorted.shape[1] // SORT_CHUNK)
    o_ref[...] = _ln(alpha * x + acc[...], g_ref[...], b_ref[...])

    @pl.when(i == last)
    def _():
        pltpu.make_async_copy(y_hbm.at[pl.ds(0, total), :], ysorted.at[1 - slot, pl.ds(0, total), :],
                              sem.at[1 - slot]).wait()


def _combine(table, y, lp, gate, xt, sw_gate, sw_up, sw_down, g2, b2, alpha, tt, ne):
    t, d = xt.shape
    f = sw_gate.shape[1]
    row = pl.BlockSpec((tt, d), lambda i, *_: (i, 0))
    wide = pl.BlockSpec((tt, LANES), lambda i, *_: (i, 0))
    grid_spec = pltpu.PrefetchScalarGridSpec(
        num_scalar_prefetch=1,
        grid=(t // tt,),
        in_specs=[pl.BlockSpec(memory_space=pl.ANY), wide, wide, row,
                  _full((d, f)), _full((d, f)), _full((f, d)), _full((1, d)), _full((1, d))],
        out_specs=row,
        scratch_shapes=[pltpu.VMEM((2, _sorted_rows(tt, ne), d // 2), U32),
                        pltpu.VMEM((tt, d), F32), pltpu.VMEM((TOP_K, tt, LANES), I32), pltpu.VMEM((TOP_K, tt, LANES), F32),
                        pltpu.SemaphoreType.DMA((2,))],
    )
    return pl.pallas_call(
        functools.partial(_combine_kernel, ne=ne, alpha=alpha),
        grid_spec=grid_spec,
        out_shape=jax.ShapeDtypeStruct((t, d), F32),
        compiler_params=_cparams(("arbitrary",)),
        name="moe_combine_ln",
    )(table, y, lp, gate, xt, sw_gate, sw_up, sw_down, g2, b2)


def _moe_layer(xt, router_w, router_bias, w_gate, w_up, w_down, layer, sw_gate, sw_up, sw_down, g2, b2, alpha, cfg):
    t, d = xt.shape
    ne = router_w.shape[1]
    br = cfg["moe_rows"]
    tt = cfg["moe_tile"]
    n_tiles = t // tt
    lp, lpt, gate, cnt, base, tot = _router(xt, router_w, router_bias, tt)
    cnt = cnt.reshape(n_tiles, ne).astype(I32)
    tot = tot.reshape(ne).astype(I32)
    padded = (tot + br - 1) // br * br
    pends = jnp.cumsum(padded)
    pstarts = pends - padded
    seg_start = pstarts[None, :] + base.reshape(n_tiles, ne).astype(I32)
    tile_rows = jnp.sum(cnt, axis=1, keepdims=True)
    table = jnp.concatenate([cnt, seg_start, tile_rows, jnp.zeros((n_tiles, SEG_ALIGN - 1), I32)], axis=1).reshape(-1)
    n_blocks = -(-(t * TOP_K + ne * n_tiles * SEG_ALIGN) // br) + ne
    blk_start = jnp.arange(n_blocks, dtype=I32) * br
    blk_exp = jnp.minimum(jnp.sum(pends[None, :] <= blk_start[:, None], axis=1), ne - 1).astype(I32)
    n_used = (pends[-1:] // br).astype(I32)
    blk_valid = jnp.clip((pstarts + tot)[blk_exp] - blk_start, 0, br).astype(I32)
    xs = _dispatch(xt, lpt, table, pstarts + tot, padded - tot, n_blocks * br, tt, br)
    y = _experts(xs, blk_exp, n_used, blk_valid, w_gate, w_up, w_down, layer, br)
    return _combine(table, y, lp, gate, xt, sw_gate.astype(BF16), sw_up.astype(BF16), sw_down.astype(BF16),
                    g2, b2, alpha, tt, ne)


def _config(b, s, d):
    return dict(row_tm=min(512, s), conv_tm=min(1024, s), conv_cw=min(512, d), proj_rows=min(512, s), attn_out_tm=512,
                moe_rows=1024, moe_tile=min(256, b * s))


def kernel(x, positions, ln1_g, ln1_b, ln2_g, ln2_b, conv_w_in, conv_b_in, conv_w_dw, conv_b_dw, conv_ln_g, conv_ln_b, conv_w_out, conv_b_out, w_kv, attn_w_q, attn_w_o, router_w, router_bias, exp_w_gate, exp_w_up, exp_w_down, sh_w_gate, sh_w_up, sh_w_down):
    b, s, d = x.shape
    t = b * s
    depth = ln1_g.shape[0]
    n_conv = conv_w_in.shape[0]
    alpha = (2.0 * depth) ** 0.25
    cfg = _config(b, s, d)
    n_groups = len(WINDOW_DILATIONS)
    gw = attn_w_o.shape[1]
    vec = lambda a: a.reshape(1, -1)

    xt = x.reshape(t, d)
    kv = None
    tables = None
    for layer in range(depth):
        g1, b1 = vec(ln1_g[layer]), vec(ln1_b[layer])
        if layer < n_conv:
            h = _glu(xt, conv_w_in[layer].astype(BF16), vec(conv_b_in[layer]), cfg["row_tm"])
            c = _dwconv(h, conv_w_dw[layer], vec(conv_b_dw[layer]), b, s, cfg["conv_tm"], cfg["conv_cw"])
            xt = _conv_out(c, xt, vec(conv_ln_g[layer]), vec(conv_ln_b[layer]), conv_w_out[layer].astype(BF16),
                           vec(conv_b_out[layer]), g1, b1, alpha, cfg["row_tm"])
        else:
            j = layer - n_conv
            if tables is None:
                cos, sin = _rope_tables(positions, cfg["row_tm"])
                tables = (cos.reshape(b, s, LANES), sin.reshape(b, s, LANES))
            x3 = xt.reshape(b, s, d)
            wq = attn_w_q[j].astype(BF16)
            outs, lses = [], []
            new_kv = []
            for g, (win, dil) in enumerate(WINDOW_DILATIONS):
                assert win // dil == ATTN_BLOCK
                if kv is None:
                    wkv = w_kv.astype(BF16)
                    q, kg, vg = _project(x3, *tables, [(wq, g), (wkv, g), (wkv, n_groups + g)], (True, True, False),
                                         dil, cfg["proj_rows"], gw)
                    new_kv.append((kg, vg))
                else:
                    (q,) = _project(x3, *tables, [(wq, g)], (True,), dil, cfg["proj_rows"], gw)
                    kg, vg = kv[g]
                o, lse = _attention(q, kg, vg, dil, min(1024, s // dil))
                outs.append(o)
                lses.append(lse)
            if kv is None:
                kv = new_kv
            xt = _attn_out(outs, lses, x3, attn_w_o[j].astype(BF16), g1, b1, alpha, cfg["attn_out_tm"]).reshape(t, d)
        xt = _moe_layer(xt, router_w[layer], router_bias[layer], exp_w_gate, exp_w_up, exp_w_down, layer,
                        sh_w_gate[layer], sh_w_up[layer], sh_w_down[layer],
                        vec(ln2_g[layer]), vec(ln2_b[layer]), alpha, cfg)
    return xt.reshape(b, s, d)
```

```python
import functools
import math

import jax
import jax.numpy as jnp
from jax import lax
from jax.experimental import pallas as pl
from jax.experimental.pallas import tpu as pltpu

F32 = jnp.float32
BF16 = jnp.bfloat16
I32 = jnp.int32
U32 = jnp.uint32

LANES = 128
HEAD_DIM = 128
ROT_DIM = HEAD_DIM // 4
ROPE_THETA = 500000.0
ATTN_BLOCK = 128
WINDOW_DILATIONS = ((128, 1), (512, 4), (2048, 16))
NEG_INF = -1e30
TOP_K = 8
ROUTED_SCALE = 2.5
LN_EPS = 1e-5
CONV_HALO = 32
CONV_ROWS = 64
VMEM_LIMIT = 56 * 1024 * 1024
SEG_ALIGN = 8
SORT_CHUNK = 256


def _cparams(sem):
    return pltpu.CompilerParams(dimension_semantics=sem, vmem_limit_bytes=VMEM_LIMIT)


def _ln(y, g, b):
    mu = jnp.mean(y, axis=-1, keepdims=True)
    d = y - mu
    var = jnp.mean(d * d, axis=-1, keepdims=True)
    return d * lax.rsqrt(var + LN_EPS) * g + b


def _full(shape):
    n = len(shape)
    return pl.BlockSpec(shape, lambda *_: (0,) * n)


def _glu_kernel(x_ref, w_ref, b_ref, o_ref):
    d = o_ref.shape[-1]
    h = jnp.dot(x_ref[...].astype(BF16), w_ref[...], preferred_element_type=F32) + b_ref[...]
    o_ref[...] = h[:, :d] * jax.nn.sigmoid(h[:, d:])


def _glu(xt, w_in, b_in, tm):
    t, d = xt.shape
    return pl.pallas_call(
        _glu_kernel,
        grid=(t // tm,),
        in_specs=[pl.BlockSpec((tm, d), lambda i: (i, 0)), _full((d, 2 * d)), _full((1, 2 * d))],
        out_specs=pl.BlockSpec((tm, d), lambda i: (i, 0)),
        out_shape=jax.ShapeDtypeStruct((t, d), F32),
        compiler_params=_cparams(("arbitrary",)),
        name="conv_glu",
    )(xt, w_in, b_in)


def _dwconv_kernel(h_ref, w_ref, b_ref, o_ref, buf, win, *, rows):
    s = pl.program_id(2)
    tm = h_ref.shape[0]
    width = w_ref.shape[0]
    off = CONV_HALO - (width - 1)

    @pl.when(s == 0)
    def _():
        buf[0:CONV_HALO, :] = jnp.zeros((CONV_HALO, buf.shape[1]), F32)

    @pl.when(s > 0)
    def _():
        buf[0:CONV_HALO, :] = buf[tm:tm + CONV_HALO, :]

    buf[CONV_HALO:CONV_HALO + tm, :] = h_ref[...]
    sub = 8
    for c0 in range(0, buf.shape[1], LANES):
        cs = slice(c0, c0 + LANES)
        for r0 in range(0, tm, rows):
            acc = None
            for phase in range(min(sub, width)):
                n_taps = (width - 1 - phase) // sub + 1
                start = r0 + off + phase
                span = rows + sub * (n_taps - 1)
                win[0:span, :] = buf[start:start + span, cs]
                for a in range(n_taps):
                    k = sub * a + phase
                    term = win[sub * a:sub * a + rows, :] * w_ref[k:k + 1, cs]
                    acc = term if acc is None else acc + term
            o_ref[r0:r0 + rows, cs] = acc + b_ref[:, cs]


def _dwconv(h, w_dw, b_dw, b, s, tm, cw):
    t, d = h.shape
    h3 = h.reshape(b, s, d)
    width = w_dw.shape[0]
    out = pl.pallas_call(
        functools.partial(_dwconv_kernel, rows=CONV_ROWS),
        grid=(b, d // cw, s // tm),
        in_specs=[pl.BlockSpec((None, tm, cw), lambda bi, c, si: (bi, si, c)),
                  pl.BlockSpec((width, cw), lambda bi, c, si: (0, c)),
                  pl.BlockSpec((1, cw), lambda bi, c, si: (0, c))],
        out_specs=pl.BlockSpec((None, tm, cw), lambda bi, c, si: (bi, si, c)),
        out_shape=jax.ShapeDtypeStruct((b, s, d), F32),
        scratch_shapes=[pltpu.VMEM((CONV_HALO + tm, cw), F32), pltpu.VMEM((CONV_ROWS + CONV_HALO, LANES), F32)],
        compiler_params=_cparams(("arbitrary", "arbitrary", "arbitrary")),
        name="conv_depthwise",
    )(h3, w_dw, b_dw)
    return out.reshape(t, d)


def _conv_out_kernel(c_ref, x_ref, cg_ref, cb_ref, w_ref, bo_ref, g_ref, b_ref, o_ref, *, alpha):
    u = _ln(c_ref[...], cg_ref[...], cb_ref[...])
    u = u * jax.nn.sigmoid(u)
    mix = jnp.dot(u.astype(BF16), w_ref[...], preferred_element_type=F32) + bo_ref[...]
    o_ref[...] = _ln(alpha * x_ref[...] + mix, g_ref[...], b_ref[...])


def _conv_out(c, xt, cg, cb, w_out, b_out, g1, b1, alpha, tm):
    t, d = xt.shape
    row = pl.BlockSpec((tm, d), lambda i: (i, 0))
    vec = _full((1, d))
    return pl.pallas_call(
        functools.partial(_conv_out_kernel, alpha=alpha),
        grid=(t // tm,),
        in_specs=[row, row, vec, vec, _full((d, d)), vec, vec, vec],
        out_specs=row,
        out_shape=jax.ShapeDtypeStruct((t, d), F32),
        compiler_params=_cparams(("arbitrary",)),
        name="conv_out_ln",
    )(c, xt, cg, cb, w_out, b_out, g1, b1)


def _rope_kernel(pos_ref, invf_ref, c_ref, s_ref):
    ang = pos_ref[...].astype(F32) * invf_ref[...]
    lane = lax.broadcasted_iota(I32, ang.shape, 1)
    half = ROT_DIM // 2
    c_ref[...] = jnp.where(lane < ROT_DIM, jnp.cos(ang), 1.0)
    sn = jnp.sin(ang)
    s_ref[...] = jnp.where(lane < half, -sn, jnp.where(lane < ROT_DIM, sn, 0.0))


def _rope_tables(positions, tm):
    t = positions.size
    half = ROT_DIM // 2
    inv_freq = ROPE_THETA ** (-jnp.arange(half, dtype=F32) * 2.0 / ROT_DIM)
    invf = jnp.zeros((1, LANES), F32).at[0, :half].set(inv_freq).at[0, half:ROT_DIM].set(inv_freq)
    out = jax.ShapeDtypeStruct((t, LANES), F32)
    return pl.pallas_call(
        _rope_kernel,
        grid=(t // tm,),
        in_specs=[pl.BlockSpec((tm, 1), lambda i: (i, 0)), _full((1, LANES))],
        out_specs=[pl.BlockSpec((tm, LANES), lambda i: (i, 0))] * 2,
        out_shape=[out, out],
        compiler_params=_cparams(("arbitrary",)),
        name="rope_tables",
    )(positions.reshape(t, 1), invf)


def _proj_kernel(x_ref, c_ref, s_ref, *rest, rot, heads, dil):
    n_out = len(rot)
    w_refs, o_refs = rest[:n_out], rest[n_out:2 * n_out]
    lhs, tabs, xc = rest[2 * n_out:]
    tm = o_refs[0].shape[1]
    n_lane_tiles = xc.shape[0]
    if dil > 1:
        for c in range(n_lane_tiles):
            xc[c] = x_ref[:, c * LANES:(c + 1) * LANES]
    for r in range(dil):
        rows = slice(r * tm, (r + 1) * tm)
        if dil > 1:
            src = pl.ds(r, tm, stride=dil)
            for c in range(n_lane_tiles):
                lhs[rows, c * LANES:(c + 1) * LANES] = xc[c, src, :].astype(BF16)
        else:
            src = slice(None)
            lhs[rows, :] = x_ref[...].astype(BF16)
        tabs[0, rows, :] = c_ref[src, :]
        tabs[1, rows, :] = s_ref[src, :]
    cos = tabs[0]
    sin = tabs[1]
    lane = lax.broadcasted_iota(I32, cos.shape, 1)
    first = lane < ROT_DIM // 2
    for j, o_ref in enumerate(o_refs):
        y = jnp.dot(lhs[...], w_refs[j][...], preferred_element_type=F32)
        for h in range(heads):
            t = y[:, h * HEAD_DIM:(h + 1) * HEAD_DIM]
            if rot[j]:
                partner = jnp.where(first, pltpu.roll(t, HEAD_DIM - ROT_DIM // 2, 1), pltpu.roll(t, ROT_DIM // 2, 1))
                t = t * cos + partner * sin
            t = t.astype(BF16)
            for r in range(dil):
                o_ref[r, :, h * HEAD_DIM:(h + 1) * HEAD_DIM] = t[r * tm:(r + 1) * tm]


def _project(x3, cos3, sin3, weights, rot, dil, rows, gw):
    b, s, d = x3.shape
    l = s // dil
    tm = rows // dil
    n_out = len(rot)
    out = jax.ShapeDtypeStruct((b, dil, l, gw), BF16)
    tab = pl.BlockSpec((None, rows, LANES), lambda bi, i: (bi, i, 0))
    w_specs = [pl.BlockSpec((d, gw), lambda bi, i, blk=blk: (0, blk)) for _, blk in weights]
    return pl.pallas_call(
        functools.partial(_proj_kernel, rot=rot, heads=gw // HEAD_DIM, dil=dil),
        grid=(b, s // rows),
        in_specs=[pl.BlockSpec((None, rows, d), lambda bi, i: (bi, i, 0)), tab, tab] + w_specs,
        out_specs=[pl.BlockSpec((None, dil, tm, gw), lambda bi, i: (bi, 0, i, 0))] * n_out,
        out_shape=[out] * n_out,
        scratch_shapes=[pltpu.VMEM((rows, d), BF16), pltpu.VMEM((2, rows, LANES), F32),
                        pltpu.VMEM((d // LANES, rows, LANES), F32)],
        compiler_params=_cparams(("arbitrary", "arbitrary")),
        name=f"proj_dil{dil}_n{n_out}",
    )(x3, cos3, sin3, *[w for w, _ in weights])


def _attn_kernel(q_ref, kc_ref, kh_ref, vc_ref, vh_ref, o_ref, lse_ref, kbuf, vbuf, *, heads):
    i = pl.program_id(2)
    tq = q_ref.shape[0]
    blk = ATTN_BLOCK
    kbuf[0:blk, :] = kh_ref[...]
    kbuf[blk:blk + tq, :] = kc_ref[...]
    vbuf[0:blk, :] = vh_ref[...]
    vbuf[blk:blk + tq, :] = vc_ref[...]
    scale = 1.0 / math.sqrt(HEAD_DIM)
    row = lax.broadcasted_iota(I32, (blk, 2 * blk), 0)
    col = lax.broadcasted_iota(I32, (blk, 2 * blk), 1)
    band = jnp.logical_and(col >= row, col <= row + blk)
    first_band = jnp.logical_and(band, jnp.logical_or(col >= blk, i > 0))
    lane = lax.broadcasted_iota(I32, (blk, LANES), 1)
    dims = (((1,), (1,)), ((), ()))
    for n in range(tq // blk):
        rs = slice(n * blk, (n + 1) * blk)
        ks = slice(n * blk, (n + 2) * blk)
        mask = first_band if n == 0 else band
        lse_tile = jnp.zeros((blk, LANES), F32)
        for h in range(heads):
            cs = slice(h * HEAD_DIM, (h + 1) * HEAD_DIM)
            s = lax.dot_general(q_ref[rs, cs], kbuf[ks, cs], dims, preferred_element_type=F32) * scale
            s = jnp.where(mask, s, NEG_INF)
            m = jnp.max(s, axis=1, keepdims=True)
            p = jnp.exp(s - m)
            den = jnp.sum(p, axis=1, keepdims=True)
            o = jnp.dot(p.astype(BF16), vbuf[ks, cs], preferred_element_type=F32)
            o_ref[rs, cs] = (o / den).astype(o_ref.dtype)
            lse_tile = jnp.where(lane == h, m + jnp.log(den), lse_tile)
        lse_ref[rs, :] = lse_tile


def _attention(q, k, v, dil, tq):
    b, _, l, gw = q.shape
    nb = tq // ATTN_BLOCK
    cur = pl.BlockSpec((None, None, tq, gw), lambda bi, r, i: (bi, r, i, 0))
    halo = pl.BlockSpec((None, None, ATTN_BLOCK, gw), lambda bi, r, i: (bi, r, jnp.maximum(i * nb - 1, 0), 0))
    o, lse = pl.pallas_call(
        functools.partial(_attn_kernel, heads=gw // HEAD_DIM),
        grid=(b, dil, l // tq),
        in_specs=[cur, cur, halo, cur, halo],
        out_specs=[pl.BlockSpec((None, None, tq, gw), lambda bi, r, i: (bi, r, i, 0)),
                   pl.BlockSpec((None, None, tq, LANES), lambda bi, r, i: (bi, r, i, 0))],
        out_shape=[jax.ShapeDtypeStruct((b, dil, l, gw), BF16),
                   jax.ShapeDtypeStruct((b, dil, l, LANES), F32)],
        scratch_shapes=[pltpu.VMEM((ATTN_BLOCK + tq, gw), BF16), pltpu.VMEM((ATTN_BLOCK + tq, gw), BF16)],
        compiler_params=_cparams(("arbitrary", "arbitrary", "arbitrary")),
        name=f"attn_dil{dil}",
    )(q, k, k, v, v)
    return o, lse


def _attn_out_kernel(o0_ref, o1_ref, o2_ref, l0_ref, l1_ref, l2_ref, x_ref, w_ref, g_ref, b_ref,
                     out_ref, onat, lnat, mixed, *, alpha, heads, dils):
    tm = x_ref.shape[0]
    for gi, (o_ref, l_ref) in enumerate(((o0_ref, l0_ref), (o1_ref, l1_ref), (o2_ref, l2_ref))):
        dil = dils[gi]
        for r in range(dil):
            dst = pl.ds(r, tm // dil, stride=dil) if dil > 1 else slice(None)
            for h in range(heads):
                onat[gi * heads + h, dst, :] = o_ref[r, :, h * HEAD_DIM:(h + 1) * HEAD_DIM].astype(F32)
            lnat[gi, dst, :] = l_ref[r]
    l0, l1, l2 = lnat[0], lnat[1], lnat[2]
    m = jnp.maximum(jnp.maximum(l0, l1), l2)
    e0, e1, e2 = jnp.exp(l0 - m), jnp.exp(l1 - m), jnp.exp(l2 - m)
    den = e0 + e1 + e2
    w0, w1, w2 = e0 / den, e1 / den, e2 / den
    for h in range(heads):
        cs = slice(h * HEAD_DIM, (h + 1) * HEAD_DIM)
        o = w0[:, h:h + 1] * onat[h] + w1[:, h:h + 1] * onat[heads + h] + w2[:, h:h + 1] * onat[2 * heads + h]
        mixed[:, cs] = o.astype(BF16)
    mix = jnp.dot(mixed[...], w_ref[...], preferred_element_type=F32)
    out_ref[...] = _ln(alpha * x_ref[...] + mix, g_ref[...], b_ref[...])


def _attn_out(outs, lses, x3, w_o, g1, b1, alpha, tm):
    b, s, d = x3.shape
    gw = w_o.shape[0]
    dils = tuple(o.shape[1] for o in outs)
    ospecs = [pl.BlockSpec((None, dil, tm // dil, gw), lambda bi, i: (bi, 0, i, 0)) for dil in dils]
    lspecs = [pl.BlockSpec((None, dil, tm // dil, LANES), lambda bi, i: (bi, 0, i, 0)) for dil in dils]
    xrow = pl.BlockSpec((None, tm, d), lambda bi, i: (bi, i, 0))
    n = len(dils)
    return pl.pallas_call(
        functools.partial(_attn_out_kernel, alpha=alpha, heads=gw // HEAD_DIM, dils=dils),
        grid=(b, s // tm),
        in_specs=ospecs + lspecs + [xrow, _full((gw, d)), _full((1, d)), _full((1, d))],
        out_specs=xrow,
        out_shape=jax.ShapeDtypeStruct((b, s, d), F32),
        scratch_shapes=[pltpu.VMEM((n * gw // HEAD_DIM, tm, HEAD_DIM), F32), pltpu.VMEM((n, tm, LANES), F32),
                        pltpu.VMEM((tm, gw), BF16)],
        compiler_params=_cparams(("arbitrary", "arbitrary")),
        name="attn_out_ln",
    )(*outs, *lses, x3, w_o, g1, b1)


def _pack_bf16_pairs(v, holds_bf16=False):
    half = v.shape[1] // 2
    first, second = v[:, :half], v[:, half:]
    if not holds_bf16:
        first, second = first.astype(BF16).astype(F32), second.astype(BF16).astype(F32)
    return lax.bitcast_convert_type(first, U32) | (lax.bitcast_convert_type(second, U32) >> 16)


def _unpack_bf16_pairs(p):
    first = lax.bitcast_convert_type(p & jnp.uint32(0xFFFF0000), F32)
    second = lax.bitcast_convert_type(p << 16, F32)
    return first, second


def _router_kernel(x_ref, rwt_ref, rb_ref, lp_ref, lpt_ref, gate_ref, cnt_ref, base_ref, tot_ref, carry):
    @pl.when(pl.program_id(0) == 0)
    def _():
        carry[...] = jnp.zeros(carry.shape, F32)

    tm = x_ref.shape[0]
    ne = rwt_ref.shape[0]
    dims = (((1,), (1,)), ((), ()))
    w = rwt_ref[...]
    x = x_ref[...]
    w_hi = w.astype(BF16)
    w_lo = (w - w_hi.astype(F32)).astype(BF16)
    x_hi = x.astype(BF16)
    x_lo = (x - x_hi.astype(F32)).astype(BF16)
    logits = (lax.dot_general(w_hi, x_hi, dims, preferred_element_type=F32)
              + (lax.dot_general(w_lo, x_hi, dims, preferred_element_type=F32)
                 + lax.dot_general(w_hi, x_lo, dims, preferred_element_type=F32)))
    scores = jax.nn.sigmoid(logits)
    sel = scores + rb_ref[...]
    eidx = lax.broadcasted_iota(I32, (ne, tm), 0)
    chosen = jnp.zeros((ne, tm), F32)
    gsum = jnp.zeros((1, tm), F32)
    picks = []
    for _ in range(TOP_K):
        best = jnp.max(sel, axis=0, keepdims=True)
        j = jnp.min(jnp.where(sel == best, eidx, ne), axis=0, keepdims=True)
        onehot = eidx == j
        sc = jnp.sum(jnp.where(onehot, scores, 0.0), axis=0, keepdims=True)
        sel = jnp.where(onehot, -jnp.inf, sel)
        chosen = chosen + onehot.astype(F32)
        gsum = gsum + sc
        picks.append((onehot, sc))
    r = lax.broadcasted_iota(I32, (tm, tm), 0)
    c = lax.broadcasted_iota(I32, (tm, tm), 1)
    rank = jnp.dot(chosen.astype(BF16), (r < c).astype(BF16), preferred_element_type=F32)
    cnt = jnp.sum(chosen, axis=1, keepdims=True)
    cnt_al = jnp.maximum(jnp.floor((cnt + (SEG_ALIGN - 1)) / SEG_ALIGN), 1.0) * SEG_ALIGN
    er = lax.broadcasted_iota(I32, (ne, ne), 0)
    ec = lax.broadcasted_iota(I32, (ne, ne), 1)
    seg_off = jnp.dot((ec < er).astype(BF16), jnp.broadcast_to(cnt_al, (ne, LANES)).astype(BF16),
                      preferred_element_type=F32)[:, 0:1]
    slot = rank + seg_off
    krow = lax.broadcasted_iota(I32, (TOP_K, tm), 0)
    lpt = jnp.zeros((TOP_K, tm), I32)
    gt = jnp.zeros((TOP_K, tm), F32)
    for k, (onehot, sc) in enumerate(picks):
        row = jnp.sum(jnp.where(onehot, slot, 0.0), axis=0, keepdims=True)
        lpt = jnp.where(krow == k, row.astype(I32), lpt)
        gt = jnp.where(krow == k, sc / gsum * ROUTED_SCALE, gt)
    lpt_ref[...] = lpt
    lp_ref[...] = jnp.concatenate([lpt, jnp.zeros((LANES - TOP_K, tm), I32)], axis=0).T
    gate_ref[...] = jnp.concatenate([gt, jnp.zeros((LANES - TOP_K, tm), F32)], axis=0).T
    cnt_ref[...] = cnt_al
    base_ref[...] = carry[...]
    carry[...] = carry[...] + cnt_al
    tot_ref[...] = carry[...]


def _router(xt, router_w, router_bias, tm):
    t, d = xt.shape
    ne = router_w.shape[1]
    n_tiles = t // tm
    wide = pl.BlockSpec((tm, LANES), lambda i: (i, 0))
    per_tile = pl.BlockSpec((None, ne, 1), lambda i: (i, 0, 0))
    return pl.pallas_call(
        _router_kernel,
        grid=(n_tiles,),
        in_specs=[pl.BlockSpec((tm, d), lambda i: (i, 0)), _full((ne, d)), _full((ne, 1))],
        out_specs=[wide, pl.BlockSpec((None, TOP_K, tm), lambda i: (i, 0, 0)), wide, per_tile, per_tile, _full((ne, 1))],
        out_shape=[jax.ShapeDtypeStruct((t, LANES), I32), jax.ShapeDtypeStruct((n_tiles, TOP_K, tm), I32),
                   jax.ShapeDtypeStruct((t, LANES), F32), jax.ShapeDtypeStruct((n_tiles, ne, 1), F32),
                   jax.ShapeDtypeStruct((n_tiles, ne, 1), F32), jax.ShapeDtypeStruct((ne, 1), F32)],
        scratch_shapes=[pltpu.VMEM((ne, 1), F32)],
        compiler_params=_cparams(("arbitrary",)),
        name="moe_router",
    )(xt, router_w.T, router_bias.reshape(ne, 1))


def _start_segments(tbl, base, ne, make_copy):
    off = 0
    for e in range(ne):
        n = pl.multiple_of(tbl[base + e], SEG_ALIGN)
        g = pl.multiple_of(tbl[base + ne + e], SEG_ALIGN)
        make_copy(off, g, n).start()
        off = pl.multiple_of(off + n, SEG_ALIGN)


def _for_sorted_chunks(chunk, n_chunks, max_chunks):
    for ci in range(max_chunks - 1):
        chunk(ci)

    @pl.when(n_chunks == max_chunks)
    def _():
        chunk(max_chunks - 1)


def _start(cp):
    cp.start()


def _wait(cp):
    cp.wait()


def _dispatch_kernel(fs_ref, fn_ref, tbl, lpt_ref, x_ref, xs_hbm, sorted_buf, zbuf, sem, *, ne):
    i = pl.program_id(0)
    last = pl.num_programs(0) - 1
    slot = lax.rem(i, 2)
    tt = x_ref.shape[0]
    width = _table_width(ne)

    @pl.when(i == 0)
    def _():
        zbuf[...] = jnp.zeros(zbuf.shape, zbuf.dtype)

        def visit(action):
            def body(e, carry):
                n = pl.multiple_of(fn_ref[e], SEG_ALIGN)
                s = pl.multiple_of(fs_ref[e], SEG_ALIGN)

                @pl.when(n > 0)
                def _():
                    action(pltpu.make_async_copy(zbuf.at[pl.ds(0, n), :], xs_hbm.at[pl.ds(s, n), :], sem.at[0]))

                return carry
            return body

        lax.fori_loop(0, ne, visit(_start), 0)
        lax.fori_loop(0, ne, visit(_wait), 0)

    xb = x_ref[...].astype(BF16)
    lpt = lpt_ref[...]
    n_chunks = lax.shift_right_logical(tbl[i * width + 2 * ne] + (SORT_CHUNK - 1), SORT_CHUNK.bit_length() - 1)

    def chunk(ci):
        j0 = ci * SORT_CHUNK
        rows = j0 + lax.broadcasted_iota(I32, (SORT_CHUNK, tt), 0)
        hit = lpt[0:1, :] == rows
        for k in range(1, TOP_K):
            hit = jnp.logical_or(hit, lpt[k:k + 1, :] == rows)
        srt = jnp.dot(jnp.where(hit, 1.0, 0.0).astype(BF16), xb, preferred_element_type=F32)
        sorted_buf[slot, j0:j0 + SORT_CHUNK, :] = _pack_bf16_pairs(srt, holds_bf16=True)

    _for_sorted_chunks(chunk, n_chunks, sorted_buf.shape[1] // SORT_CHUNK)

    def writes(sl):
        return lambda off, g, n: pltpu.make_async_copy(
            sorted_buf.at[sl, pl.ds(off, n), :], xs_hbm.at[pl.ds(g, n), :], sem.at[sl])

    _start_segments(tbl, i * width, ne, writes(slot))

    def wait_all(step, sl):
        tot = pl.multiple_of(tbl[step * width + 2 * ne], SEG_ALIGN)
        pltpu.make_async_copy(sorted_buf.at[sl, pl.ds(0, tot), :], xs_hbm.at[pl.ds(0, tot), :], sem.at[sl]).wait()

    @pl.when(i > 0)
    def _():
        wait_all(i - 1, 1 - slot)

    @pl.when(i == last)
    def _():
        wait_all(i, slot)


def _sorted_rows(tt, ne):
    rows = -(-(tt * TOP_K + ne * SEG_ALIGN) // SORT_CHUNK) * SORT_CHUNK
    assert rows < 2 ** 16
    return rows


def _table_width(ne):
    return 2 * ne + SEG_ALIGN


def _dispatch(xt, lpt, table, fill_start, fill_n, n_rows, tt, br):
    t, d = xt.shape
    ne = fill_start.shape[0]
    grid_spec = pltpu.PrefetchScalarGridSpec(
        num_scalar_prefetch=3,
        grid=(t // tt,),
        in_specs=[pl.BlockSpec((None, TOP_K, tt), lambda i, *_: (i, 0, 0)),
                  pl.BlockSpec((tt, d), lambda i, *_: (i, 0))],
        out_specs=pl.BlockSpec(memory_space=pl.ANY),
        scratch_shapes=[pltpu.VMEM((2, _sorted_rows(tt, ne), d // 2), U32),
                        pltpu.VMEM((br, d // 2), U32), pltpu.SemaphoreType.DMA((2,))],
    )
    return pl.pallas_call(
        functools.partial(_dispatch_kernel, ne=ne),
        grid_spec=grid_spec,
        out_shape=jax.ShapeDtypeStruct((n_rows, d // 2), U32),
        compiler_params=_cparams(("arbitrary",)),
        name="moe_dispatch",
    )(fill_start, fill_n, table, lpt, xt)


EXPERT_RING = 3


def _expert_kernel(be_ref, nu_ref, xs_hbm, wg_ref, wu_ref, wd_ref, y_hbm, xbuf, ybuf, wg_b, wu_b, wd_b, sem_in, sem_out):
    i = pl.program_id(0)
    nu = nu_ref[0]
    live = i < nu
    br = xbuf.shape[1]

    def read(blk):
        return pltpu.make_async_copy(xs_hbm.at[pl.ds(pl.multiple_of(blk * br, br), br), :],
                                     xbuf.at[lax.rem(blk, EXPERT_RING)], sem_in.at[lax.rem(blk, EXPERT_RING)])

    def write(blk):
        return pltpu.make_async_copy(ybuf.at[lax.rem(blk, EXPERT_RING)],
                                     y_hbm.at[pl.ds(pl.multiple_of(blk * br, br), br), :],
                                     sem_out.at[lax.rem(blk, EXPERT_RING)])

    @pl.when(i == 0)
    def _():
        for ahead in range(EXPERT_RING - 1):
            @pl.when(ahead < nu)
            def _():
                read(ahead).start()

    @pl.when(i + (EXPERT_RING - 1) < nu)
    def _():
        read(i + (EXPERT_RING - 1)).start()

    @pl.when(jnp.logical_and(live, jnp.logical_or(i == 0, be_ref[i] != be_ref[jnp.maximum(i - 1, 0)])))
    def _():
        wg_b[...] = wg_ref[...].astype(BF16)
        wu_b[...] = wu_ref[...].astype(BF16)
        wd_b[...] = wd_ref[...].astype(BF16)

    @pl.when(live)
    def _():
        slot = lax.rem(i, EXPERT_RING)
        read(i).wait()

        @pl.when(i >= EXPERT_RING)
        def _():
            write(i - EXPERT_RING).wait()

        first, second = _unpack_bf16_pairs(xbuf[slot])
        xb = jnp.concatenate([first.astype(BF16), second.astype(BF16)], axis=1)
        g = jnp.dot(xb, wg_b[...], preferred_element_type=F32)
        u = jnp.dot(xb, wu_b[...], preferred_element_type=F32)
        h = g * jax.nn.sigmoid(g) * u
        y = jnp.dot(h.astype(BF16), wd_b[...], preferred_element_type=F32)
        ybuf[slot] = _pack_bf16_pairs(y)
        write(i).start()

        @pl.when(i == nu - 1)
        def _():
            for back in range(EXPERT_RING):
                @pl.when(i - back >= 0)
                def _():
                    write(i - back).wait()


def _experts(xs, blk_exp, n_used, w_gate, w_up, w_down, layer, br):
    p, dh = xs.shape
    d, f = w_gate.shape[2:]
    n_blocks = p // br

    def w_map(i, be, nu):
        return (layer, be[jnp.minimum(i, nu[0] - 1)], 0, 0)

    grid_spec = pltpu.PrefetchScalarGridSpec(
        num_scalar_prefetch=2,
        grid=(n_blocks,),
        in_specs=[pl.BlockSpec(memory_space=pl.ANY),
                  pl.BlockSpec((None, None, d, f), w_map),
                  pl.BlockSpec((None, None, d, f), w_map),
                  pl.BlockSpec((None, None, f, d), w_map)],
        out_specs=pl.BlockSpec(memory_space=pl.ANY),
        scratch_shapes=[pltpu.VMEM((EXPERT_RING, br, dh), U32), pltpu.VMEM((EXPERT_RING, br, dh), U32),
                        pltpu.VMEM((d, f), BF16), pltpu.VMEM((d, f), BF16), pltpu.VMEM((f, d), BF16),
                        pltpu.SemaphoreType.DMA((EXPERT_RING,)), pltpu.SemaphoreType.DMA((EXPERT_RING,))],
    )
    return pl.pallas_call(
        _expert_kernel,
        grid_spec=grid_spec,
        out_shape=jax.ShapeDtypeStruct((p, dh), U32),
        compiler_params=_cparams(("arbitrary",)),
        name="moe_experts",
    )(blk_exp, n_used, xs, w_gate, w_up, w_down)


def _combine_kernel(tbl, y_hbm, lp_ref, gate_ref, x_ref, sg_ref, su_ref, sd_ref, g_ref, b_ref, o_ref,
                    ysorted, acc, lpb, gb, sem, *, ne, alpha):
    i = pl.program_id(0)
    last = pl.num_programs(0) - 1
    slot = lax.rem(i, 2)
    tt = x_ref.shape[0]
    half = x_ref.shape[1] // 2
    width = _table_width(ne)

    def gathers(sl):
        return lambda off, g, n: pltpu.make_async_copy(
            y_hbm.at[pl.ds(g, n), :], ysorted.at[sl, pl.ds(off, n), :], sem.at[sl])

    @pl.when(i == 0)
    def _():
        _start_segments(tbl, 0, ne, gathers(0))

    nxt = jnp.minimum(i + 1, last)
    _start_segments(tbl, nxt * width, ne, gathers(1 - slot))

    x = x_ref[...]
    xb = x.astype(BF16)
    g = jnp.dot(xb, sg_ref[...], preferred_element_type=F32)
    u = jnp.dot(xb, su_ref[...], preferred_element_type=F32)
    h = g * jax.nn.sigmoid(g) * u
    acc[...] = jnp.dot(h.astype(BF16), sd_ref[...], preferred_element_type=F32)
    gate_bits = lax.bitcast_convert_type(gate_ref[...].astype(BF16).astype(F32), U32)
    word = gate_bits | lax.bitcast_convert_type(lp_ref[...], U32)
    for k in range(TOP_K):
        wb = jnp.broadcast_to(word[:, k:k + 1], (tt, LANES))
        lpb[k] = lax.bitcast_convert_type(wb & jnp.uint32(0xFFFF), I32)
        gb[k] = lax.bitcast_convert_type(wb & jnp.uint32(0xFFFF0000), F32)
    total = pl.multiple_of(tbl[i * width + 2 * ne], SEG_ALIGN)
    pltpu.make_async_copy(y_hbm.at[pl.ds(0, total), :], ysorted.at[slot, pl.ds(0, total), :], sem.at[slot]).wait()
    n_chunks = lax.shift_right_logical(total + (SORT_CHUNK - 1), SORT_CHUNK.bit_length() - 1)
    lane = lax.broadcasted_iota(I32, (tt, LANES), 1)

    def chunk(ci):
        j0 = ci * SORT_CHUNK
        parts = []
        for c0 in range(0, SORT_CHUNK, LANES):
            cols = lane + (j0 + c0)
            gm = jnp.zeros((tt, LANES), F32)
            for k in range(TOP_K):
                gm = jnp.where(lpb[k] == cols, gb[k], gm)
            parts.append(gm)
        gmat = jnp.concatenate(parts, axis=1).astype(BF16)
        rows = j0 + lax.broadcasted_iota(I32, (SORT_CHUNK, half), 0)
        ys = jnp.where(rows < total, ysorted[slot, j0:j0 + SORT_CHUNK, :], jnp.uint32(0))
        first, second = _unpack_bf16_pairs(ys)
        first = first.astype(BF16)
        second = second.astype(BF16)
        acc[:, :half] += jnp.dot(gmat, first, preferred_element_type=F32)
        acc[:, half:] += jnp.dot(gmat, second, preferred_element_type=F32)

    _for_sorted_chunks(chunk, n_chunks, ysorted.shape[1] // SORT_CHUNK)
    o_ref[...] = _ln(alpha * x + acc[...], g_ref[...], b_ref[...])

    @pl.when(i == last)
    def _():
        pltpu.make_async_copy(y_hbm.at[pl.ds(0, total), :], ysorted.at[1 - slot, pl.ds(0, total), :],
                              sem.at[1 - slot]).wait()


def _combine(table, y, lp, gate, xt, sw_gate, sw_up, sw_down, g2, b2, alpha, tt, ne):
    t, d = xt.shape
    f = sw_gate.shape[1]
    row = pl.BlockSpec((tt, d), lambda i, *_: (i, 0))
    wide = pl.BlockSpec((tt, LANES), lambda i, *_: (i, 0))
    grid_spec = pltpu.PrefetchScalarGridSpec(
        num_scalar_prefetch=1,
        grid=(t // tt,),
        in_specs=[pl.BlockSpec(memory_space=pl.ANY), wide, wide, row,
                  _full((d, f)), _full((d, f)), _full((f, d)), _full((1, d)), _full((1, d))],
        out_specs=row,
        scratch_shapes=[pltpu.VMEM((2, _sorted_rows(tt, ne), d // 2), U32),
                        pltpu.VMEM((tt, d), F32), pltpu.VMEM((TOP_K, tt, LANES), I32), pltpu.VMEM((TOP_K, tt, LANES), F32),
                        pltpu.SemaphoreType.DMA((2,))],
    )
    return pl.pallas_call(
        functools.partial(_combine_kernel, ne=ne, alpha=alpha),
        grid_spec=grid_spec,
        out_shape=jax.ShapeDtypeStruct((t, d), F32),
        compiler_params=_cparams(("arbitrary",)),
        name="moe_combine_ln",
    )(table, y, lp, gate, xt, sw_gate, sw_up, sw_down, g2, b2)


def _moe_layer(xt, router_w, router_bias, w_gate, w_up, w_down, layer, sw_gate, sw_up, sw_down, g2, b2, alpha, cfg):
    t, d = xt.shape
    ne = router_w.shape[1]
    br = cfg["moe_rows"]
    tt = cfg["moe_tile"]
    n_tiles = t // tt
    lp, lpt, gate, cnt, base, tot = _router(xt, router_w, router_bias, tt)
    cnt = cnt.reshape(n_tiles, ne).astype(I32)
    tot = tot.reshape(ne).astype(I32)
    padded = (tot + br - 1) // br * br
    pends = jnp.cumsum(padded)
    pstarts = pends - padded
    seg_start = pstarts[None, :] + base.reshape(n_tiles, ne).astype(I32)
    tile_rows = jnp.sum(cnt, axis=1, keepdims=True)
    table = jnp.concatenate([cnt, seg_start, tile_rows, jnp.zeros((n_tiles, SEG_ALIGN - 1), I32)], axis=1).reshape(-1)
    n_blocks = -(-(t * TOP_K + ne * n_tiles * SEG_ALIGN) // br) + ne
    blk_start = jnp.arange(n_blocks, dtype=I32) * br
    blk_exp = jnp.minimum(jnp.sum(pends[None, :] <= blk_start[:, None], axis=1), ne - 1).astype(I32)
    n_used = (pends[-1:] // br).astype(I32)
    xs = _dispatch(xt, lpt, table, pstarts + tot, padded - tot, n_blocks * br, tt, br)
    y = _experts(xs, blk_exp, n_used, w_gate, w_up, w_down, layer, br)
    return _combine(table, y, lp, gate, xt, sw_gate.astype(BF16), sw_up.astype(BF16), sw_down.astype(BF16),
                    g2, b2, alpha, tt, ne)


def _config(b, s, d):
    return dict(row_tm=min(512, s), conv_tm=min(1024, s), conv_cw=min(512, d), proj_rows=min(512, s), attn_out_tm=512,
                moe_rows=1024, moe_tile=min(256, b * s))


def kernel(x, positions, ln1_g, ln1_b, ln2_g, ln2_b, conv_w_in, conv_b_in, conv_w_dw, conv_b_dw, conv_ln_g, conv_ln_b, conv_w_out, conv_b_out, w_kv, attn_w_q, attn_w_o, router_w, router_bias, exp_w_gate, exp_w_up, exp_w_down, sh_w_gate, sh_w_up, sh_w_down):
    b, s, d = x.shape
    t = b * s
    depth = ln1_g.shape[0]
    n_conv = conv_w_in.shape[0]
    alpha = (2.0 * depth) ** 0.25
    cfg = _config(b, s, d)
    n_groups = len(WINDOW_DILATIONS)
    gw = attn_w_o.shape[1]
    vec = lambda a: a.reshape(1, -1)

    xt = x.reshape(t, d)
    kv = None
    tables = None
    for layer in range(depth):
        g1, b1 = vec(ln1_g[layer]), vec(ln1_b[layer])
        if layer < n_conv:
            h = _glu(xt, conv_w_in[layer].astype(BF16), vec(conv_b_in[layer]), cfg["row_tm"])
            c = _dwconv(h, conv_w_dw[layer], vec(conv_b_dw[layer]), b, s, cfg["conv_tm"], cfg["conv_cw"])
            xt = _conv_out(c, xt, vec(conv_ln_g[layer]), vec(conv_ln_b[layer]), conv_w_out[layer].astype(BF16),
                           vec(conv_b_out[layer]), g1, b1, alpha, cfg["row_tm"])
        else:
            j = layer - n_conv
            if tables is None:
                cos, sin = _rope_tables(positions, cfg["row_tm"])
                tables = (cos.reshape(b, s, LANES), sin.reshape(b, s, LANES))
            x3 = xt.reshape(b, s, d)
            wq = attn_w_q[j].astype(BF16)
            outs, lses = [], []
            new_kv = []
            for g, (win, dil) in enumerate(WINDOW_DILATIONS):
                assert win // dil == ATTN_BLOCK
                if kv is None:
                    wkv = w_kv.astype(BF16)
                    q, kg, vg = _project(x3, *tables, [(wq, g), (wkv, g), (wkv, n_groups + g)], (True, True, False),
                                         dil, cfg["proj_rows"], gw)
                    new_kv.append((kg, vg))
                else:
                    (q,) = _project(x3, *tables, [(wq, g)], (True,), dil, cfg["proj_rows"], gw)
                    kg, vg = kv[g]
                o, lse = _attention(q, kg, vg, dil, min(1024, s // dil))
                outs.append(o)
                lses.append(lse)
            if kv is None:
                kv = new_kv
            xt = _attn_out(outs, lses, x3, attn_w_o[j].astype(BF16), g1, b1, alpha, cfg["attn_out_tm"]).reshape(t, d)
        xt = _moe_layer(xt, router_w[layer], router_bias[layer], exp_w_gate, exp_w_up, exp_w_down, layer,
                        sh_w_gate[layer], sh_w_up[layer], sh_w_down[layer],
                        vec(ln2_g[layer]), vec(ln2_b[layer]), alpha, cfg)
    return xt.reshape(b, s, d)
```

```python
import functools
import math

import jax
import jax.numpy as jnp
from jax import lax
from jax.experimental import pallas as pl
from jax.experimental.pallas import tpu as pltpu

F32 = jnp.float32
BF16 = jnp.bfloat16
I32 = jnp.int32
U32 = jnp.uint32

LANES = 128
HEAD_DIM = 128
ROT_DIM = HEAD_DIM // 4
ROPE_THETA = 500000.0
ATTN_BLOCK = 128
WINDOW_DILATIONS = ((128, 1), (512, 4), (2048, 16))
NEG_INF = -1e30
TOP_K = 8
ROUTED_SCALE = 2.5
LN_EPS = 1e-5
CONV_HALO = 32
CONV_ROWS = 64
VMEM_LIMIT = 56 * 1024 * 1024
SEG_ALIGN = 8
SORT_CHUNK = 256


def _cparams(sem):
    return pltpu.CompilerParams(dimension_semantics=sem, vmem_limit_bytes=VMEM_LIMIT)


def _ln(y, g, b):
    mu = jnp.mean(y, axis=-1, keepdims=True)
    d = y - mu
    var = jnp.mean(d * d, axis=-1, keepdims=True)
    return d * lax.rsqrt(var + LN_EPS) * g + b


def _full(shape):
    n = len(shape)
    return pl.BlockSpec(shape, lambda *_: (0,) * n)


def _glu_kernel(x_ref, w_ref, b_ref, o_ref):
    d = o_ref.shape[-1]
    h = jnp.dot(x_ref[...].astype(BF16), w_ref[...], preferred_element_type=F32) + b_ref[...]
    o_ref[...] = h[:, :d] * jax.nn.sigmoid(h[:, d:])


def _glu(xt, w_in, b_in, tm):
    t, d = xt.shape
    return pl.pallas_call(
        _glu_kernel,
        grid=(t // tm,),
        in_specs=[pl.BlockSpec((tm, d), lambda i: (i, 0)), _full((d, 2 * d)), _full((1, 2 * d))],
        out_specs=pl.BlockSpec((tm, d), lambda i: (i, 0)),
        out_shape=jax.ShapeDtypeStruct((t, d), F32),
        compiler_params=_cparams(("arbitrary",)),
        name="conv_glu",
    )(xt, w_in, b_in)


def _dwconv_kernel(h_ref, w_ref, b_ref, o_ref, buf, win, *, rows):
    s = pl.program_id(2)
    tm = h_ref.shape[0]
    width = w_ref.shape[0]
    off = CONV_HALO - (width - 1)

    @pl.when(s == 0)
    def _():
        buf[0:CONV_HALO, :] = jnp.zeros((CONV_HALO, buf.shape[1]), F32)

    @pl.when(s > 0)
    def _():
        buf[0:CONV_HALO, :] = buf[tm:tm + CONV_HALO, :]

    buf[CONV_HALO:CONV_HALO + tm, :] = h_ref[...]
    sub = 8
    for c0 in range(0, buf.shape[1], LANES):
        cs = slice(c0, c0 + LANES)
        for r0 in range(0, tm, rows):
            acc = None
            for phase in range(min(sub, width)):
                n_taps = (width - 1 - phase) // sub + 1
                start = r0 + off + phase
                span = rows + sub * (n_taps - 1)
                win[0:span, :] = buf[start:start + span, cs]
                for a in range(n_taps):
                    k = sub * a + phase
                    term = win[sub * a:sub * a + rows, :] * w_ref[k:k + 1, cs]
                    acc = term if acc is None else acc + term
            o_ref[r0:r0 + rows, cs] = acc + b_ref[:, cs]


def _dwconv(h, w_dw, b_dw, b, s, tm, cw):
    t, d = h.shape
    h3 = h.reshape(b, s, d)
    width = w_dw.shape[0]
    out = pl.pallas_call(
        functools.partial(_dwconv_kernel, rows=CONV_ROWS),
        grid=(b, d // cw, s // tm),
        in_specs=[pl.BlockSpec((None, tm, cw), lambda bi, c, si: (bi, si, c)),
                  pl.BlockSpec((width, cw), lambda bi, c, si: (0, c)),
                  pl.BlockSpec((1, cw), lambda bi, c, si: (0, c))],
        out_specs=pl.BlockSpec((None, tm, cw), lambda bi, c, si: (bi, si, c)),
        out_shape=jax.ShapeDtypeStruct((b, s, d), F32),
        scratch_shapes=[pltpu.VMEM((CONV_HALO + tm, cw), F32), pltpu.VMEM((CONV_ROWS + CONV_HALO, LANES), F32)],
        compiler_params=_cparams(("arbitrary", "arbitrary", "arbitrary")),
        name="conv_depthwise",
    )(h3, w_dw, b_dw)
    return out.reshape(t, d)


def _conv_out_kernel(c_ref, x_ref, cg_ref, cb_ref, w_ref, bo_ref, g_ref, b_ref, o_ref, *, alpha):
    u = _ln(c_ref[...], cg_ref[...], cb_ref[...])
    u = u * jax.nn.sigmoid(u)
    mix = jnp.dot(u.astype(BF16), w_ref[...], preferred_element_type=F32) + bo_ref[...]
    o_ref[...] = _ln(alpha * x_ref[...] + mix, g_ref[...], b_ref[...])


def _conv_out(c, xt, cg, cb, w_out, b_out, g1, b1, alpha, tm):
    t, d = xt.shape
    row = pl.BlockSpec((tm, d), lambda i: (i, 0))
    vec = _full((1, d))
    return pl.pallas_call(
        functools.partial(_conv_out_kernel, alpha=alpha),
        grid=(t // tm,),
        in_specs=[row, row, vec, vec, _full((d, d)), vec, vec, vec],
        out_specs=row,
        out_shape=jax.ShapeDtypeStruct((t, d), F32),
        compiler_params=_cparams(("arbitrary",)),
        name="conv_out_ln",
    )(c, xt, cg, cb, w_out, b_out, g1, b1)


def _rope_kernel(pos_ref, invf_ref, c_ref, s_ref):
    ang = pos_ref[...].astype(F32) * invf_ref[...]
    lane = lax.broadcasted_iota(I32, ang.shape, 1)
    half = ROT_DIM // 2
    c_ref[...] = jnp.where(lane < ROT_DIM, jnp.cos(ang), 1.0)
    sn = jnp.sin(ang)
    s_ref[...] = jnp.where(lane < half, -sn, jnp.where(lane < ROT_DIM, sn, 0.0))


def _rope_tables(positions, tm):
    t = positions.size
    half = ROT_DIM // 2
    inv_freq = ROPE_THETA ** (-jnp.arange(half, dtype=F32) * 2.0 / ROT_DIM)
    invf = jnp.zeros((1, LANES), F32).at[0, :half].set(inv_freq).at[0, half:ROT_DIM].set(inv_freq)
    out = jax.ShapeDtypeStruct((t, LANES), F32)
    return pl.pallas_call(
        _rope_kernel,
        grid=(t // tm,),
        in_specs=[pl.BlockSpec((tm, 1), lambda i: (i, 0)), _full((1, LANES))],
        out_specs=[pl.BlockSpec((tm, LANES), lambda i: (i, 0))] * 2,
        out_shape=[out, out],
        compiler_params=_cparams(("arbitrary",)),
        name="rope_tables",
    )(positions.reshape(t, 1), invf)


def _proj_kernel(x_ref, c_ref, s_ref, *rest, rot, heads, dil):
    n_out = len(rot)
    w_refs, o_refs = rest[:n_out], rest[n_out:2 * n_out]
    lhs, tabs, xc = rest[2 * n_out:]
    tm = o_refs[0].shape[1]
    n_lane_tiles = xc.shape[0]
    if dil > 1:
        for c in range(n_lane_tiles):
            xc[c] = x_ref[:, c * LANES:(c + 1) * LANES]
    for r in range(dil):
        rows = slice(r * tm, (r + 1) * tm)
        if dil > 1:
            src = pl.ds(r, tm, stride=dil)
            for c in range(n_lane_tiles):
                lhs[rows, c * LANES:(c + 1) * LANES] = xc[c, src, :].astype(BF16)
        else:
            src = slice(None)
            lhs[rows, :] = x_ref[...].astype(BF16)
        tabs[0, rows, :] = c_ref[src, :]
        tabs[1, rows, :] = s_ref[src, :]
    cos = tabs[0]
    sin = tabs[1]
    lane = lax.broadcasted_iota(I32, cos.shape, 1)
    first = lane < ROT_DIM // 2
    for j, o_ref in enumerate(o_refs):
        y = jnp.dot(lhs[...], w_refs[j][...], preferred_element_type=F32)
        for h in range(heads):
            t = y[:, h * HEAD_DIM:(h + 1) * HEAD_DIM]
            if rot[j]:
                partner = jnp.where(first, pltpu.roll(t, HEAD_DIM - ROT_DIM // 2, 1), pltpu.roll(t, ROT_DIM // 2, 1))
                t = t * cos + partner * sin
            t = t.astype(BF16)
            for r in range(dil):
                o_ref[r, :, h * HEAD_DIM:(h + 1) * HEAD_DIM] = t[r * tm:(r + 1) * tm]


def _project(x3, cos3, sin3, weights, rot, dil, rows, gw):
    b, s, d = x3.shape
    l = s // dil
    tm = rows // dil
    n_out = len(rot)
    out = jax.ShapeDtypeStruct((b, dil, l, gw), BF16)
    tab = pl.BlockSpec((None, rows, LANES), lambda bi, i: (bi, i, 0))
    w_specs = [pl.BlockSpec((d, gw), lambda bi, i, blk=blk: (0, blk)) for _, blk in weights]
    return pl.pallas_call(
        functools.partial(_proj_kernel, rot=rot, heads=gw // HEAD_DIM, dil=dil),
        grid=(b, s // rows),
        in_specs=[pl.BlockSpec((None, rows, d), lambda bi, i: (bi, i, 0)), tab, tab] + w_specs,
        out_specs=[pl.BlockSpec((None, dil, tm, gw), lambda bi, i: (bi, 0, i, 0))] * n_out,
        out_shape=[out] * n_out,
        scratch_shapes=[pltpu.VMEM((rows, d), BF16), pltpu.VMEM((2, rows, LANES), F32),
                        pltpu.VMEM((d // LANES, rows, LANES), F32)],
        compiler_params=_cparams(("arbitrary", "arbitrary")),
        name=f"proj_dil{dil}_n{n_out}",
    )(x3, cos3, sin3, *[w for w, _ in weights])


def _attn_kernel(q_ref, kc_ref, kh_ref, vc_ref, vh_ref, o_ref, lse_ref, kbuf, vbuf, *, heads):
    i = pl.program_id(2)
    tq = q_ref.shape[0]
    blk = ATTN_BLOCK
    kbuf[0:blk, :] = kh_ref[...]
    kbuf[blk:blk + tq, :] = kc_ref[...]
    vbuf[0:blk, :] = vh_ref[...]
    vbuf[blk:blk + tq, :] = vc_ref[...]
    scale = 1.0 / math.sqrt(HEAD_DIM)
    row = lax.broadcasted_iota(I32, (blk, 2 * blk), 0)
    col = lax.broadcasted_iota(I32, (blk, 2 * blk), 1)
    band = jnp.logical_and(col >= row, col <= row + blk)
    first_band = jnp.logical_and(band, jnp.logical_or(col >= blk, i > 0))
    lane = lax.broadcasted_iota(I32, (blk, LANES), 1)
    dims = (((1,), (1,)), ((), ()))
    for n in range(tq // blk):
        rs = slice(n * blk, (n + 1) * blk)
        ks = slice(n * blk, (n + 2) * blk)
        mask = first_band if n == 0 else band
        lse_tile = jnp.zeros((blk, LANES), F32)
        for h in range(heads):
            cs = slice(h * HEAD_DIM, (h + 1) * HEAD_DIM)
            s = lax.dot_general(q_ref[rs, cs], kbuf[ks, cs], dims, preferred_element_type=F32) * scale
            s = jnp.where(mask, s, NEG_INF)
            m = jnp.max(s, axis=1, keepdims=True)
            p = jnp.exp(s - m)
            den = jnp.sum(p, axis=1, keepdims=True)
            o = jnp.dot(p.astype(BF16), vbuf[ks, cs], preferred_element_type=F32)
            o_ref[rs, cs] = (o / den).astype(o_ref.dtype)
            lse_tile = jnp.where(lane == h, m + jnp.log(den), lse_tile)
        lse_ref[rs, :] = lse_tile


def _attention(q, k, v, dil, tq):
    b, _, l, gw = q.shape
    nb = tq // ATTN_BLOCK
    cur = pl.BlockSpec((None, None, tq, gw), lambda bi, r, i: (bi, r, i, 0))
    halo = pl.BlockSpec((None, None, ATTN_BLOCK, gw), lambda bi, r, i: (bi, r, jnp.maximum(i * nb - 1, 0), 0))
    o, lse = pl.pallas_call(
        functools.partial(_attn_kernel, heads=gw // HEAD_DIM),
        grid=(b, dil, l // tq),
        in_specs=[cur, cur, halo, cur, halo],
        out_specs=[pl.BlockSpec((None, None, tq, gw), lambda bi, r, i: (bi, r, i, 0)),
                   pl.BlockSpec((None, None, tq, LANES), lambda bi, r, i: (bi, r, i, 0))],
        out_shape=[jax.ShapeDtypeStruct((b, dil, l, gw), BF16),
                   jax.ShapeDtypeStruct((b, dil, l, LANES), F32)],
        scratch_shapes=[pltpu.VMEM((ATTN_BLOCK + tq, gw), BF16), pltpu.VMEM((ATTN_BLOCK + tq, gw), BF16)],
        compiler_params=_cparams(("arbitrary", "arbitrary", "arbitrary")),
        name=f"attn_dil{dil}",
    )(q, k, k, v, v)
    return o, lse


def _attn_out_kernel(o0_ref, o1_ref, o2_ref, l0_ref, l1_ref, l2_ref, x_ref, w_ref, g_ref, b_ref,
                     out_ref, onat, lnat, mixed, *, alpha, heads, dils):
    tm = x_ref.shape[0]
    for gi, (o_ref, l_ref) in enumerate(((o0_ref, l0_ref), (o1_ref, l1_ref), (o2_ref, l2_ref))):
        dil = dils[gi]
        for r in range(dil):
            dst = pl.ds(r, tm // dil, stride=dil) if dil > 1 else slice(None)
            for h in range(heads):
                onat[gi * heads + h, dst, :] = o_ref[r, :, h * HEAD_DIM:(h + 1) * HEAD_DIM].astype(F32)
            lnat[gi, dst, :] = l_ref[r]
    l0, l1, l2 = lnat[0], lnat[1], lnat[2]
    m = jnp.maximum(jnp.maximum(l0, l1), l2)
    e0, e1, e2 = jnp.exp(l0 - m), jnp.exp(l1 - m), jnp.exp(l2 - m)
    den = e0 + e1 + e2
    w0, w1, w2 = e0 / den, e1 / den, e2 / den
    for h in range(heads):
        cs = slice(h * HEAD_DIM, (h + 1) * HEAD_DIM)
        o = w0[:, h:h + 1] * onat[h] + w1[:, h:h + 1] * onat[heads + h] + w2[:, h:h + 1] * onat[2 * heads + h]
        mixed[:, cs] = o.astype(BF16)
    mix = jnp.dot(mixed[...], w_ref[...], preferred_element_type=F32)
    out_ref[...] = _ln(alpha * x_ref[...] + mix, g_ref[...], b_ref[...])


def _attn_out(outs, lses, x3, w_o, g1, b1, alpha, tm):
    b, s, d = x3.shape
    gw = w_o.shape[0]
    dils = tuple(o.shape[1] for o in outs)
    ospecs = [pl.BlockSpec((None, dil, tm // dil, gw), lambda bi, i: (bi, 0, i, 0)) for dil in dils]
    lspecs = [pl.BlockSpec((None, dil, tm // dil, LANES), lambda bi, i: (bi, 0, i, 0)) for dil in dils]
    xrow = pl.BlockSpec((None, tm, d), lambda bi, i: (bi, i, 0))
    n = len(dils)
    return pl.pallas_call(
        functools.partial(_attn_out_kernel, alpha=alpha, heads=gw // HEAD_DIM, dils=dils),
        grid=(b, s // tm),
        in_specs=ospecs + lspecs + [xrow, _full((gw, d)), _full((1, d)), _full((1, d))],
        out_specs=xrow,
        out_shape=jax.ShapeDtypeStruct((b, s, d), F32),
        scratch_shapes=[pltpu.VMEM((n * gw // HEAD_DIM, tm, HEAD_DIM), F32), pltpu.VMEM((n, tm, LANES), F32),
                        pltpu.VMEM((tm, gw), BF16)],
        compiler_params=_cparams(("arbitrary", "arbitrary")),
        name="attn_out_ln",
    )(*outs, *lses, x3, w_o, g1, b1)


def _pack_bf16_pairs(v, holds_bf16=False):
    half = v.shape[1] // 2
    first, second = v[:, :half], v[:, half:]
    if not holds_bf16:
        first, second = first.astype(BF16).astype(F32), second.astype(BF16).astype(F32)
    return lax.bitcast_convert_type(first, U32) | (lax.bitcast_convert_type(second, U32) >> 16)


def _unpack_bf16_pairs(p):
    first = lax.bitcast_convert_type(p & jnp.uint32(0xFFFF0000), F32)
    second = lax.bitcast_convert_type(p << 16, F32)
    return first, second


def _router_kernel(x_ref, rwt_ref, rb_ref, lp_ref, lpt_ref, gate_ref, cnt_ref, base_ref, tot_ref, carry):
    @pl.when(pl.program_id(0) == 0)
    def _():
        carry[...] = jnp.zeros(carry.shape, F32)

    tm = x_ref.shape[0]
    ne = rwt_ref.shape[0]
    dims = (((1,), (1,)), ((), ()))
    w = rwt_ref[...]
    x = x_ref[...]
    w_hi = w.astype(BF16)
    w_lo = (w - w_hi.astype(F32)).astype(BF16)
    x_hi = x.astype(BF16)
    x_lo = (x - x_hi.astype(F32)).astype(BF16)
    logits = (lax.dot_general(w_hi, x_hi, dims, preferred_element_type=F32)
              + (lax.dot_general(w_lo, x_hi, dims, preferred_element_type=F32)
                 + lax.dot_general(w_hi, x_lo, dims, preferred_element_type=F32)))
    scores = jax.nn.sigmoid(logits)
    sel = scores + rb_ref[...]
    eidx = lax.broadcasted_iota(I32, (ne, tm), 0)
    chosen = jnp.zeros((ne, tm), F32)
    gsum = jnp.zeros((1, tm), F32)
    picks = []
    for _ in range(TOP_K):
        best = jnp.max(sel, axis=0, keepdims=True)
        j = jnp.min(jnp.where(sel == best, eidx, ne), axis=0, keepdims=True)
        onehot = eidx == j
        sc = jnp.sum(jnp.where(onehot, scores, 0.0), axis=0, keepdims=True)
        sel = jnp.where(onehot, -jnp.inf, sel)
        chosen = chosen + onehot.astype(F32)
        gsum = gsum + sc
        picks.append((onehot, sc))
    r = lax.broadcasted_iota(I32, (tm, tm), 0)
    c = lax.broadcasted_iota(I32, (tm, tm), 1)
    rank = jnp.dot(chosen.astype(BF16), (r < c).astype(BF16), preferred_element_type=F32)
    cnt = jnp.sum(chosen, axis=1, keepdims=True)
    cnt_al = jnp.maximum(jnp.floor((cnt + (SEG_ALIGN - 1)) / SEG_ALIGN), 1.0) * SEG_ALIGN
    er = lax.broadcasted_iota(I32, (ne, ne), 0)
    ec = lax.broadcasted_iota(I32, (ne, ne), 1)
    seg_off = jnp.dot((ec < er).astype(BF16), jnp.broadcast_to(cnt_al, (ne, LANES)).astype(BF16),
                      preferred_element_type=F32)[:, 0:1]
    slot = rank + seg_off
    krow = lax.broadcasted_iota(I32, (TOP_K, tm), 0)
    lpt = jnp.zeros((TOP_K, tm), I32)
    gt = jnp.zeros((TOP_K, tm), F32)
    for k, (onehot, sc) in enumerate(picks):
        row = jnp.sum(jnp.where(onehot, slot, 0.0), axis=0, keepdims=True)
        lpt = jnp.where(krow == k, row.astype(I32), lpt)
        gt = jnp.where(krow == k, sc / gsum * ROUTED_SCALE, gt)
    lpt_ref[...] = lpt
    lp_ref[...] = jnp.concatenate([lpt, jnp.zeros((LANES - TOP_K, tm), I32)], axis=0).T
    gate_ref[...] = jnp.concatenate([gt, jnp.zeros((LANES - TOP_K, tm), F32)], axis=0).T
    cnt_ref[...] = cnt_al
    base_ref[...] = carry[...]
    carry[...] = carry[...] + cnt_al
    tot_ref[...] = carry[...]


def _router(xt, router_w, router_bias, tm):
    t, d = xt.shape
    ne = router_w.shape[1]
    n_tiles = t // tm
    wide = pl.BlockSpec((tm, LANES), lambda i: (i, 0))
    per_tile = pl.BlockSpec((None, ne, 1), lambda i: (i, 0, 0))
    return pl.pallas_call(
        _router_kernel,
        grid=(n_tiles,),
        in_specs=[pl.BlockSpec((tm, d), lambda i: (i, 0)), _full((ne, d)), _full((ne, 1))],
        out_specs=[wide, pl.BlockSpec((None, TOP_K, tm), lambda i: (i, 0, 0)), wide, per_tile, per_tile, _full((ne, 1))],
        out_shape=[jax.ShapeDtypeStruct((t, LANES), I32), jax.ShapeDtypeStruct((n_tiles, TOP_K, tm), I32),
                   jax.ShapeDtypeStruct((t, LANES), F32), jax.ShapeDtypeStruct((n_tiles, ne, 1), F32),
                   jax.ShapeDtypeStruct((n_tiles, ne, 1), F32), jax.ShapeDtypeStruct((ne, 1), F32)],
        scratch_shapes=[pltpu.VMEM((ne, 1), F32)],
        compiler_params=_cparams(("arbitrary",)),
        name="moe_router",
    )(xt, router_w.T, router_bias.reshape(ne, 1))


def _start_segments(tbl, base, ne, make_copy):
    off = 0
    for e in range(ne):
        n = pl.multiple_of(tbl[base + e], SEG_ALIGN)
        g = pl.multiple_of(tbl[base + ne + e], SEG_ALIGN)
        make_copy(off, g, n).start()
        off = pl.multiple_of(off + n, SEG_ALIGN)


def _for_sorted_chunks(chunk, n_chunks, max_chunks):
    for ci in range(max_chunks - 1):
        chunk(ci)

    @pl.when(n_chunks == max_chunks)
    def _():
        chunk(max_chunks - 1)


def _start(cp):
    cp.start()


def _wait(cp):
    cp.wait()


def _dispatch_kernel(fs_ref, fn_ref, tbl, lpt_ref, x_ref, xs_hbm, sorted_buf, zbuf, sem, *, ne):
    i = pl.program_id(0)
    last = pl.num_programs(0) - 1
    slot = lax.rem(i, 2)
    tt = x_ref.shape[0]
    width = _table_width(ne)

    @pl.when(i == 0)
    def _():
        zbuf[...] = jnp.zeros(zbuf.shape, zbuf.dtype)

        def visit(action):
            def body(e, carry):
                n = pl.multiple_of(fn_ref[e], SEG_ALIGN)
                s = pl.multiple_of(fs_ref[e], SEG_ALIGN)

                @pl.when(n > 0)
                def _():
                    action(pltpu.make_async_copy(zbuf.at[pl.ds(0, n), :], xs_hbm.at[pl.ds(s, n), :], sem.at[0]))

                return carry
            return body

        lax.fori_loop(0, ne, visit(_start), 0)
        lax.fori_loop(0, ne, visit(_wait), 0)

    xb = x_ref[...].astype(BF16)
    lpt = lpt_ref[...]
    n_chunks = lax.shift_right_logical(tbl[i * width + 2 * ne] + (SORT_CHUNK - 1), SORT_CHUNK.bit_length() - 1)

    def chunk(ci):
        j0 = ci * SORT_CHUNK
        rows = j0 + lax.broadcasted_iota(I32, (SORT_CHUNK, tt), 0)
        hit = lpt[0:1, :] == rows
        for k in range(1, TOP_K):
            hit = jnp.logical_or(hit, lpt[k:k + 1, :] == rows)
        srt = jnp.dot(jnp.where(hit, 1.0, 0.0).astype(BF16), xb, preferred_element_type=F32)
        sorted_buf[slot, j0:j0 + SORT_CHUNK, :] = _pack_bf16_pairs(srt, holds_bf16=True)

    _for_sorted_chunks(chunk, n_chunks, sorted_buf.shape[1] // SORT_CHUNK)

    def writes(sl):
        return lambda off, g, n: pltpu.make_async_copy(
            sorted_buf.at[sl, pl.ds(off, n), :], xs_hbm.at[pl.ds(g, n), :], sem.at[sl])

    _start_segments(tbl, i * width, ne, writes(slot))

    def wait_all(step, sl):
        tot = pl.multiple_of(tbl[step * width + 2 * ne], SEG_ALIGN)
        pltpu.make_async_copy(sorted_buf.at[sl, pl.ds(0, tot), :], xs_hbm.at[pl.ds(0, tot), :], sem.at[sl]).wait()

    @pl.when(i > 0)
    def _():
        wait_all(i - 1, 1 - slot)

    @pl.when(i == last)
    def _():
        wait_all(i, slot)


def _sorted_rows(tt, ne):
    rows = -(-(tt * TOP_K + ne * SEG_ALIGN) // SORT_CHUNK) * SORT_CHUNK
    assert rows < 2 ** 16
    return rows


def _table_width(ne):
    return 2 * ne + SEG_ALIGN


def _dispatch(xt, lpt, table, fill_start, fill_n, n_rows, tt, br):
    t, d = xt.shape
    ne = fill_start.shape[0]
    grid_spec = pltpu.PrefetchScalarGridSpec(
        num_scalar_prefetch=3,
        grid=(t // tt,),
        in_specs=[pl.BlockSpec((None, TOP_K, tt), lambda i, *_: (i, 0, 0)),
                  pl.BlockSpec((tt, d), lambda i, *_: (i, 0))],
        out_specs=pl.BlockSpec(memory_space=pl.ANY),
        scratch_shapes=[pltpu.VMEM((2, _sorted_rows(tt, ne), d // 2), U32),
                        pltpu.VMEM((br, d // 2), U32), pltpu.SemaphoreType.DMA((2,))],
    )
    return pl.pallas_call(
        functools.partial(_dispatch_kernel, ne=ne),
        grid_spec=grid_spec,
        out_shape=jax.ShapeDtypeStruct((n_rows, d // 2), U32),
        compiler_params=_cparams(("arbitrary",)),
        name="moe_dispatch",
    )(fill_start, fill_n, table, lpt, xt)


EXPERT_RING = 4


def _expert_kernel(be_ref, nu_ref, xs_hbm, wg_ref, wu_ref, wd_ref, y_hbm, xbuf, ybuf, wg_b, wu_b, wd_b, sem_in, sem_out):
    i = pl.program_id(0)
    nu = nu_ref[0]
    live = i < nu
    br = xbuf.shape[1]

    def read(blk):
        return pltpu.make_async_copy(xs_hbm.at[pl.ds(pl.multiple_of(blk * br, br), br), :],
                                     xbuf.at[lax.rem(blk, EXPERT_RING)], sem_in.at[lax.rem(blk, EXPERT_RING)])

    def write(blk):
        return pltpu.make_async_copy(ybuf.at[lax.rem(blk, EXPERT_RING)],
                                     y_hbm.at[pl.ds(pl.multiple_of(blk * br, br), br), :],
                                     sem_out.at[lax.rem(blk, EXPERT_RING)])

    @pl.when(i == 0)
    def _():
        for ahead in range(EXPERT_RING - 1):
            @pl.when(ahead < nu)
            def _():
                read(ahead).start()

    @pl.when(i + (EXPERT_RING - 1) < nu)
    def _():
        read(i + (EXPERT_RING - 1)).start()

    @pl.when(jnp.logical_and(live, jnp.logical_or(i == 0, be_ref[i] != be_ref[jnp.maximum(i - 1, 0)])))
    def _():
        wg_b[...] = wg_ref[...].astype(BF16)
        wu_b[...] = wu_ref[...].astype(BF16)
        wd_b[...] = wd_ref[...].astype(BF16)

    @pl.when(live)
    def _():
        slot = lax.rem(i, EXPERT_RING)
        read(i).wait()

        @pl.when(i >= EXPERT_RING)
        def _():
            write(i - EXPERT_RING).wait()

        first, second = _unpack_bf16_pairs(xbuf[slot])
        xb = jnp.concatenate([first.astype(BF16), second.astype(BF16)], axis=1)
        g = jnp.dot(xb, wg_b[...], preferred_element_type=F32)
        u = jnp.dot(xb, wu_b[...], preferred_element_type=F32)
        h = g * jax.nn.sigmoid(g) * u
        y = jnp.dot(h.astype(BF16), wd_b[...], preferred_element_type=F32)
        ybuf[slot] = _pack_bf16_pairs(y)
        write(i).start()

        @pl.when(i == nu - 1)
        def _():
            for back in range(EXPERT_RING):
                @pl.when(i - back >= 0)
                def _():
                    write(i - back).wait()


def _experts(xs, blk_exp, n_used, w_gate, w_up, w_down, layer, br):
    p, dh = xs.shape
    d, f = w_gate.shape[2:]
    n_blocks = p // br

    def w_map(i, be, nu):
        return (layer, be[jnp.minimum(i, nu[0] - 1)], 0, 0)

    grid_spec = pltpu.PrefetchScalarGridSpec(
        num_scalar_prefetch=2,
        grid=(n_blocks,),
        in_specs=[pl.BlockSpec(memory_space=pl.ANY),
                  pl.BlockSpec((None, None, d, f), w_map),
                  pl.BlockSpec((None, None, d, f), w_map),
                  pl.BlockSpec((None, None, f, d), w_map)],
        out_specs=pl.BlockSpec(memory_space=pl.ANY),
        scratch_shapes=[pltpu.VMEM((EXPERT_RING, br, dh), U32), pltpu.VMEM((EXPERT_RING, br, dh), U32),
                        pltpu.VMEM((d, f), BF16), pltpu.VMEM((d, f), BF16), pltpu.VMEM((f, d), BF16),
                        pltpu.SemaphoreType.DMA((EXPERT_RING,)), pltpu.SemaphoreType.DMA((EXPERT_RING,))],
    )
    return pl.pallas_call(
        _expert_kernel,
        grid_spec=grid_spec,
        out_shape=jax.ShapeDtypeStruct((p, dh), U32),
        compiler_params=_cparams(("arbitrary",)),
        name="moe_experts",
    )(blk_exp, n_used, xs, w_gate, w_up, w_down)


def _combine_kernel(tbl, y_hbm, lp_ref, gate_ref, x_ref, sg_ref, su_ref, sd_ref, g_ref, b_ref, o_ref,
                    ysorted, acc, lpb, gb, sem, *, ne, alpha):
    i = pl.program_id(0)
    last = pl.num_programs(0) - 1
    slot = lax.rem(i, 2)
    tt = x_ref.shape[0]
    half = x_ref.shape[1] // 2
    width = _table_width(ne)

    def gathers(sl):
        return lambda off, g, n: pltpu.make_async_copy(
            y_hbm.at[pl.ds(g, n), :], ysorted.at[sl, pl.ds(off, n), :], sem.at[sl])

    @pl.when(i == 0)
    def _():
        _start_segments(tbl, 0, ne, gathers(0))

    nxt = jnp.minimum(i + 1, last)
    _start_segments(tbl, nxt * width, ne, gathers(1 - slot))

    x = x_ref[...]
    xb = x.astype(BF16)
    g = jnp.dot(xb, sg_ref[...], preferred_element_type=F32)
    u = jnp.dot(xb, su_ref[...], preferred_element_type=F32)
    h = g * jax.nn.sigmoid(g) * u
    acc[...] = jnp.dot(h.astype(BF16), sd_ref[...], preferred_element_type=F32)
    gate_bits = lax.bitcast_convert_type(gate_ref[...].astype(BF16).astype(F32), U32)
    word = gate_bits | lax.bitcast_convert_type(lp_ref[...], U32)
    for k in range(TOP_K):
        wb = jnp.broadcast_to(word[:, k:k + 1], (tt, LANES))
        lpb[k] = lax.bitcast_convert_type(wb & jnp.uint32(0xFFFF), I32)
        gb[k] = lax.bitcast_convert_type(wb & jnp.uint32(0xFFFF0000), F32)
    total = pl.multiple_of(tbl[i * width + 2 * ne], SEG_ALIGN)
    pltpu.make_async_copy(y_hbm.at[pl.ds(0, total), :], ysorted.at[slot, pl.ds(0, total), :], sem.at[slot]).wait()
    n_chunks = lax.shift_right_logical(total + (SORT_CHUNK - 1), SORT_CHUNK.bit_length() - 1)
    lane = lax.broadcasted_iota(I32, (tt, LANES), 1)

    def chunk(ci):
        j0 = ci * SORT_CHUNK
        parts = []
        for c0 in range(0, SORT_CHUNK, LANES):
            cols = lane + (j0 + c0)
            gm = jnp.zeros((tt, LANES), F32)
            for k in range(TOP_K):
                gm = jnp.where(lpb[k] == cols, gb[k], gm)
            parts.append(gm)
        gmat = jnp.concatenate(parts, axis=1).astype(BF16)
        rows = j0 + lax.broadcasted_iota(I32, (SORT_CHUNK, half), 0)
        ys = jnp.where(rows < total, ysorted[slot, j0:j0 + SORT_CHUNK, :], jnp.uint32(0))
        first, second = _unpack_bf16_pairs(ys)
        first = first.astype(BF16)
        second = second.astype(BF16)
        acc[:, :half] += jnp.dot(gmat, first, preferred_element_type=F32)
        acc[:, half:] += jnp.dot(gmat, second, preferred_element_type=F32)

    _for_sorted_chunks(chunk, n_chunks, ysorted.shape[1] // SORT_CHUNK)
    o_ref[...] = _ln(alpha * x + acc[...], g_ref[...], b_ref[...])

    @pl.when(i == last)
    def _():
        pltpu.make_async_copy(y_hbm.at[pl.ds(0, total), :], ysorted.at[1 - slot, pl.ds(0, total), :],
                              sem.at[1 - slot]).wait()


def _combine(table, y, lp, gate, xt, sw_gate, sw_up, sw_down, g2, b2, alpha, tt, ne):
    t, d = xt.shape
    f = sw_gate.shape[1]
    row = pl.BlockSpec((tt, d), lambda i, *_: (i, 0))
    wide = pl.BlockSpec((tt, LANES), lambda i, *_: (i, 0))
    grid_spec = pltpu.PrefetchScalarGridSpec(
        num_scalar_prefetch=1,
        grid=(t // tt,),
        in_specs=[pl.BlockSpec(memory_space=pl.ANY), wide, wide, row,
                  _full((d, f)), _full((d, f)), _full((f, d)), _full((1, d)), _full((1, d))],
        out_specs=row,
        scratch_shapes=[pltpu.VMEM((2, _sorted_rows(tt, ne), d // 2), U32),
                        pltpu.VMEM((tt, d), F32), pltpu.VMEM((TOP_K, tt, LANES), I32), pltpu.VMEM((TOP_K, tt, LANES), F32),
                        pltpu.SemaphoreType.DMA((2,))],
    )
    return pl.pallas_call(
        functools.partial(_combine_kernel, ne=ne, alpha=alpha),
        grid_spec=grid_spec,
        out_shape=jax.ShapeDtypeStruct((t, d), F32),
        compiler_params=_cparams(("arbitrary",)),
        name="moe_combine_ln",
    )(table, y, lp, gate, xt, sw_gate, sw_up, sw_down, g2, b2)


def _moe_layer(xt, router_w, router_bias, w_gate, w_up, w_down, layer, sw_gate, sw_up, sw_down, g2, b2, alpha, cfg):
    t, d = xt.shape
    ne = router_w.shape[1]
    br = cfg["moe_rows"]
    tt = cfg["moe_tile"]
    n_tiles = t // tt
    lp, lpt, gate, cnt, base, tot = _router(xt, router_w, router_bias, tt)
    cnt = cnt.reshape(n_tiles, ne).astype(I32)
    tot = tot.reshape(ne).astype(I32)
    padded = (tot + br - 1) // br * br
    pends = jnp.cumsum(padded)
    pstarts = pends - padded
    seg_start = pstarts[None, :] + base.reshape(n_tiles, ne).astype(I32)
    tile_rows = jnp.sum(cnt, axis=1, keepdims=True)
    table = jnp.concatenate([cnt, seg_start, tile_rows, jnp.zeros((n_tiles, SEG_ALIGN - 1), I32)], axis=1).reshape(-1)
    n_blocks = -(-(t * TOP_K + ne * n_tiles * SEG_ALIGN) // br) + ne
    blk_start = jnp.arange(n_blocks, dtype=I32) * br
    blk_exp = jnp.minimum(jnp.sum(pends[None, :] <= blk_start[:, None], axis=1), ne - 1).astype(I32)
    n_used = (pends[-1:] // br).astype(I32)
    xs = _dispatch(xt, lpt, table, pstarts + tot, padded - tot, n_blocks * br, tt, br)
    y = _experts(xs, blk_exp, n_used, w_gate, w_up, w_down, layer, br)
    return _combine(table, y, lp, gate, xt, sw_gate.astype(BF16), sw_up.astype(BF16), sw_down.astype(BF16),
                    g2, b2, alpha, tt, ne)


def _config(b, s, d):
    return dict(row_tm=min(512, s), conv_tm=min(1024, s), conv_cw=min(512, d), proj_rows=min(512, s), attn_out_tm=512,
                moe_rows=1024, moe_tile=min(256, b * s))


def kernel(x, positions, ln1_g, ln1_b, ln2_g, ln2_b, conv_w_in, conv_b_in, conv_w_dw, conv_b_dw, conv_ln_g, conv_ln_b, conv_w_out, conv_b_out, w_kv, attn_w_q, attn_w_o, router_w, router_bias, exp_w_gate, exp_w_up, exp_w_down, sh_w_gate, sh_w_up, sh_w_down):
    b, s, d = x.shape
    t = b * s
    depth = ln1_g.shape[0]
    n_conv = conv_w_in.shape[0]
    alpha = (2.0 * depth) ** 0.25
    cfg = _config(b, s, d)
    n_groups = len(WINDOW_DILATIONS)
    gw = attn_w_o.shape[1]
    vec = lambda a: a.reshape(1, -1)

    xt = x.reshape(t, d)
    kv = None
    tables = None
    for layer in range(depth):
        g1, b1 = vec(ln1_g[layer]), vec(ln1_b[layer])
        if layer < n_conv:
            h = _glu(xt, conv_w_in[layer].astype(BF16), vec(conv_b_in[layer]), cfg["row_tm"])
            c = _dwconv(h, conv_w_dw[layer], vec(conv_b_dw[layer]), b, s, cfg["conv_tm"], cfg["conv_cw"])
            xt = _conv_out(c, xt, vec(conv_ln_g[layer]), vec(conv_ln_b[layer]), conv_w_out[layer].astype(BF16),
                           vec(conv_b_out[layer]), g1, b1, alpha, cfg["row_tm"])
        else:
            j = layer - n_conv
            if tables is None:
                cos, sin = _rope_tables(positions, cfg["row_tm"])
                tables = (cos.reshape(b, s, LANES), sin.reshape(b, s, LANES))
            x3 = xt.reshape(b, s, d)
            wq = attn_w_q[j].astype(BF16)
            outs, lses = [], []
            new_kv = []
            for g, (win, dil) in enumerate(WINDOW_DILATIONS):
                assert win // dil == ATTN_BLOCK
                if kv is None:
                    wkv = w_kv.astype(BF16)
                    q, kg, vg = _project(x3, *tables, [(wq, g), (wkv, g), (wkv, n_groups + g)], (True, True, False),
                                         dil, cfg["proj_rows"], gw)
                    new_kv.append((kg, vg))
                else:
                    (q,) = _project(x3, *tables, [(wq, g)], (True,), dil, cfg["proj_rows"], gw)
                    kg, vg = kv[g]
                o, lse = _attention(q, kg, vg, dil, min(1024, s // dil))
                outs.append(o)
                lses.append(lse)
            if kv is None:
                kv = new_kv
            xt = _attn_out(outs, lses, x3, attn_w_o[j].astype(BF16), g1, b1, alpha, cfg["attn_out_tm"]).reshape(t, d)
        xt = _moe_layer(xt, router_w[layer], router_bias[layer], exp_w_gate, exp_w_up, exp_w_down, layer,
                        sh_w_gate[layer], sh_w_up[layer], sh_w_down[layer],
                        vec(ln2_g[layer]), vec(ln2_b[layer]), alpha, cfg)
    return xt.reshape(b, s, d)
```
